```python
import math
import jax, jax.numpy as jnp
from jax import lax
import numpy as np

D_MODEL = 1024
BATCH = 16
SEQ = 2048
DEPTH = 4
DEC_BATCH = 4
DEC_SEQ = 8192
PAST_LEN = 128

HEAD_DIM = 64
GRID_W = 64
N_MEM = 256
A_HEADS = 8
A_PATTERNS = ((128, 1), (512, 4), (2048, 16))
B_HEADS = 8
NA_KH_MAX = 8
NA_KW = 16
C_Q_HEADS = 16
C_KV_HEADS = 4
Q_BLOCK = 128
ROPE_THETA = 10000.0
N_BUCKETS = 32
MAX_DISTANCE = 1024
X_HEADS = 4
D_FF = 2816
N_EXPERTS = 8
TOP_K = 2
D_FF_EXPERT = 3584
EPS = 1e-6
NEG = -1e30

N_EVEN = (DEPTH + 1) // 2
N_ODD = DEPTH // 2
A_W = A_HEADS * HEAD_DIM
B_W = B_HEADS * HEAD_DIM
EVEN_IN = 3 * (A_W + B_W)
EVEN_OUT = A_W + B_W
C_QW = C_Q_HEADS * HEAD_DIM
C_KW = C_KV_HEADS * HEAD_DIM
ODD_IN = C_QW + 2 * C_KW
X_W = X_HEADS * HEAD_DIM

kernel_name = "hybrid_dilated_natten_gqa_encoder"


def rms_norm(x, g):
    xf = x.astype(jnp.float32)
    y = xf * lax.rsqrt(jnp.mean(xf * xf, axis=-1, keepdims=True) + EPS)
    return (y * g.astype(jnp.float32)).astype(x.dtype)


def swiglu(h, w_gate, w_up, w_down):
    return (jax.nn.silu(h @ w_gate) * (h @ w_up)) @ w_down


def t5_bucket(rel):
    nb = N_BUCKETS // 2
    max_exact = nb // 2
    ret = jnp.where(rel > 0, nb, 0)
    n = jnp.abs(rel)
    large = max_exact + (jnp.log(jnp.maximum(n, 1).astype(jnp.float32) / max_exact)
                         / math.log(MAX_DISTANCE / max_exact) * (nb - max_exact)).astype(jnp.int32)
    large = jnp.minimum(large, nb - 1)
    return ret + jnp.where(n < max_exact, n, large)


def dilated_window_attention(q, k, v, dil, half, rel_table):
    B, S, H, hd = q.shape
    L = S // dil
    nb = -(-L // half)
    Lp = nb * half

    def sub(x):
        x = x.reshape(B, L, dil, H, hd)
        return jnp.pad(x, ((0, 0), (0, Lp - L), (0, 0), (0, 0), (0, 0)))

    qs = sub(q).reshape(B, nb, half, dil, H, hd)

    def kblocks(x):
        xp = jnp.pad(sub(x), ((0, 0), (half, half), (0, 0), (0, 0), (0, 0)))
        return jnp.concatenate(
            [xp[:, c * half: c * half + Lp].reshape(B, nb, half, dil, H, hd) for c in range(3)], axis=2)

    kb, vb = kblocks(k), kblocks(v)
    off = np.arange(3 * half)[None, :] - half - np.arange(half)[:, None]
    bias = rel_table[t5_bucket(jnp.asarray(off * dil, dtype=jnp.int32))].transpose(2, 0, 1)
    kidx = np.arange(nb)[:, None] * half + np.arange(3 * half)[None, :] - half
    valid = (np.abs(off) <= half)[None] & ((kidx >= 0) & (kidx < L))[:, None, :]

    s = jnp.einsum('bnqrhd,bnkrhd->bnrhqk', qs, kb).astype(jnp.float32) * (hd ** -0.5)
    s = s + bias.astype(jnp.float32)[None, None, None]
    s = jnp.where(jnp.asarray(valid)[None, :, None, None], s, NEG)
    m = jnp.max(s, axis=-1, keepdims=True)
    p = jnp.exp(s - m)
    den = jnp.sum(p, axis=-1, keepdims=True)
    o = jnp.einsum('bnrhqk,bnkrhd->bnqrhd', (p / den).astype(v.dtype), vb)
    lse = (m + jnp.log(den))[..., 0]
    o = o.reshape(B, Lp, dil, H, hd)[:, :L].reshape(B, S, H, hd)
    lse = lse.transpose(0, 1, 4, 2, 3).reshape(B, Lp, dil, H)[:, :L].reshape(B, S, H)
    return o, lse


def mixture_of_dilations(q, k, v, rel_table):
    outs, lses = [], []
    for window, dil in A_PATTERNS:
        o, l = dilated_window_attention(q, k, v, dil, window // (2 * dil), rel_table)
        outs.append(o)
        lses.append(l)
    w = jax.nn.softmax(jnp.stack(lses, axis=0), axis=0)
    return jnp.einsum('gbsh,gbshd->bshd', w.astype(q.dtype), jnp.stack(outs, axis=0))


def neighborhood_attention(q, k, v, rpb):
    B, S, H, hd = q.shape
    rows = S // GRID_W
    kh = min(NA_KH_MAX, rows)
    r = np.arange(rows)
    rstart = np.clip(r - kh // 2, 0, rows - kh)
    krow = rstart[:, None] + np.arange(kh)[None, :]
    dr = krow - r[:, None]
    c = np.arange(GRID_W)
    cstart = np.clip(c - NA_KW // 2, 0, GRID_W - NA_KW)
    dc = c[None, :] - c[:, None]
    col_valid = (c[None, :] >= cstart[:, None]) & (c[None, :] < cstart[:, None] + NA_KW)
    dr_i = dr + NA_KH_MAX - 1
    dc_i = np.clip(dc, -(NA_KW - 1), NA_KW - 1) + NA_KW - 1

    qg = q.reshape(B, rows, GRID_W, H, hd)
    kg = k.reshape(B, rows, GRID_W, H, hd)[:, krow]
    vg = v.reshape(B, rows, GRID_W, H, hd)[:, krow]
    bias = rpb[:, dr_i[:, None, :, None], dc_i[None, :, None, :]]
    bias = bias.transpose(1, 0, 2, 3, 4).astype(jnp.float32)
    s = jnp.einsum('brqhd,brjkhd->brhqjk', qg, kg).astype(jnp.float32) * (hd ** -0.5) + bias[None]
    s = jnp.where(jnp.asarray(col_valid)[:, None, :], s, NEG)
    p = jax.nn.softmax(s, axis=(-2, -1))
    o = jnp.einsum('brhqjk,brjkhd->brqhd', p.astype(v.dtype), vg)
    return o.reshape(B, S, H, hd)


def axial_rope(x):
    S = x.shape[1]
    t = jnp.arange(S)
    half_dim = HEAD_DIM // 2
    freqs = ROPE_THETA ** (-jnp.arange(0, half_dim, 2, dtype=jnp.float32) / half_dim)

    def rot(xa, pos):
        ang = pos.astype(jnp.float32)[:, None] * freqs[None, :]
        cos = jnp.cos(ang)[None, :, None, :]
        sin = jnp.sin(ang)[None, :, None, :]
        x1, x2 = xa[..., :half_dim // 2], xa[..., half_dim // 2:]
        return jnp.concatenate([x1 * cos - x2 * sin, x1 * sin + x2 * cos], axis=-1)

    y = jnp.concatenate([rot(x[..., :half_dim], t // GRID_W), rot(x[..., half_dim:], t % GRID_W)], axis=-1)
    return y.astype(x.dtype)


def gqa_blocked(q, k, v):
    B, S, Hq, hd = q.shape
    G = Hq // C_KV_HEADS
    nblk = S // Q_BLOCK
    qb = q.reshape(B, nblk, Q_BLOCK, C_KV_HEADS, G, hd).transpose(1, 0, 2, 3, 4, 5)

    def attend(qblk):
        s = jnp.einsum('bqkgd,bskd->bkgqs', qblk, k).astype(jnp.float32) * (hd ** -0.5)
        p = jax.nn.softmax(s, axis=-1)
        return jnp.einsum('bkgqs,bskd->bqkgd', p.astype(v.dtype), v)

    o = lax.map(attend, qb)
    return o.transpose(1, 0, 2, 3, 4, 5).reshape(B, S, Hq, hd)


def memory_cross_attention(h, mem, w_q, w_kv, w_out):
    B, S, _ = h.shape
    M = mem.shape[1]
    q = (h @ w_q).reshape(B, S, X_HEADS, HEAD_DIM)
    kv = (mem @ w_kv).reshape(B, M, 2, X_HEADS, HEAD_DIM)
    k, v = kv[:, :, 0], kv[:, :, 1]
    s = jnp.einsum('bshd,bmhd->bhsm', q, k).astype(jnp.float32) * (HEAD_DIM ** -0.5)
    p = jax.nn.softmax(s, axis=-1)
    o = jnp.einsum('bhsm,bmhd->bshd', p.astype(v.dtype), v).reshape(B, S, X_W)
    return o @ w_out


def moe_swiglu(h, w_router, w_gate, w_up, w_down):
    logits = (h @ w_router).astype(jnp.float32)
    top_v, top_i = lax.top_k(logits, TOP_K)
    gates = jax.nn.softmax(top_v, axis=-1)
    combine = jnp.sum(jax.nn.one_hot(top_i, N_EXPERTS, dtype=jnp.float32) * gates[..., None], axis=-2)
    out = jnp.zeros_like(h)
    for e in range(N_EXPERTS):
        out = out + combine[..., e:e + 1].astype(h.dtype) * swiglu(h, w_gate[e], w_up[e], w_down[e])
    return out


def heads(x, n):
    return x.reshape(x.shape[0], x.shape[1], n, HEAD_DIM)


def trunk(x, mem, p):
    B, S, _ = x.shape
    for layer in range(DEPTH):
        if layer % 2 == 0:
            i = layer // 2
            h = rms_norm(x, p['ev_norm_mix'][i])
            proj = h @ p['ev_w_in'][i]
            qa = heads(proj[..., 0:A_W], A_HEADS)
            ka = heads(proj[..., A_W:2 * A_W], A_HEADS)
            va = heads(proj[..., 2 * A_W:3 * A_W], A_HEADS)
            o0 = 3 * A_W
            qb = heads(proj[..., o0:o0 + B_W], B_HEADS)
            kb = heads(proj[..., o0 + B_W:o0 + 2 * B_W], B_HEADS)
            vb = heads(proj[..., o0 + 2 * B_W:o0 + 3 * B_W], B_HEADS)
            oa = mixture_of_dilations(qa, ka, va, p['rel_table']).reshape(B, S, A_W)
            ob = neighborhood_attention(qb, kb, vb, p['ev_na_rpb'][i]).reshape(B, S, B_W)
            x = x + jnp.concatenate([oa, ob], axis=-1) @ p['ev_w_out'][i]
        else:
            i = layer // 2
            h = rms_norm(x, p['od_norm_mix'][i])
            proj = h @ p['od_w_in'][i]
            q = rms_norm(heads(proj[..., :C_QW], C_Q_HEADS), p['od_q_norm'][i])
            k = rms_norm(heads(proj[..., C_QW:C_QW + C_KW], C_KV_HEADS), p['od_k_norm'][i])
            v = heads(proj[..., C_QW + C_KW:], C_KV_HEADS)
            o = gqa_blocked(axial_rope(q), axial_rope(k), v).reshape(B, S, C_QW)
            x = x + o @ p['od_w_out'][i]
        h = rms_norm(x, p['xa_norm'][layer])
        m = rms_norm(mem, p['xa_mem_norm'][layer])
        x = x + memory_cross_attention(h, m, p['xa_w_q'][layer], p['xa_w_kv'][layer], p['xa_w_out'][layer])
        if layer % 2 == 0:
            i = layer // 2
            h = rms_norm(x, p['ev_norm_ffn'][i])
            x = x + swiglu(h, p['ev_w_gate'][i], p['ev_w_up'][i], p['ev_w_down'][i])
        else:
            i = layer // 2
            h = rms_norm(x, p['od_norm_ffn'][i])
            x = x + moe_swiglu(h, p['od_router'][i], p['od_moe_gate'][i], p['od_moe_up'][i], p['od_moe_down'][i])
    return rms_norm(x, p['final_norm'])


def setup_inputs(seed: int = 0) -> dict:
    key = jax.random.key(seed)
    ks = iter(jax.random.split(key, 40))
    f32 = jnp.float32

    def w(shape, fan_in):
        return jax.random.normal(next(ks), shape, f32) * (fan_in ** -0.5)

    def gain(shape):
        return 1.0 + 0.01 * jax.random.normal(next(ks), shape, f32)

    return {
        'x_prompt': jax.random.normal(next(ks), (BATCH, SEQ, D_MODEL), f32),
        'x_sample': jax.random.normal(next(ks), (DEC_BATCH, DEC_SEQ, D_MODEL), f32),
        'mem_prompt': jax.random.normal(next(ks), (BATCH, N_MEM, D_MODEL), f32),
        'mem_sample': jax.random.normal(next(ks), (DEC_BATCH, N_MEM, D_MODEL), f32),
        'rel_table': 0.1 * jax.random.normal(next(ks), (N_BUCKETS, A_HEADS), f32),
        'ev_norm_mix': gain((N_EVEN, D_MODEL)),
        'ev_w_in': w((N_EVEN, D_MODEL, EVEN_IN), D_MODEL),
        'ev_na_rpb': 0.1 * jax.random.normal(next(ks), (N_EVEN, B_HEADS, 2 * NA_KH_MAX - 1, 2 * NA_KW - 1), f32),
        'ev_w_out': w((N_EVEN, EVEN_OUT, D_MODEL), EVEN_OUT),
        'ev_norm_ffn': gain((N_EVEN, D_MODEL)),
        'ev_w_gate': w((N_EVEN, D_MODEL, D_FF), D_MODEL),
        'ev_w_up': w((N_EVEN, D_MODEL, D_FF), D_MODEL),
        'ev_w_down': w((N_EVEN, D_FF, D_MODEL), D_FF),
        'od_norm_mix': gain((N_ODD, D_MODEL)),
        'od_w_in': w((N_ODD, D_MODEL, ODD_IN), D_MODEL),
        'od_q_norm': gain((N_ODD, HEAD_DIM)),
        'od_k_norm': gain((N_ODD, HEAD_DIM)),
        'od_w_out': w((N_ODD, C_QW, D_MODEL), C_QW),
        'od_norm_ffn': gain((N_ODD, D_MODEL)),
        'od_router': w((N_ODD, D_MODEL, N_EXPERTS), D_MODEL),
        'od_moe_gate': w((N_ODD, N_EXPERTS, D_MODEL, D_FF_EXPERT), D_MODEL),
        'od_moe_up': w((N_ODD, N_EXPERTS, D_MODEL, D_FF_EXPERT), D_MODEL),
        'od_moe_down': w((N_ODD, N_EXPERTS, D_FF_EXPERT, D_MODEL), D_FF_EXPERT),
        'xa_norm': gain((DEPTH, D_MODEL)),
        'xa_mem_norm': gain((DEPTH, D_MODEL)),
        'xa_w_q': w((DEPTH, D_MODEL, X_W), D_MODEL),
        'xa_w_kv': w((DEPTH, D_MODEL, 2 * X_W), D_MODEL),
        'xa_w_out': w((DEPTH, X_W, D_MODEL), X_W),
        'final_norm': gain((D_MODEL,)),
    }


def reference(x_prompt, x_sample, mem_prompt, mem_sample, rel_table,
              ev_norm_mix, ev_w_in, ev_na_rpb, ev_w_out, ev_norm_ffn, ev_w_gate, ev_w_up, ev_w_down,
              od_norm_mix, od_w_in, od_q_norm, od_k_norm, od_w_out, od_norm_ffn, od_router,
              od_moe_gate, od_moe_up, od_moe_down,
              xa_norm, xa_mem_norm, xa_w_q, xa_w_kv, xa_w_out, final_norm):
    p = dict(rel_table=rel_table,
             ev_norm_mix=ev_norm_mix, ev_w_in=ev_w_in, ev_na_rpb=ev_na_rpb, ev_w_out=ev_w_out,
             ev_norm_ffn=ev_norm_ffn, ev_w_gate=ev_w_gate, ev_w_up=ev_w_up, ev_w_down=ev_w_down,
             od_norm_mix=od_norm_mix, od_w_in=od_w_in, od_q_norm=od_q_norm, od_k_norm=od_k_norm,
             od_w_out=od_w_out, od_norm_ffn=od_norm_ffn, od_router=od_router,
             od_moe_gate=od_moe_gate, od_moe_up=od_moe_up, od_moe_down=od_moe_down,
             xa_norm=xa_norm, xa_mem_norm=xa_mem_norm, xa_w_q=xa_w_q, xa_w_kv=xa_w_kv,
             xa_w_out=xa_w_out, final_norm=final_norm)
    y_prompt = trunk(x_prompt, mem_prompt, p)
    y_sample = trunk(x_sample, mem_sample, p)
    return (y_prompt, y_sample)
```

```python
import functools
import math

import numpy as np
import jax
import jax.numpy as jnp
from jax import lax
from jax.experimental import pallas as pl
from jax.experimental.pallas import tpu as pltpu

F32 = jnp.float32
BF16 = jnp.bfloat16

D_MODEL = 1024
HEAD_DIM = 64
GRID_W = 64
N_MEM = 256
A_HEADS = 8
A_PATTERNS = ((128, 1), (512, 4), (2048, 16))
A_HALF = 64
B_HEADS = 8
NA_KH = 8
NA_KW = 16
C_Q_HEADS = 16
C_KV_HEADS = 4
ROPE_THETA = 10000.0
N_BUCKETS = 32
MAX_DISTANCE = 1024
X_HEADS = 4
N_EXPERTS = 8
EPS = 1e-6
NEG = -1e30
A_W = A_HEADS * HEAD_DIM
X_W = X_HEADS * HEAD_DIM
Q_SCALE = HEAD_DIM ** -0.5

LANES = 128
VMEM_LIMIT = 48 * 1024 * 1024

NA_QROWS = 4
NA_KROWS = 12


def _params(*sem):
    return pltpu.CompilerParams(dimension_semantics=sem, vmem_limit_bytes=VMEM_LIMIT)


def _rms(x, g):
    ms = jnp.mean(x * x, axis=-1, keepdims=True)
    return x * lax.rsqrt(ms + EPS) * g


def _dot(a, b):
    return jnp.dot(a, b, preferred_element_type=F32)


def _dot_nt(a, b):
    return lax.dot_general(a, b, (((1,), (1,)), ((), ())), preferred_element_type=F32)


def _half_masks(rows):
    lane = lax.broadcasted_iota(jnp.int32, (rows, LANES), 1)
    lo = lane < HEAD_DIM
    return lo, jnp.logical_not(lo)


def _softmax_pv(s, v):
    m = jnp.max(s, axis=1, keepdims=True)
    e = jnp.exp(s - m)
    l = jnp.sum(e, axis=1, keepdims=True)
    o = _dot(e.astype(BF16), v) / l
    return o, m, l


def _proj_kernel(x_ref, g_ref, w_ref, o_ref):
    h = _rms(x_ref[...], g_ref[...]).astype(BF16)
    o_ref[...] = _dot(h, w_ref[...]).astype(o_ref.dtype)


def _proj(x, g, w, tm):
    n, d = x.shape
    nout = w.shape[1]
    return pl.pallas_call(
        _proj_kernel,
        grid=(n // tm,),
        in_specs=[pl.BlockSpec((tm, d), lambda i: (i, 0)),
                  pl.BlockSpec((1, d), lambda i: (0, 0)),
                  pl.BlockSpec((d, nout), lambda i: (0, 0))],
        out_specs=pl.BlockSpec((tm, nout), lambda i: (i, 0)),
        out_shape=jax.ShapeDtypeStruct((n, nout), BF16),
        compiler_params=_params("parallel"),
        name="proj",
    )(x, g.reshape(1, d), w)


def _proj_odd_kernel(x_ref, g_ref, w_ref, qg_ref, kg_ref, cos_ref, sin_ref, q_ref, k_ref, v_ref):
    h = _rms(x_ref[...], g_ref[...]).astype(BF16)
    y = _dot(h, w_ref[...])
    tm = y.shape[0]
    cos = cos_ref[...]
    sin = sin_ref[...]
    lane = lax.broadcasted_iota(jnp.int32, (tm, LANES), 1)
    lo = lane < HEAD_DIM
    first = (lane & 31) < 16

    def norm_rope(c, gain, scale):
        ss = c * c
        s_lo = jnp.sum(jnp.where(lo, ss, 0.0), axis=1, keepdims=True)
        s_hi = jnp.sum(jnp.where(lo, 0.0, ss), axis=1, keepdims=True)
        ms = jnp.where(lo, s_lo, s_hi) * (1.0 / HEAD_DIM)
        c = c * lax.rsqrt(ms + EPS) * gain
        partner = jnp.where(first, pltpu.roll(c, LANES - 16, 1), pltpu.roll(c, 16, 1))
        return (c * cos + partner * sin) * scale

    nq = q_ref.shape[1] // LANES
    nk = k_ref.shape[1] // LANES
    for j in range(nq):
        c = y[:, j * LANES:(j + 1) * LANES]
        q_ref[:, j * LANES:(j + 1) * LANES] = norm_rope(c, qg_ref[...], Q_SCALE).astype(BF16)
    for j in range(nk):
        c = y[:, (nq + j) * LANES:(nq + j + 1) * LANES]
        k_ref[:, j * LANES:(j + 1) * LANES] = norm_rope(c, kg_ref[...], 1.0).astype(BF16)
    v_ref[...] = y[:, (nq + nk) * LANES:].astype(BF16)


def _proj_odd(x, g, w, qg, kg, cos, sin, seq, tm):
    n, d = x.shape
    nq = C_Q_HEADS * HEAD_DIM
    nk = C_KV_HEADS * LANES
    spb = seq // tm
    return pl.pallas_call(
        _proj_odd_kernel,
        grid=(n // tm,),
        in_specs=[pl.BlockSpec((tm, d), lambda i: (i, 0)),
                  pl.BlockSpec((1, d), lambda i: (0, 0)),
                  pl.BlockSpec((d, nq + 2 * nk), lambda i: (0, 0)),
                  pl.BlockSpec((1, LANES), lambda i: (0, 0)),
                  pl.BlockSpec((1, LANES), lambda i: (0, 0)),
                  pl.BlockSpec((tm, LANES), lambda i: (i % spb, 0)),
                  pl.BlockSpec((tm, LANES), lambda i: (i % spb, 0))],
        out_specs=[pl.BlockSpec((tm, nq), lambda i: (i, 0)),
                   pl.BlockSpec((tm, nk), lambda i: (i, 0)),
                   pl.BlockSpec((tm, nk), lambda i: (i, 0))],
        out_shape=[jax.ShapeDtypeStruct((n, nq), BF16),
                   jax.ShapeDtypeStruct((n, nk), BF16),
                   jax.ShapeDtypeStruct((n, nk), BF16)],
        compiler_params=_params("parallel"),
        name="proj_odd",
    )(x, g.reshape(1, d), w, qg, kg, cos, sin)


def _dil_kernel(*refs, nsub, mix):
    if mix:
        (q_ref, kp_ref, kc_ref, kn_ref, vp_ref, vc_ref, vn_ref, bias_ref,
         o0_ref, l0_ref, o1_ref, l1_ref, o_ref, kwin, vwin) = refs
    else:
        (q_ref, kp_ref, kc_ref, kn_ref, vp_ref, vc_ref, vn_ref, bias_ref,
         o_ref, lse_ref, kwin, vwin) = refs
    tq = q_ref.shape[1]
    i = pl.program_id(2)
    nblk = pl.num_programs(2)
    kwin[0:A_HALF] = kp_ref[0]
    kwin[A_HALF:A_HALF + tq] = kc_ref[0]
    kwin[A_HALF + tq:] = kn_ref[0]
    vwin[0:A_HALF] = vp_ref[0]
    vwin[A_HALF:A_HALF + tq] = vc_ref[0]
    vwin[A_HALF + tq:] = vn_ref[0]
    lo, hi = _half_masks(LANES)
    lane = lax.broadcasted_iota(jnp.int32, (LANES, LANES), 1)
    kj = lax.broadcasted_iota(jnp.int32, (LANES, 2 * LANES), 1)
    zero = jnp.zeros((), BF16)
    for a in range(nsub):
        rows = slice(a * LANES, (a + 1) * LANES)
        valid = None
        if a == 0:
            valid = kj >= jnp.where(i == 0, A_HALF, 0)
        if a == nsub - 1:
            v2 = kj < jnp.where(i == nblk - 1, LANES + A_HALF, 2 * LANES)
            valid = v2 if valid is None else jnp.logical_and(valid, v2)
        lse_all = jnp.zeros((LANES, LANES), F32)
        for p in range(A_HEADS // 2):
            cols = slice(p * LANES, (p + 1) * LANES)
            qp = q_ref[0, rows, cols]
            kk = kwin[a * LANES:(a + 2) * LANES, cols]
            vv = vwin[a * LANES:(a + 2) * LANES, cols]
            halves = []
            for hf in range(2):
                h = 2 * p + hf
                qm = jnp.where(lo if hf == 0 else hi, qp, zero)
                s = _dot_nt(qm, kk) + bias_ref[h]
                if valid is not None:
                    s = jnp.where(valid, s, NEG)
                o, m, l = _softmax_pv(s, vv)
                lse = m + jnp.log(l)
                if mix:
                    la = l0_ref[0, rows, 16 * h:16 * h + 1]
                    lb = l1_ref[0, rows, 16 * h:16 * h + 1]
                    mx = jnp.maximum(jnp.maximum(la, lb), lse)
                    wa = jnp.exp(la - mx)
                    wb = jnp.exp(lb - mx)
                    wc = jnp.exp(lse - mx)
                    den = wa + wb + wc
                    o = ((wa / den) * o0_ref[0, rows, cols].astype(F32)
                         + (wb / den) * o1_ref[0, rows, cols].astype(F32)
                         + (wc / den) * o)
                else:
                    lse_all = jnp.where(lane // 16 == h, lse, lse_all)
                halves.append(o)
            o_ref[0, rows, cols] = jnp.where(lo, halves[0], halves[1]).astype(o_ref.dtype)
        if not mix:
            lse_ref[0, rows, :] = lse_all


def _dilated_pattern(pr, batch, seq, dil, bias, prev):
    n = batch * seq
    sub = seq // dil
    tq = min(sub, 4 * LANES)
    nsub = tq // LANES
    nblk = sub // tq
    hb = tq // A_HALF
    nhalo = sub // A_HALF
    ncol = pr.shape[1] // A_W
    prv = pr.reshape(batch, sub, dil * pr.shape[1])

    def main(c):
        return pl.BlockSpec((1, tq, A_W), lambda b, r, i: (b, i, r * ncol + c))

    def before(c):
        return pl.BlockSpec((1, A_HALF, A_W),
                            lambda b, r, i: (b, jnp.maximum(i * hb - 1, 0), r * ncol + c))

    def after(c):
        return pl.BlockSpec((1, A_HALF, A_W),
                            lambda b, r, i: (b, jnp.minimum((i + 1) * hb, nhalo - 1), r * ncol + c))

    o_spec = pl.BlockSpec((1, tq, A_W), lambda b, r, i: (b, i, r))
    l_spec = pl.BlockSpec((1, tq, LANES), lambda b, r, i: (b, i, r))
    in_specs = [main(0), before(1), main(1), after(1), before(2), main(2), after(2),
                pl.BlockSpec((A_HEADS, LANES, 2 * LANES), lambda b, r, i: (0, 0, 0))]
    args = [prv] * 7 + [bias]
    o_shape = jax.ShapeDtypeStruct((batch, sub, dil * A_W), BF16)
    l_shape = jax.ShapeDtypeStruct((batch, sub, dil * LANES), F32)
    if prev is None:
        out_specs, out_shape = [o_spec, l_spec], [o_shape, l_shape]
    else:
        o0, l0, o1, l1 = prev
        in_specs += [o_spec, l_spec, o_spec, l_spec]
        args += [o0.reshape(o_shape.shape), l0.reshape(l_shape.shape),
                 o1.reshape(o_shape.shape), l1.reshape(l_shape.shape)]
        out_specs, out_shape = o_spec, o_shape
    out = pl.pallas_call(
        functools.partial(_dil_kernel, nsub=nsub, mix=prev is not None),
        grid=(batch, dil, nblk),
        in_specs=in_specs,
        out_specs=out_specs,
        out_shape=out_shape,
        scratch_shapes=[pltpu.VMEM((tq + 2 * A_HALF, A_W), BF16),
                        pltpu.VMEM((tq + 2 * A_HALF, A_W), BF16)],
        compiler_params=_params("parallel", "parallel", "parallel"),
        name=f"dilated_d{dil}",
    )(*args)
    if prev is None:
        return out[0].reshape(n, A_W), out[1].reshape(n, LANES)
    return out.reshape(n, A_W)


def _dilated(pr, batch, seq, biases):
    o0, l0 = _dilated_pattern(pr, batch, seq, A_PATTERNS[0][1], biases[0], None)
    o1, l1 = _dilated_pattern(pr, batch, seq, A_PATTERNS[1][1], biases[1], None)
    return _dilated_pattern(pr, batch, seq, A_PATTERNS[2][1], biases[2], (o0, l0, o1, l1))


def _na_kernel(q_ref, k0_ref, k1_ref, k2_ref, v0_ref, v1_ref, v2_ref, bias_ref, o_ref):
    tq = q_ref.shape[1]
    lo, hi = _half_masks(tq)
    zero = jnp.zeros((), BF16)
    for p in range(B_HEADS // 2):
        cols = slice(p * LANES, (p + 1) * LANES)
        qp = q_ref[0, :, cols]
        kk = jnp.concatenate([k0_ref[0, :, cols], k1_ref[0, :, cols], k2_ref[0, :, cols]], axis=0)
        vv = jnp.concatenate([v0_ref[0, :, cols], v1_ref[0, :, cols], v2_ref[0, :, cols]], axis=0)
        halves = []
        for hf in range(2):
            qm = jnp.where(lo if hf == 0 else hi, qp, zero)
            s = _dot_nt(qm, kk) + bias_ref[0, 2 * p + hf]
            o, _, _ = _softmax_pv(s, vv)
            halves.append(o)
        o_ref[0, :, cols] = jnp.where(lo, halves[0], halves[1]).astype(o_ref.dtype)


def _natten(pr, batch, seq, bias):
    n = batch * seq
    tq = NA_QROWS * GRID_W
    nblk = seq // tq
    nkb = NA_KROWS // NA_QROWS
    ncol = pr.shape[1] // A_W
    prv = pr.reshape(batch, seq, pr.shape[1])

    def kv_spec(c, j):
        return pl.BlockSpec((1, tq, A_W),
                            lambda b, i: (b, jnp.clip(i - 1, 0, nblk - nkb) + j, c))

    def variant(i):
        return jnp.where(i == 0, 0, jnp.where(i == nblk - 1, 2, 1))

    in_specs = ([pl.BlockSpec((1, tq, A_W), lambda b, i: (b, i, 3))]
                + [kv_spec(4, j) for j in range(nkb)] + [kv_spec(5, j) for j in range(nkb)]
                + [pl.BlockSpec((1, B_HEADS, tq, nkb * tq), lambda b, i: (variant(i), 0, 0, 0))])
    out = pl.pallas_call(
        _na_kernel,
        grid=(batch, nblk),
        in_specs=in_specs,
        out_specs=pl.BlockSpec((1, tq, A_W), lambda b, i: (b, i, 0)),
        out_shape=jax.ShapeDtypeStruct((batch, seq, A_W), BF16),
        compiler_params=_params("parallel", "parallel"),
        name="natten",
    )(*([prv] * (1 + 2 * nkb) + [bias]))
    return out.reshape(n, A_W)


def _gqa_kernel(q_ref, k_ref, v_ref, o_ref, *, tk):
    tq = q_ref.shape[1]
    nchunk = k_ref.shape[1] // tk
    lo, hi = _half_masks(tq)
    zero = jnp.zeros((), BF16)
    parts = []
    for p in range(2):
        qp = q_ref[0, :, p * LANES:(p + 1) * LANES]
        parts += [jnp.where(lo, qp, zero), jnp.where(hi, qp, zero)]
    qs = jnp.concatenate(parts, axis=0)
    rows = 4 * tq

    def body(c, carry):
        m, l, acc = carry
        start = pl.multiple_of(c * tk, tk)
        kc = k_ref[0, pl.ds(start, tk), :]
        vc = v_ref[0, pl.ds(start, tk), :]
        s = _dot_nt(qs, kc)
        m_new = jnp.maximum(m, jnp.max(s, axis=1, keepdims=True))
        alpha = jnp.exp(m - m_new)
        e = jnp.exp(s - m_new)
        l = alpha * l + jnp.sum(e, axis=1, keepdims=True)
        acc = alpha * acc + _dot(e.astype(BF16), vc)
        return m_new, l, acc

    init = (jnp.full((rows, 1), NEG, F32), jnp.zeros((rows, 1), F32), jnp.zeros((rows, LANES), F32))
    _, l, acc = lax.fori_loop(0, nchunk, body, init)
    o = acc / l
    for p in range(2):
        even = o[(2 * p) * tq:(2 * p + 1) * tq]
        odd = o[(2 * p + 1) * tq:(2 * p + 2) * tq]
        o_ref[0, :, p * LANES:(p + 1) * LANES] = jnp.where(lo, even, odd).astype(o_ref.dtype)


def _gqa(q, kk, vv, batch, seq, tq=128, tk=512):
    n = batch * seq
    gw = (C_Q_HEADS // C_KV_HEADS) * HEAD_DIM
    out = pl.pallas_call(
        functools.partial(_gqa_kernel, tk=tk),
        grid=(batch, C_KV_HEADS, seq // tq),
        in_specs=[pl.BlockSpec((1, tq, gw), lambda b, g, i: (b, i, g)),
                  pl.BlockSpec((1, seq, LANES), lambda b, g, i: (b, 0, g)),
                  pl.BlockSpec((1, seq, LANES), lambda b, g, i: (b, 0, g))],
        out_specs=pl.BlockSpec((1, tq, gw), lambda b, g, i: (b, i, g)),
        out_shape=jax.ShapeDtypeStruct((batch, seq, C_Q_HEADS * HEAD_DIM), BF16),
        compiler_params=_params("parallel", "parallel", "parallel"),
        name="gqa",
    )(q.reshape(batch, seq, -1), kk.reshape(batch, seq, -1), vv.reshape(batch, seq, -1))
    return out.reshape(n, C_Q_HEADS * HEAD_DIM)


def _post_kernel(*refs, route):
    if route:
        (x_ref, a1_ref, a2_ref, wo1_ref, wo2_ref, gxa_ref, wq_ref, kv_ref, wox_ref, gffn_ref,
         wr_ref, x2_ref, hn_ref, comb_ref) = refs
    else:
        (x_ref, a1_ref, a2_ref, wo1_ref, wo2_ref, gxa_ref, wq_ref, kv_ref, wox_ref, gffn_ref,
         x2_ref, hn_ref) = refs
    x1 = x_ref[...] + _dot(a1_ref[...], wo1_ref[...]) + _dot(a2_ref[...], wo2_ref[...])
    tm = x1.shape[0]
    h = _rms(x1, gxa_ref[...]).astype(BF16)
    q = _dot(h, wq_ref[...]).astype(BF16)
    lo, hi = _half_masks(tm)
    zero = jnp.zeros((), BF16)
    outs = []
    for p in range(X_HEADS // 2):
        qp = q[:, p * LANES:(p + 1) * LANES]
        kk = kv_ref[0, :, p * LANES:(p + 1) * LANES]
        vv = kv_ref[0, :, X_W + p * LANES:X_W + (p + 1) * LANES]
        halves = []
        for hf in range(2):
            qm = jnp.where(lo if hf == 0 else hi, qp, zero)
            o, _, _ = _softmax_pv(_dot_nt(qm, kk), vv)
            halves.append(o)
        outs.append(jnp.where(lo, halves[0], halves[1]).astype(BF16))
    x2 = x1 + _dot(jnp.concatenate(outs, axis=1), wox_ref[...])
    x2_ref[...] = x2
    hf32 = _rms(x2, gffn_ref[...])
    hn_ref[...] = hf32.astype(BF16)
    if route:
        logits = jnp.dot(hf32, wr_ref[...], preferred_element_type=F32,
                         precision=lax.Precision.HIGHEST)
        lane = lax.broadcasted_iota(jnp.int32, logits.shape, 1).astype(F32)
        lg = jnp.where(lane < N_EXPERTS, logits, NEG)
        m1 = jnp.max(lg, axis=1, keepdims=True)
        i1 = jnp.min(jnp.where(lg == m1, lane, float(LANES)), axis=1, keepdims=True)
        lg2 = jnp.where(lane == i1, NEG, lg)
        m2 = jnp.max(lg2, axis=1, keepdims=True)
        i2 = jnp.min(jnp.where(lg2 == m2, lane, float(LANES)), axis=1, keepdims=True)
        e = jnp.exp(m2 - m1)
        den = 1.0 + e
        comb_ref[...] = jnp.where(lane == i1, 1.0 / den, 0.0) + jnp.where(lane == i2, e / den, 0.0)


def _post(x, a1, a2, c1, c2, wo1, wo2, gxa, wq, kvm, wox, gffn, wr, seq, tm):
    n, d = x.shape
    spb = seq // tm
    row = lambda i: (i, 0)
    const = lambda i: (0, 0)
    in_specs = [pl.BlockSpec((tm, d), row),
                pl.BlockSpec((tm, A_W), lambda i: (i, c1)),
                pl.BlockSpec((tm, A_W), lambda i: (i, c2)),
                pl.BlockSpec((A_W, d), const),
                pl.BlockSpec((A_W, d), const),
                pl.BlockSpec((1, d), const),
                pl.BlockSpec((d, X_W), const),
                pl.BlockSpec((1, N_MEM, 2 * X_W), lambda i: (i // spb, 0, 0)),
                pl.BlockSpec((X_W, d), const),
                pl.BlockSpec((1, d), const)]
    args = [x, a1, a2, wo1, wo2, gxa.reshape(1, d), wq, kvm, wox, gffn.reshape(1, d)]
    out_specs = [pl.BlockSpec((tm, d), row), pl.BlockSpec((tm, d), row)]
    out_shape = [jax.ShapeDtypeStruct((n, d), F32), jax.ShapeDtypeStruct((n, d), BF16)]
    if wr is not None:
        in_specs.append(pl.BlockSpec((d, LANES), const))
        args.append(wr)
        out_specs.append(pl.BlockSpec((tm, LANES), row))
        out_shape.append(jax.ShapeDtypeStruct((n, LANES), F32))
    return pl.pallas_call(
        functools.partial(_post_kernel, route=wr is not None),
        grid=(n // tm,),
        in_specs=in_specs,
        out_specs=out_specs,
        out_shape=out_shape,
        compiler_params=_params("parallel"),
        name="post_route" if wr is not None else "post",
    )(*args)


def _swiglu_chunk(h, wg, wu, wd):
    g = _dot(h, wg)
    u = _dot(h, wu)
    a = (g / (1.0 + jnp.exp(-g))) * u
    return _dot(a.astype(BF16), wd)


def _ffn_kernel(x_ref, h_ref, wg_ref, wu_ref, wd_ref, o_ref):
    f = pl.program_id(1)
    y = _swiglu_chunk(h_ref[...], wg_ref[...], wu_ref[...], wd_ref[...])

    @pl.when(f == 0)
    def _():
        o_ref[...] = x_ref[...] + y

    @pl.when(f > 0)
    def _():
        o_ref[...] += y


def _ffn(x, hn, wg, wu, wd, tm, tf):
    n, d = x.shape
    ff = wg.shape[1]
    return pl.pallas_call(
        _ffn_kernel,
        grid=(n // tm, ff // tf),
        in_specs=[pl.BlockSpec((tm, d), lambda i, f: (i, 0)),
                  pl.BlockSpec((tm, d), lambda i, f: (i, 0)),
                  pl.BlockSpec((d, tf), lambda i, f: (0, f)),
                  pl.BlockSpec((d, tf), lambda i, f: (0, f)),
                  pl.BlockSpec((tf, d), lambda i, f: (f, 0))],
        out_specs=pl.BlockSpec((tm, d), lambda i, f: (i, 0)),
        out_shape=jax.ShapeDtypeStruct((n, d), F32),
        compiler_params=_params("parallel", "arbitrary"),
        name="ffn",
    )(x, hn, wg, wu, wd)


def _moe_kernel(x_ref, h_ref, comb_ref, wg_ref, wu_ref, wd_ref, o_ref):
    e = pl.program_id(1)
    f = pl.program_id(2)
    lane = lax.broadcasted_iota(jnp.int32, comb_ref.shape, 1)
    c = jnp.sum(jnp.where(lane == e, comb_ref[...], 0.0), axis=1, keepdims=True)
    y = c * _swiglu_chunk(h_ref[...], wg_ref[0], wu_ref[0], wd_ref[0])
    first = jnp.logical_and(e == 0, f == 0)

    @pl.when(first)
    def _():
        o_ref[...] = x_ref[...] + y

    @pl.when(jnp.logical_not(first))
    def _():
        o_ref[...] += y


def _moe(x, hn, comb, wg, wu, wd, tm, tf):
    n, d = x.shape
    ne, _, ff = wg.shape
    return pl.pallas_call(
        _moe_kernel,
        grid=(n // tm, ne, ff // tf),
        in_specs=[pl.BlockSpec((tm, d), lambda i, e, f: (i, 0)),
                  pl.BlockSpec((tm, d), lambda i, e, f: (i, 0)),
                  pl.BlockSpec((tm, LANES), lambda i, e, f: (i, 0)),
                  pl.BlockSpec((1, d, tf), lambda i, e, f: (e, 0, f)),
                  pl.BlockSpec((1, d, tf), lambda i, e, f: (e, 0, f)),
                  pl.BlockSpec((1, tf, d), lambda i, e, f: (e, f, 0))],
        out_specs=pl.BlockSpec((tm, d), lambda i, e, f: (i, 0)),
        out_shape=jax.ShapeDtypeStruct((n, d), F32),
        compiler_params=_params("parallel", "arbitrary", "arbitrary"),
        name="moe",
    )(x, hn, comb, wg, wu, wd)


def _final_norm_kernel(x_ref, g_ref, o_ref):
    o_ref[...] = _rms(x_ref[...], g_ref[...])


def _final_norm(x, g, tm):
    n, d = x.shape
    return pl.pallas_call(
        _final_norm_kernel,
        grid=(n // tm,),
        in_specs=[pl.BlockSpec((tm, d), lambda i: (i, 0)), pl.BlockSpec((1, d), lambda i: (0, 0))],
        out_specs=pl.BlockSpec((tm, d), lambda i: (i, 0)),
        out_shape=jax.ShapeDtypeStruct((n, d), F32),
        compiler_params=_params("parallel"),
        name="final_norm",
    )(x, g.reshape(1, d))


def _t5_bucket_np(rel):
    nb = N_BUCKETS // 2
    max_exact = nb // 2
    ret = np.where(rel > 0, nb, 0)
    n = np.abs(rel)
    large = max_exact + (np.log(np.maximum(n, 1).astype(np.float32) / max_exact)
                         / math.log(MAX_DISTANCE / max_exact) * (nb - max_exact)).astype(np.int32)
    large = np.minimum(large, nb - 1)
    return ret + np.where(n < max_exact, n, large)


def _dilated_bias(rel_table, dil):
    off = np.arange(2 * LANES)[None, :] - A_HALF - np.arange(LANES)[:, None]
    bucket = _t5_bucket_np(off * dil)
    bias = rel_table[jnp.asarray(bucket)].transpose(2, 0, 1).astype(F32)
    return jnp.where(jnp.asarray(np.abs(off) <= A_HALF)[None], bias, NEG)


def _natten_bias(rpb):
    rows = 4 * NA_KROWS
    tables = []
    for r0 in (0, 2 * NA_QROWS, rows - NA_QROWS):
        ks = int(np.clip(r0 - NA_KH // 2, 0, rows - NA_KROWS))
        qr = r0 + np.arange(NA_QROWS)
        kr = ks + np.arange(NA_KROWS)
        rs = np.clip(qr - NA_KH // 2, 0, rows - NA_KH)
        row_ok = (kr[None, :] >= rs[:, None]) & (kr[None, :] < rs[:, None] + NA_KH)
        dr = np.clip(kr[None, :] - qr[:, None] + NA_KH - 1, 0, 2 * NA_KH - 2)
        c = np.arange(GRID_W)
        cs = np.clip(c - NA_KW // 2, 0, GRID_W - NA_KW)
        col_ok = (c[None, :] >= cs[:, None]) & (c[None, :] < cs[:, None] + NA_KW)
        dc = np.clip(c[None, :] - c[:, None], -(NA_KW - 1), NA_KW - 1) + NA_KW - 1
        shape = (NA_QROWS, GRID_W, NA_KROWS, GRID_W)
        dr_i = np.broadcast_to(dr[:, None, :, None], shape).reshape(NA_QROWS * GRID_W, -1)
        dc_i = np.broadcast_to(dc[None, :, None, :], shape).reshape(NA_QROWS * GRID_W, -1)
        ok = np.broadcast_to(row_ok[:, None, :, None] & col_ok[None, :, None, :], shape)
        ok = ok.reshape(NA_QROWS * GRID_W, -1)
        tables.append(jnp.where(jnp.asarray(ok)[None], rpb[:, dr_i, dc_i].astype(F32), NEG))
    return jnp.stack(tables, axis=0)


def _rope_tables(seq):
    t = jnp.arange(seq)
    half = HEAD_DIM // 2
    freqs = ROPE_THETA ** (-jnp.arange(0, half, 2, dtype=F32) / half)
    ang_r = (t // GRID_W).astype(F32)[:, None] * freqs[None, :]
    ang_c = (t % GRID_W).astype(F32)[:, None] * freqs[None, :]
    cos_h = jnp.concatenate([jnp.cos(ang_r)] * 2 + [jnp.cos(ang_c)] * 2, axis=1)
    sin_h = jnp.concatenate([-jnp.sin(ang_r), jnp.sin(ang_r), -jnp.sin(ang_c), jnp.sin(ang_c)], axis=1)
    return jnp.concatenate([cos_h, cos_h], axis=1), jnp.concatenate([sin_h, sin_h], axis=1)


def _prepare(p):
    w = {}
    ev_in = p['ev_w_in']
    scale = np.ones((ev_in.shape[-1],), np.float32)
    scale[0:A_W] = Q_SCALE
    scale[3 * A_W:4 * A_W] = Q_SCALE
    w['ev_w_in'] = (ev_in * scale).astype(BF16)
    w['ev_w_out'] = p['ev_w_out'].astype(BF16)
    w['ev_w_gate'] = p['ev_w_gate'].astype(BF16)
    w['ev_w_up'] = p['ev_w_up'].astype(BF16)
    w['ev_w_down'] = p['ev_w_down'].astype(BF16)
    nq = C_Q_HEADS * HEAD_DIM
    nkv = C_KV_HEADS * HEAD_DIM
    dup = np.repeat(np.arange(C_KV_HEADS), 2)[:, None] * HEAD_DIM + np.arange(HEAD_DIM)[None, :]
    dup = dup.reshape(-1)
    cols = np.concatenate([np.arange(nq), nq + dup, nq + nkv + dup])
    w['od_w_in'] = p['od_w_in'][:, :, cols].astype(BF16)
    w['od_w_out'] = p['od_w_out'].astype(BF16)
    w['od_q_norm'] = jnp.concatenate([p['od_q_norm']] * 2, axis=-1)[:, None, :]
    w['od_k_norm'] = jnp.concatenate([p['od_k_norm']] * 2, axis=-1)[:, None, :]
    w['od_router'] = jnp.pad(p['od_router'], ((0, 0), (0, 0), (0, LANES - N_EXPERTS)))
    w['od_moe_gate'] = p['od_moe_gate'].astype(BF16)
    w['od_moe_up'] = p['od_moe_up'].astype(BF16)
    w['od_moe_down'] = p['od_moe_down'].astype(BF16)
    w['xa_w_q'] = (p['xa_w_q'] * Q_SCALE).astype(BF16)
    w['xa_w_kv'] = p['xa_w_kv'].astype(BF16)
    w['xa_w_out'] = p['xa_w_out'].astype(BF16)
    w['dil_bias'] = [_dilated_bias(p['rel_table'], dil) for _, dil in A_PATTERNS]
    w['na_bias'] = [_natten_bias(p['ev_na_rpb'][i]) for i in range(p['ev_na_rpb'].shape[0])]
    return w


def _trunk(x, mem, p, w):
    batch, seq, d = x.shape
    n = batch * seq
    xf = x.reshape(n, d)
    memf = mem.reshape(batch * N_MEM, d)
    cos, sin = _rope_tables(seq)
    depth = p['xa_norm'].shape[0]
    for layer in range(depth):
        i = layer // 2
        if layer % 2 == 0:
            pr = _proj(xf, p['ev_norm_mix'][i], w['ev_w_in'][i], tm=512)
            a1 = _dilated(pr, batch, seq, w['dil_bias'])
            a2 = _natten(pr, batch, seq, w['na_bias'][i])
            c1 = c2 = 0
            wo = w['ev_w_out'][i]
            wr = None
        else:
            q, kk, vv = _proj_odd(xf, p['od_norm_mix'][i], w['od_w_in'][i], w['od_q_norm'][i],
                                  w['od_k_norm'][i], cos, sin, seq, tm=512)
            a1 = a2 = _gqa(q, kk, vv, batch, seq)
            c1, c2 = 0, 1
            wo = w['od_w_out'][i]
            wr = w['od_router'][i]
        kvm = _proj(memf, p['xa_mem_norm'][layer], w['xa_w_kv'][layer], tm=512)
        kvm = kvm.reshape(batch, N_MEM, 2 * X_W)
        gffn = p['ev_norm_ffn'][i] if layer % 2 == 0 else p['od_norm_ffn'][i]
        outs = _post(xf, a1, a2, c1, c2, wo[:A_W], wo[A_W:], p['xa_norm'][layer], w['xa_w_q'][layer],
                     kvm, w['xa_w_out'][layer], gffn, wr, seq, tm=512)
        if layer % 2 == 0:
            x2, hn = outs
            xf = _ffn(x2, hn, w['ev_w_gate'][i], w['ev_w_up'][i], w['ev_w_down'][i], tm=1024, tf=256)
        else:
            x2, hn, comb = outs
            xf = _moe(x2, hn, comb, w['od_moe_gate'][i], w['od_moe_up'][i], w['od_moe_down'][i],
                      tm=1024, tf=512)
    return _final_norm(xf, p['final_norm'], tm=1024).reshape(batch, seq, d)


def kernel(x_prompt, x_sample, mem_prompt, mem_sample, rel_table, ev_norm_mix, ev_w_in, ev_na_rpb, ev_w_out, ev_norm_ffn, ev_w_gate, ev_w_up, ev_w_down, od_norm_mix, od_w_in, od_q_norm, od_k_norm, od_w_out, od_norm_ffn, od_router, od_moe_gate, od_moe_up, od_moe_down, xa_norm, xa_mem_norm, xa_w_q, xa_w_kv, xa_w_out, final_norm):
    p = dict(rel_table=rel_table,
             ev_norm_mix=ev_norm_mix, ev_w_in=ev_w_in, ev_na_rpb=ev_na_rpb, ev_w_out=ev_w_out,
             ev_norm_ffn=ev_norm_ffn, ev_w_gate=ev_w_gate, ev_w_up=ev_w_up, ev_w_down=ev_w_down,
             od_norm_mix=od_norm_mix, od_w_in=od_w_in, od_q_norm=od_q_norm, od_k_norm=od_k_norm,
             od_w_out=od_w_out, od_norm_ffn=od_norm_ffn, od_router=od_router,
             od_moe_gate=od_moe_gate, od_moe_up=od_moe_up, od_moe_down=od_moe_down,
             xa_norm=xa_norm, xa_mem_norm=xa_mem_norm, xa_w_q=xa_w_q, xa_w_kv=xa_w_kv,
             xa_w_out=xa_w_out, final_norm=final_norm)
    w = _prepare(p)
    return (_trunk(x_prompt, mem_prompt, p, w), _trunk(x_sample, mem_sample, p, w))
```

```python
import functools
import math

import numpy as np
import jax
import jax.numpy as jnp
from jax import lax
from jax.experimental import pallas as pl
from jax.experimental.pallas import tpu as pltpu

F32 = jnp.float32
BF16 = jnp.bfloat16

D_MODEL = 1024
HEAD_DIM = 64
GRID_W = 64
N_MEM = 256
A_HEADS = 8
A_PATTERNS = ((128, 1), (512, 4), (2048, 16))
A_HALF = 64
B_HEADS = 8
NA_KH = 8
NA_KW = 16
C_Q_HEADS = 16
C_KV_HEADS = 4
ROPE_THETA = 10000.0
N_BUCKETS = 32
MAX_DISTANCE = 1024
X_HEADS = 4
N_EXPERTS = 8
EPS = 1e-6
NEG = -1e30
A_W = A_HEADS * HEAD_DIM
X_W = X_HEADS * HEAD_DIM
Q_SCALE = HEAD_DIM ** -0.5

LANES = 128
VMEM_LIMIT = 48 * 1024 * 1024

NA_QROWS = 4
NA_KROWS = 12


def _params(*sem):
    return pltpu.CompilerParams(dimension_semantics=sem, vmem_limit_bytes=VMEM_LIMIT)


def _rms(x, g):
    ms = jnp.mean(x * x, axis=-1, keepdims=True)
    return x * lax.rsqrt(ms + EPS) * g


def _dot(a, b):
    return jnp.dot(a, b, preferred_element_type=F32)


def _dot_nt(a, b):
    return lax.dot_general(a, b, (((1,), (1,)), ((), ())), preferred_element_type=F32)


def _half_masks(rows):
    lane = lax.broadcasted_iota(jnp.int32, (rows, LANES), 1)
    lo = lane < HEAD_DIM
    return lo, jnp.logical_not(lo)


def _softmax_pv(s, v):
    m = jnp.max(s, axis=1, keepdims=True)
    e = jnp.exp(s - m)
    l = jnp.sum(e, axis=1, keepdims=True)
    o = _dot(e.astype(BF16), v) / l
    return o, m, l


def _proj_kernel(x_ref, g_ref, w_ref, o_ref):
    h = _rms(x_ref[...], g_ref[...]).astype(BF16)
    o_ref[...] = _dot(h, w_ref[...]).astype(o_ref.dtype)


def _proj(x, g, w, tm):
    n, d = x.shape
    nout = w.shape[1]
    return pl.pallas_call(
        _proj_kernel,
        grid=(n // tm,),
        in_specs=[pl.BlockSpec((tm, d), lambda i: (i, 0)),
                  pl.BlockSpec((1, d), lambda i: (0, 0)),
                  pl.BlockSpec((d, nout), lambda i: (0, 0))],
        out_specs=pl.BlockSpec((tm, nout), lambda i: (i, 0)),
        out_shape=jax.ShapeDtypeStruct((n, nout), BF16),
        compiler_params=_params("parallel"),
        name="proj",
    )(x, g.reshape(1, d), w)


def _proj_odd_kernel(x_ref, g_ref, w_ref, qg_ref, kg_ref, cos_ref, sin_ref, q_ref, k_ref, v_ref):
    h = _rms(x_ref[...], g_ref[...]).astype(BF16)
    y = _dot(h, w_ref[...])
    tm = y.shape[0]
    cos = cos_ref[...]
    sin = sin_ref[...]
    lane = lax.broadcasted_iota(jnp.int32, (tm, LANES), 1)
    lo = lane < HEAD_DIM
    first = (lane & 31) < 16

    def norm_rope(c, gain, scale):
        ss = c * c
        s_lo = jnp.sum(jnp.where(lo, ss, 0.0), axis=1, keepdims=True)
        s_hi = jnp.sum(jnp.where(lo, 0.0, ss), axis=1, keepdims=True)
        ms = jnp.where(lo, s_lo, s_hi) * (1.0 / HEAD_DIM)
        c = c * lax.rsqrt(ms + EPS) * gain
        partner = jnp.where(first, pltpu.roll(c, LANES - 16, 1), pltpu.roll(c, 16, 1))
        return (c * cos + partner * sin) * scale

    nq = q_ref.shape[1] // LANES
    nk = k_ref.shape[1] // LANES
    for j in range(nq):
        c = y[:, j * LANES:(j + 1) * LANES]
        q_ref[:, j * LANES:(j + 1) * LANES] = norm_rope(c, qg_ref[...], Q_SCALE).astype(BF16)
    for j in range(nk):
        c = y[:, (nq + j) * LANES:(nq + j + 1) * LANES]
        k_ref[:, j * LANES:(j + 1) * LANES] = norm_rope(c, kg_ref[...], 1.0).astype(BF16)
    v_ref[...] = y[:, (nq + nk) * LANES:].astype(BF16)


def _proj_odd(x, g, w, qg, kg, cos, sin, seq, tm):
    n, d = x.shape
    nq = C_Q_HEADS * HEAD_DIM
    nk = C_KV_HEADS * LANES
    spb = seq // tm
    return pl.pallas_call(
        _proj_odd_kernel,
        grid=(n // tm,),
        in_specs=[pl.BlockSpec((tm, d), lambda i: (i, 0)),
                  pl.BlockSpec((1, d), lambda i: (0, 0)),
                  pl.BlockSpec((d, nq + 2 * nk), lambda i: (0, 0)),
                  pl.BlockSpec((1, LANES), lambda i: (0, 0)),
                  pl.BlockSpec((1, LANES), lambda i: (0, 0)),
                  pl.BlockSpec((tm, LANES), lambda i: (i % spb, 0)),
                  pl.BlockSpec((tm, LANES), lambda i: (i % spb, 0))],
        out_specs=[pl.BlockSpec((tm, nq), lambda i: (i, 0)),
                   pl.BlockSpec((tm, nk), lambda i: (i, 0)),
                   pl.BlockSpec((tm, nk), lambda i: (i, 0))],
        out_shape=[jax.ShapeDtypeStruct((n, nq), BF16),
                   jax.ShapeDtypeStruct((n, nk), BF16),
                   jax.ShapeDtypeStruct((n, nk), BF16)],
        compiler_params=_params("parallel"),
        name="proj_odd",
    )(x, g.reshape(1, d), w, qg, kg, cos, sin)


def _dil_kernel(*refs, nsub, mix):
    if mix:
        (q_ref, kp_ref, kc_ref, kn_ref, vp_ref, vc_ref, vn_ref, bias_ref,
         o0_ref, l0_ref, o1_ref, l1_ref, o_ref, kwin, vwin) = refs
    else:
        (q_ref, kp_ref, kc_ref, kn_ref, vp_ref, vc_ref, vn_ref, bias_ref,
         o_ref, lse_ref, kwin, vwin) = refs
    tq = q_ref.shape[1]
    i = pl.program_id(2)
    nblk = pl.num_programs(2)
    kwin[0:A_HALF] = kp_ref[0]
    kwin[A_HALF:A_HALF + tq] = kc_ref[0]
    kwin[A_HALF + tq:] = kn_ref[0]
    vwin[0:A_HALF] = vp_ref[0]
    vwin[A_HALF:A_HALF + tq] = vc_ref[0]
    vwin[A_HALF + tq:] = vn_ref[0]
    lo, hi = _half_masks(LANES)
    lane = lax.broadcasted_iota(jnp.int32, (LANES, LANES), 1)
    kj = lax.broadcasted_iota(jnp.int32, (LANES, 2 * LANES), 1)
    zero = jnp.zeros((), BF16)
    for a in range(nsub):
        rows = slice(a * LANES, (a + 1) * LANES)
        valid = None
        if a == 0:
            valid = kj >= jnp.where(i == 0, A_HALF, 0)
        if a == nsub - 1:
            v2 = kj < jnp.where(i == nblk - 1, LANES + A_HALF, 2 * LANES)
            valid = v2 if valid is None else jnp.logical_and(valid, v2)
        lse_all = jnp.zeros((LANES, LANES), F32)
        for p in range(A_HEADS // 2):
            cols = slice(p * LANES, (p + 1) * LANES)
            qp = q_ref[0, rows, cols]
            kk = kwin[a * LANES:(a + 2) * LANES, cols]
            vv = vwin[a * LANES:(a + 2) * LANES, cols]
            halves = []
            for hf in range(2):
                h = 2 * p + hf
                qm = jnp.where(lo if hf == 0 else hi, qp, zero)
                s = _dot_nt(qm, kk) + bias_ref[h]
                if valid is not None:
                    s = jnp.where(valid, s, NEG)
                o, m, l = _softmax_pv(s, vv)
                lse = m + jnp.log(l)
                if mix:
                    la = l0_ref[0, rows, 16 * h:16 * h + 1]
                    lb = l1_ref[0, rows, 16 * h:16 * h + 1]
                    mx = jnp.maximum(jnp.maximum(la, lb), lse)
                    wa = jnp.exp(la - mx)
                    wb = jnp.exp(lb - mx)
                    wc = jnp.exp(lse - mx)
                    den = wa + wb + wc
                    o = ((wa / den) * o0_ref[0, rows, cols].astype(F32)
                         + (wb / den) * o1_ref[0, rows, cols].astype(F32)
                         + (wc / den) * o)
                else:
                    lse_all = jnp.where(lane // 16 == h, lse, lse_all)
                halves.append(o)
            o_ref[0, rows, cols] = jnp.where(lo, halves[0], halves[1]).astype(o_ref.dtype)
        if not mix:
            lse_ref[0, rows, :] = lse_all


def _dilated_pattern(pr, batch, seq, dil, bias, prev):
    n = batch * seq
    sub = seq // dil
    tq = min(sub, 4 * LANES)
    nsub = tq // LANES
    nblk = sub // tq
    hb = tq // A_HALF
    nhalo = sub // A_HALF
    ncol = pr.shape[1] // A_W
    prv = pr.reshape(batch, sub, dil * pr.shape[1])

    def main(c):
        return pl.BlockSpec((1, tq, A_W), lambda b, r, i: (b, i, r * ncol + c))

    def before(c):
        return pl.BlockSpec((1, A_HALF, A_W),
                            lambda b, r, i: (b, jnp.maximum(i * hb - 1, 0), r * ncol + c))

    def after(c):
        return pl.BlockSpec((1, A_HALF, A_W),
                            lambda b, r, i: (b, jnp.minimum((i + 1) * hb, nhalo - 1), r * ncol + c))

    o_spec = pl.BlockSpec((1, tq, A_W), lambda b, r, i: (b, i, r))
    l_spec = pl.BlockSpec((1, tq, LANES), lambda b, r, i: (b, i, r))
    in_specs = [main(0), before(1), main(1), after(1), before(2), main(2), after(2),
                pl.BlockSpec((A_HEADS, LANES, 2 * LANES), lambda b, r, i: (0, 0, 0))]
    args = [prv] * 7 + [bias]
    o_shape = jax.ShapeDtypeStruct((batch, sub, dil * A_W), BF16)
    l_shape = jax.ShapeDtypeStruct((batch, sub, dil * LANES), F32)
    if prev is None:
        out_specs, out_shape = [o_spec, l_spec], [o_shape, l_shape]
    else:
        o0, l0, o1, l1 = prev
        in_specs += [o_spec, l_spec, o_spec, l_spec]
        args += [o0.reshape(o_shape.shape), l0.reshape(l_shape.shape),
                 o1.reshape(o_shape.shape), l1.reshape(l_shape.shape)]
        out_specs, out_shape = o_spec, o_shape
    out = pl.pallas_call(
        functools.partial(_dil_kernel, nsub=nsub, mix=prev is not None),
        grid=(batch, dil, nblk),
        in_specs=in_specs,
        out_specs=out_specs,
        out_shape=out_shape,
        scratch_shapes=[pltpu.VMEM((tq + 2 * A_HALF, A_W), BF16),
                        pltpu.VMEM((tq + 2 * A_HALF, A_W), BF16)],
        compiler_params=_params("parallel", "parallel", "parallel"),
        name=f"dilated_d{dil}",
    )(*args)
    if prev is None:
        return out[0].reshape(n, A_W), out[1].reshape(n, LANES)
    return out.reshape(n, A_W)


def _dilated(pr, batch, seq, biases):
    o0, l0 = _dilated_pattern(pr, batch, seq, A_PATTERNS[0][1], biases[0], None)
    o1, l1 = _dilated_pattern(pr, batch, seq, A_PATTERNS[1][1], biases[1], None)
    return _dilated_pattern(pr, batch, seq, A_PATTERNS[2][1], biases[2], (o0, l0, o1, l1))


def _na_kernel(q_ref, k0_ref, k1_ref, k2_ref, v0_ref, v1_ref, v2_ref, bias_ref, o_ref):
    tq = q_ref.shape[1]
    lo, hi = _half_masks(tq)
    zero = jnp.zeros((), BF16)
    for p in range(B_HEADS // 2):
        cols = slice(p * LANES, (p + 1) * LANES)
        qp = q_ref[0, :, cols]
        kk = jnp.concatenate([k0_ref[0, :, cols], k1_ref[0, :, cols], k2_ref[0, :, cols]], axis=0)
        vv = jnp.concatenate([v0_ref[0, :, cols], v1_ref[0, :, cols], v2_ref[0, :, cols]], axis=0)
        halves = []
        for hf in range(2):
            qm = jnp.where(lo if hf == 0 else hi, qp, zero)
            s = _dot_nt(qm, kk) + bias_ref[0, 2 * p + hf]
            o, _, _ = _softmax_pv(s, vv)
            halves.append(o)
        o_ref[0, :, cols] = jnp.where(lo, halves[0], halves[1]).astype(o_ref.dtype)


def _natten(pr, batch, seq, bias):
    n = batch * seq
    tq = NA_QROWS * GRID_W
    nblk = seq // tq
    nkb = NA_KROWS // NA_QROWS
    ncol = pr.shape[1] // A_W
    prv = pr.reshape(batch, seq, pr.shape[1])

    def kv_spec(c, j):
        return pl.BlockSpec((1, tq, A_W),
                            lambda b, i: (b, jnp.clip(i - 1, 0, nblk - nkb) + j, c))

    def variant(i):
        return jnp.where(i == 0, 0, jnp.where(i == nblk - 1, 2, 1))

    in_specs = ([pl.BlockSpec((1, tq, A_W), lambda b, i: (b, i, 3))]
                + [kv_spec(4, j) for j in range(nkb)] + [kv_spec(5, j) for j in range(nkb)]
                + [pl.BlockSpec((1, B_HEADS, tq, nkb * tq), lambda b, i: (variant(i), 0, 0, 0))])
    out = pl.pallas_call(
        _na_kernel,
        grid=(batch, nblk),
        in_specs=in_specs,
        out_specs=pl.BlockSpec((1, tq, A_W), lambda b, i: (b, i, 0)),
        out_shape=jax.ShapeDtypeStruct((batch, seq, A_W), BF16),
        compiler_params=_params("parallel", "parallel"),
        name="natten",
    )(*([prv] * (1 + 2 * nkb) + [bias]))
    return out.reshape(n, A_W)


def _gqa_kernel(q_ref, k_ref, v_ref, o_ref, *, tk):
    tq = q_ref.shape[1]
    nchunk = k_ref.shape[1] // tk
    lo, hi = _half_masks(tq)
    zero = jnp.zeros((), BF16)
    parts = []
    for p in range(2):
        qp = q_ref[0, :, p * LANES:(p + 1) * LANES]
        parts += [jnp.where(lo, qp, zero), jnp.where(hi, qp, zero)]
    qs = jnp.concatenate(parts, axis=0)
    rows = 4 * tq

    def body(c, carry):
        m, l, acc = carry
        start = pl.multiple_of(c * tk, tk)
        kc = k_ref[0, pl.ds(start, tk), :]
        vc = v_ref[0, pl.ds(start, tk), :]
        s = _dot_nt(qs, kc)
        m_new = jnp.maximum(m, jnp.max(s, axis=1, keepdims=True))
        alpha = jnp.exp(m - m_new)
        e = jnp.exp(s - m_new)
        l = alpha * l + jnp.sum(e, axis=1, keepdims=True)
        acc = alpha * acc + _dot(e.astype(BF16), vc)
        return m_new, l, acc

    init = (jnp.full((rows, 1), NEG, F32), jnp.zeros((rows, 1), F32), jnp.zeros((rows, LANES), F32))
    _, l, acc = lax.fori_loop(0, nchunk, body, init)
    o = acc / l
    for p in range(2):
        even = o[(2 * p) * tq:(2 * p + 1) * tq]
        odd = o[(2 * p + 1) * tq:(2 * p + 2) * tq]
        o_ref[0, :, p * LANES:(p + 1) * LANES] = jnp.where(lo, even, odd).astype(o_ref.dtype)


def _gqa(q, kk, vv, batch, seq, tq=128, tk=512):
    n = batch * seq
    gw = (C_Q_HEADS // C_KV_HEADS) * HEAD_DIM
    out = pl.pallas_call(
        functools.partial(_gqa_kernel, tk=tk),
        grid=(batch, C_KV_HEADS, seq // tq),
        in_specs=[pl.BlockSpec((1, tq, gw), lambda b, g, i: (b, i, g)),
                  pl.BlockSpec((1, seq, LANES), lambda b, g, i: (b, 0, g)),
                  pl.BlockSpec((1, seq, LANES), lambda b, g, i: (b, 0, g))],
        out_specs=pl.BlockSpec((1, tq, gw), lambda b, g, i: (b, i, g)),
        out_shape=jax.ShapeDtypeStruct((batch, seq, C_Q_HEADS * HEAD_DIM), BF16),
        compiler_params=_params("parallel", "parallel", "parallel"),
        name="gqa",
    )(q.reshape(batch, seq, -1), kk.reshape(batch, seq, -1), vv.reshape(batch, seq, -1))
    return out.reshape(n, C_Q_HEADS * HEAD_DIM)


def _post_kernel(*refs, route):
    if route:
        (x_ref, a1_ref, a2_ref, wo1_ref, wo2_ref, gxa_ref, wq_ref, kv_ref, wox_ref, gffn_ref,
         wr_ref, x2_ref, hn_ref, info_ref) = refs
    else:
        (x_ref, a1_ref, a2_ref, wo1_ref, wo2_ref, gxa_ref, wq_ref, kv_ref, wox_ref, gffn_ref,
         x2_ref, hn_ref) = refs
    x1 = x_ref[...] + _dot(a1_ref[...], wo1_ref[...]) + _dot(a2_ref[...], wo2_ref[...])
    tm = x1.shape[0]
    h = _rms(x1, gxa_ref[...]).astype(BF16)
    q = _dot(h, wq_ref[...]).astype(BF16)
    lo, hi = _half_masks(tm)
    zero = jnp.zeros((), BF16)
    outs = []
    for p in range(X_HEADS // 2):
        qp = q[:, p * LANES:(p + 1) * LANES]
        kk = kv_ref[0, :, p * LANES:(p + 1) * LANES]
        vv = kv_ref[0, :, X_W + p * LANES:X_W + (p + 1) * LANES]
        halves = []
        for hf in range(2):
            qm = jnp.where(lo if hf == 0 else hi, qp, zero)
            o, _, _ = _softmax_pv(_dot_nt(qm, kk), vv)
            halves.append(o)
        outs.append(jnp.where(lo, halves[0], halves[1]).astype(BF16))
    x2 = x1 + _dot(jnp.concatenate(outs, axis=1), wox_ref[...])
    x2_ref[...] = x2
    hf32 = _rms(x2, gffn_ref[...])
    hn_ref[...] = hf32.astype(hn_ref.dtype)
    if route:
        logits = jnp.dot(hf32, wr_ref[...], preferred_element_type=F32,
                         precision=lax.Precision.HIGHEST)
        lane = lax.broadcasted_iota(jnp.int32, logits.shape, 1).astype(F32)
        lg = jnp.where(lane < N_EXPERTS, logits, NEG)
        m1 = jnp.max(lg, axis=1, keepdims=True)
        i1 = jnp.min(jnp.where(lg == m1, lane, float(LANES)), axis=1, keepdims=True)
        lg2 = jnp.where(lane == i1, NEG, lg)
        m2 = jnp.max(lg2, axis=1, keepdims=True)
        i2 = jnp.min(jnp.where(lg2 == m2, lane, float(LANES)), axis=1, keepdims=True)
        e = jnp.exp(m2 - m1)
        den = 1.0 + e
        info_ref[...] = (jnp.where(lane == 0.0, i1, 0.0) + jnp.where(lane == 1.0, i2, 0.0)
                         + jnp.where(lane == 2.0, 1.0 / den, 0.0) + jnp.where(lane == 3.0, e / den, 0.0))


def _post(x, a1, a2, c1, c2, wo1, wo2, gxa, wq, kvm, wox, gffn, wr, seq, tm):
    n, d = x.shape
    spb = seq // tm
    row = lambda i: (i, 0)
    const = lambda i: (0, 0)
    in_specs = [pl.BlockSpec((tm, d), row),
                pl.BlockSpec((tm, A_W), lambda i: (i, c1)),
                pl.BlockSpec((tm, A_W), lambda i: (i, c2)),
                pl.BlockSpec((A_W, d), const),
                pl.BlockSpec((A_W, d), const),
                pl.BlockSpec((1, d), const),
                pl.BlockSpec((d, X_W), const),
                pl.BlockSpec((1, N_MEM, 2 * X_W), lambda i: (i // spb, 0, 0)),
                pl.BlockSpec((X_W, d), const),
                pl.BlockSpec((1, d), const)]
    args = [x, a1, a2, wo1, wo2, gxa.reshape(1, d), wq, kvm, wox, gffn.reshape(1, d)]
    out_specs = [pl.BlockSpec((tm, d), row), pl.BlockSpec((tm, d), row)]
    out_shape = [jax.ShapeDtypeStruct((n, d), F32),
                 jax.ShapeDtypeStruct((n, d), BF16 if wr is None else F32)]
    if wr is not None:
        in_specs.append(pl.BlockSpec((d, LANES), const))
        args.append(wr)
        out_specs.append(pl.BlockSpec((tm, LANES), row))
        out_shape.append(jax.ShapeDtypeStruct((n, LANES), F32))
    return pl.pallas_call(
        functools.partial(_post_kernel, route=wr is not None),
        grid=(n // tm,),
        in_specs=in_specs,
        out_specs=out_specs,
        out_shape=out_shape,
        compiler_params=_params("parallel"),
        name="post_route" if wr is not None else "post",
    )(*args)


def _swiglu_chunk(h, wg, wu, wd):
    g = _dot(h, wg)
    u = _dot(h, wu)
    a = (g / (1.0 + jnp.exp(-g))) * u
    return _dot(a.astype(BF16), wd)


def _ffn_kernel(x_ref, h_ref, wg_ref, wu_ref, wd_ref, o_ref):
    f = pl.program_id(1)
    y = _swiglu_chunk(h_ref[...], wg_ref[...], wu_ref[...], wd_ref[...])

    @pl.when(f == 0)
    def _():
        o_ref[...] = x_ref[...] + y

    @pl.when(f > 0)
    def _():
        o_ref[...] += y


def _ffn(x, hn, wg, wu, wd, tm, tf):
    n, d = x.shape
    ff = wg.shape[1]
    return pl.pallas_call(
        _ffn_kernel,
        grid=(n // tm, ff // tf),
        in_specs=[pl.BlockSpec((tm, d), lambda i, f: (i, 0)),
                  pl.BlockSpec((tm, d), lambda i, f: (i, 0)),
                  pl.BlockSpec((d, tf), lambda i, f: (0, f)),
                  pl.BlockSpec((d, tf), lambda i, f: (0, f)),
                  pl.BlockSpec((tf, d), lambda i, f: (f, 0))],
        out_specs=pl.BlockSpec((tm, d), lambda i, f: (i, 0)),
        out_shape=jax.ShapeDtypeStruct((n, d), F32),
        compiler_params=_params("parallel", "arbitrary"),
        name="ffn",
    )(x, hn, wg, wu, wd)


def _route(info, tmg):
    n = info.shape[0]
    ea = info[:, 0:2].astype(jnp.int32).reshape(-1)
    onehot = (ea[:, None] == jnp.arange(N_EXPERTS, dtype=jnp.int32)[None, :]).astype(jnp.int32)
    csum = jnp.cumsum(onehot, axis=0)
    counts = csum[-1]
    padded = (counts + tmg - 1) // tmg * tmg
    ends = jnp.cumsum(padded)
    starts = ends - padded
    dest = jnp.sum(onehot * (csum - 1 + starts[None, :]), axis=1)
    r_max = 2 * n + N_EXPERTS * tmg
    token = jnp.arange(2 * n, dtype=jnp.int32) // 2
    src = jnp.zeros((r_max,), jnp.int32).at[dest].set(token, unique_indices=True)
    tile_start = jnp.arange(r_max // tmg, dtype=jnp.int32) * tmg
    tile_e = jnp.sum((tile_start[:, None] >= ends[None, :]).astype(jnp.int32), axis=1)
    tile_e = jnp.minimum(tile_e, N_EXPERTS - 1)
    tile_ok = (tile_start < ends[-1]).astype(jnp.int32)
    return src, dest, tile_e, tile_ok


def _row_copy(src_ref, row, dst_ref, r, sem):
    return pltpu.make_async_copy(src_ref.at[pl.ds(row, 1)], dst_ref.at[pl.ds(r, 1)], sem)


def _gather_kernel(idx_ref, src_ref, o_ref, sem):
    tg = o_ref.shape[0]

    def start(r, c):
        _row_copy(src_ref, idx_ref[0, 0, r], o_ref, r, sem).start()
        return c

    def wait(r, c):
        _row_copy(src_ref, idx_ref[0, 0, r], o_ref, r, sem).wait()
        return c

    lax.fori_loop(0, tg, start, 0, unroll=8)
    lax.fori_loop(0, tg, wait, 0, unroll=8)


def _gather_rows(src, idx, tg):
    n, d = src.shape
    r = idx.shape[0]
    return pl.pallas_call(
        _gather_kernel,
        grid=(r // tg,),
        in_specs=[pl.BlockSpec((1, 1, tg), lambda i: (i, 0, 0), memory_space=pltpu.SMEM),
                  pl.BlockSpec(memory_space=pl.ANY)],
        out_specs=pl.BlockSpec((tg, d), lambda i: (i, 0)),
        out_shape=jax.ShapeDtypeStruct((r, d), src.dtype),
        scratch_shapes=[pltpu.SemaphoreType.DMA(())],
        compiler_params=_params("arbitrary"),
        name="moe_gather",
    )(idx.reshape(r // tg, 1, tg), src)


def _expert_kernel(te_ref, ok_ref, x_ref, wg_ref, wu_ref, wd_ref, o_ref, hb_ref):
    t = pl.program_id(0)
    f = pl.program_id(1)
    ok = ok_ref[t] == 1

    @pl.when(jnp.logical_and(ok, f == 0))
    def _():
        hb_ref[...] = x_ref[...].astype(BF16)
        o_ref[...] = _swiglu_chunk(hb_ref[...], wg_ref[0], wu_ref[0], wd_ref[0])

    @pl.when(jnp.logical_and(ok, f > 0))
    def _():
        o_ref[...] += _swiglu_chunk(hb_ref[...], wg_ref[0], wu_ref[0], wd_ref[0])

    @pl.when(jnp.logical_and(jnp.logical_not(ok), f == 0))
    def _():
        o_ref[...] = jnp.zeros_like(o_ref)


def _expert_ffn(xs, tile_e, tile_ok, wg, wu, wd, tmg, tf):
    r, d = xs.shape
    ff = wg.shape[2]
    grid_spec = pltpu.PrefetchScalarGridSpec(
        num_scalar_prefetch=2,
        grid=(r // tmg, ff // tf),
        in_specs=[pl.BlockSpec((tmg, d), lambda t, f, te, ok: (t, 0)),
                  pl.BlockSpec((1, d, tf), lambda t, f, te, ok: (te[t], 0, f * ok[t])),
                  pl.BlockSpec((1, d, tf), lambda t, f, te, ok: (te[t], 0, f * ok[t])),
                  pl.BlockSpec((1, tf, d), lambda t, f, te, ok: (te[t], f * ok[t], 0))],
        out_specs=pl.BlockSpec((tmg, d), lambda t, f, te, ok: (t, 0)),
        scratch_shapes=[pltpu.VMEM((tmg, d), BF16)],
    )
    return pl.pallas_call(
        _expert_kernel,
        grid_spec=grid_spec,
        out_shape=jax.ShapeDtypeStruct((r, d), F32),
        compiler_params=_params("arbitrary", "arbitrary"),
        name="moe_experts",
    )(tile_e, tile_ok, xs, wg, wu, wd)


def _combine_kernel(d_ref, x_ref, info_ref, ys_ref, o_ref, buf, sem):
    tc = o_ref.shape[0]

    def copies(r):
        return (_row_copy(ys_ref, d_ref[0, 0, 2 * r], buf.at[0], r, sem),
                _row_copy(ys_ref, d_ref[0, 0, 2 * r + 1], buf.at[1], r, sem))

    def start(r, c):
        for cp in copies(r):
            cp.start()
        return c

    def wait(r, c):
        for cp in copies(r):
            cp.wait()
        return c

    lax.fori_loop(0, tc, start, 0, unroll=4)
    lax.fori_loop(0, tc, wait, 0, unroll=4)
    g1 = info_ref[:, 2:3]
    g2 = info_ref[:, 3:4]
    o_ref[...] = x_ref[...] + (g1 * buf[0] + g2 * buf[1])


def _combine(x, info, dest, ys, tc):
    n, d = x.shape
    return pl.pallas_call(
        _combine_kernel,
        grid=(n // tc,),
        in_specs=[pl.BlockSpec((1, 1, 2 * tc), lambda i: (i, 0, 0), memory_space=pltpu.SMEM),
                  pl.BlockSpec((tc, d), lambda i: (i, 0)),
                  pl.BlockSpec((tc, LANES), lambda i: (i, 0)),
                  pl.BlockSpec(memory_space=pl.ANY)],
        out_specs=pl.BlockSpec((tc, d), lambda i: (i, 0)),
        out_shape=jax.ShapeDtypeStruct((n, d), F32),
        scratch_shapes=[pltpu.VMEM((2, tc, d), F32), pltpu.SemaphoreType.DMA(())],
        compiler_params=_params("arbitrary"),
        name="moe_combine",
    )(dest.reshape(n // tc, 1, 2 * tc), x, info, ys)


def _moe(x2, hf, info, wg, wu, wd, tmg, tf, tg, tc):
    src, dest, tile_e, tile_ok = _route(info, tmg)
    xs = _gather_rows(hf, src, tg)
    ys = _expert_ffn(xs, tile_e, tile_ok, wg, wu, wd, tmg, tf)
    return _combine(x2, info, dest, ys, tc)


def _final_norm_kernel(x_ref, g_ref, o_ref):
    o_ref[...] = _rms(x_ref[...], g_ref[...])


def _final_norm(x, g, tm):
    n, d = x.shape
    return pl.pallas_call(
        _final_norm_kernel,
        grid=(n // tm,),
        in_specs=[pl.BlockSpec((tm, d), lambda i: (i, 0)), pl.BlockSpec((1, d), lambda i: (0, 0))],
        out_specs=pl.BlockSpec((tm, d), lambda i: (i, 0)),
        out_shape=jax.ShapeDtypeStruct((n, d), F32),
        compiler_params=_params("parallel"),
        name="final_norm",
    )(x, g.reshape(1, d))


def _t5_bucket_np(rel):
    nb = N_BUCKETS // 2
    max_exact = nb // 2
    ret = np.where(rel > 0, nb, 0)
    n = np.abs(rel)
    large = max_exact + (np.log(np.maximum(n, 1).astype(np.float32) / max_exact)
                         / math.log(MAX_DISTANCE / max_exact) * (nb - max_exact)).astype(np.int32)
    large = np.minimum(large, nb - 1)
    return ret + np.where(n < max_exact, n, large)


def _dilated_bias(rel_table, dil):
    off = np.arange(2 * LANES)[None, :] - A_HALF - np.arange(LANES)[:, None]
    bucket = _t5_bucket_np(off * dil)
    bias = rel_table[jnp.asarray(bucket)].transpose(2, 0, 1).astype(F32)
    return jnp.where(jnp.asarray(np.abs(off) <= A_HALF)[None], bias, NEG)


def _natten_bias(rpb):
    rows = 4 * NA_KROWS
    c = np.arange(GRID_W)
    cs = np.clip(c - NA_KW // 2, 0, GRID_W - NA_KW)
    col_ok = (c[None, :] >= cs[:, None]) & (c[None, :] < cs[:, None] + NA_KW)
    dc = np.clip(c[None, :] - c[:, None], -(NA_KW - 1), NA_KW - 1) + NA_KW - 1
    col_sel = (dc[:, :, None] == np.arange(2 * NA_KW - 1)).astype(np.float32)
    row_sel, row_ok = [], []
    for r0 in (0, 2 * NA_QROWS, rows - NA_QROWS):
        ks = int(np.clip(r0 - NA_KH // 2, 0, rows - NA_KROWS))
        qr = r0 + np.arange(NA_QROWS)
        kr = ks + np.arange(NA_KROWS)
        rs = np.clip(qr - NA_KH // 2, 0, rows - NA_KH)
        row_ok.append((kr[None, :] >= rs[:, None]) & (kr[None, :] < rs[:, None] + NA_KH))
        dr = kr[None, :] - qr[:, None] + NA_KH - 1
        row_sel.append((dr[:, :, None] == np.arange(2 * NA_KH - 1)).astype(np.float32))
    row_sel = np.stack(row_sel)
    ok = np.stack(row_ok)[:, :, None, :, None] & col_ok[None, None, :, None, :]
    by_col = jnp.einsum('hab,xyb->haxy', rpb.astype(F32), col_sel, precision=lax.Precision.HIGHEST)
    table = jnp.einsum('vqka,haxy->vhqxky', row_sel, by_col, precision=lax.Precision.HIGHEST)
    table = jnp.where(jnp.asarray(ok)[:, None], table, NEG)
    tq = NA_QROWS * GRID_W
    return table.reshape(len(row_ok), rpb.shape[0], tq, NA_KROWS * GRID_W)


def _rope_tables(seq):
    t = jnp.arange(seq)
    half = HEAD_DIM // 2
    freqs = ROPE_THETA ** (-jnp.arange(0, half, 2, dtype=F32) / half)
    ang_r = (t // GRID_W).astype(F32)[:, None] * freqs[None, :]
    ang_c = (t % GRID_W).astype(F32)[:, None] * freqs[None, :]
    cos_h = jnp.concatenate([jnp.cos(ang_r)] * 2 + [jnp.cos(ang_c)] * 2, axis=1)
    sin_h = jnp.concatenate([-jnp.sin(ang_r), jnp.sin(ang_r), -jnp.sin(ang_c), jnp.sin(ang_c)], axis=1)
    return jnp.concatenate([cos_h, cos_h], axis=1), jnp.concatenate([sin_h, sin_h], axis=1)


def _prepare(p):
    w = {}
    ev_in = p['ev_w_in']
    scale = np.ones((ev_in.shape[-1],), np.float32)
    scale[0:A_W] = Q_SCALE
    scale[3 * A_W:4 * A_W] = Q_SCALE
    w['ev_w_in'] = (ev_in * scale).astype(BF16)
    w['ev_w_out'] = p['ev_w_out'].astype(BF16)
    w['ev_w_gate'] = p['ev_w_gate'].astype(BF16)
    w['ev_w_up'] = p['ev_w_up'].astype(BF16)
    w['ev_w_down'] = p['ev_w_down'].astype(BF16)
    nq = C_Q_HEADS * HEAD_DIM
    nkv = C_KV_HEADS * HEAD_DIM
    dup = np.repeat(np.arange(C_KV_HEADS), 2)[:, None] * HEAD_DIM + np.arange(HEAD_DIM)[None, :]
    dup = dup.reshape(-1)
    cols = np.concatenate([np.arange(nq), nq + dup, nq + nkv + dup])
    w['od_w_in'] = p['od_w_in'][:, :, cols].astype(BF16)
    w['od_w_out'] = p['od_w_out'].astype(BF16)
    w['od_q_norm'] = jnp.concatenate([p['od_q_norm']] * 2, axis=-1)[:, None, :]
    w['od_k_norm'] = jnp.concatenate([p['od_k_norm']] * 2, axis=-1)[:, None, :]
    w['od_router'] = jnp.pad(p['od_router'], ((0, 0), (0, 0), (0, LANES - N_EXPERTS)))
    w['od_moe_gate'] = p['od_moe_gate'].astype(BF16)
    w['od_moe_up'] = p['od_moe_up'].astype(BF16)
    w['od_moe_down'] = p['od_moe_down'].astype(BF16)
    w['xa_w_q'] = (p['xa_w_q'] * Q_SCALE).astype(BF16)
    w['xa_w_kv'] = p['xa_w_kv'].astype(BF16)
    w['xa_w_out'] = p['xa_w_out'].astype(BF16)
    w['dil_bias'] = [_dilated_bias(p['rel_table'], dil) for _, dil in A_PATTERNS]
    w['na_bias'] = [_natten_bias(p['ev_na_rpb'][i]) for i in range(p['ev_na_rpb'].shape[0])]
    return w


def _trunk(x, mem, p, w):
    batch, seq, d = x.shape
    n = batch * seq
    xf = x.reshape(n, d)
    memf = mem.reshape(batch * N_MEM, d)
    cos, sin = _rope_tables(seq)
    depth = p['xa_norm'].shape[0]
    for layer in range(depth):
        i = layer // 2
        if layer % 2 == 0:
            pr = _proj(xf, p['ev_norm_mix'][i], w['ev_w_in'][i], tm=512)
            a1 = _dilated(pr, batch, seq, w['dil_bias'])
            a2 = _natten(pr, batch, seq, w['na_bias'][i])
            c1 = c2 = 0
            wo = w['ev_w_out'][i]
            wr = None
        else:
            q, kk, vv = _proj_odd(xf, p['od_norm_mix'][i], w['od_w_in'][i], w['od_q_norm'][i],
                                  w['od_k_norm'][i], cos, sin, seq, tm=512)
            a1 = a2 = _gqa(q, kk, vv, batch, seq)
            c1, c2 = 0, 1
            wo = w['od_w_out'][i]
            wr = w['od_router'][i]
        kvm = _proj(memf, p['xa_mem_norm'][layer], w['xa_w_kv'][layer], tm=512)
        kvm = kvm.reshape(batch, N_MEM, 2 * X_W)
        gffn = p['ev_norm_ffn'][i] if layer % 2 == 0 else p['od_norm_ffn'][i]
        outs = _post(xf, a1, a2, c1, c2, wo[:A_W], wo[A_W:], p['xa_norm'][layer], w['xa_w_q'][layer],
                     kvm, w['xa_w_out'][layer], gffn, wr, seq, tm=512)
        if layer % 2 == 0:
            x2, hn = outs
            xf = _ffn(x2, hn, w['ev_w_gate'][i], w['ev_w_up'][i], w['ev_w_down'][i], tm=1024, tf=256)
        else:
            x2, hf, info = outs
            xf = _moe(x2, hf, info, w['od_moe_gate'][i], w['od_moe_up'][i], w['od_moe_down'][i],
                      tmg=1024, tf=512, tg=512, tc=256)
    return _final_norm(xf, p['final_norm'], tm=1024).reshape(batch, seq, d)


def kernel(x_prompt, x_sample, mem_prompt, mem_sample, rel_table, ev_norm_mix, ev_w_in, ev_na_rpb, ev_w_out, ev_norm_ffn, ev_w_gate, ev_w_up, ev_w_down, od_norm_mix, od_w_in, od_q_norm, od_k_norm, od_w_out, od_norm_ffn, od_router, od_moe_gate, od_moe_up, od_moe_down, xa_norm, xa_mem_norm, xa_w_q, xa_w_kv, xa_w_out, final_norm):
    p = dict(rel_table=rel_table,
             ev_norm_mix=ev_norm_mix, ev_w_in=ev_w_in, ev_na_rpb=ev_na_rpb, ev_w_out=ev_w_out,
             ev_norm_ffn=ev_norm_ffn, ev_w_gate=ev_w_gate, ev_w_up=ev_w_up, ev_w_down=ev_w_down,
             od_norm_mix=od_norm_mix, od_w_in=od_w_in, od_q_norm=od_q_norm, od_k_norm=od_k_norm,
             od_w_out=od_w_out, od_norm_ffn=od_norm_ffn, od_router=od_router,
             od_moe_gate=od_moe_gate, od_moe_up=od_moe_up, od_moe_down=od_moe_down,
             xa_norm=xa_norm, xa_mem_norm=xa_mem_norm, xa_w_q=xa_w_q, xa_w_kv=xa_w_kv,
             xa_w_out=xa_w_out, final_norm=final_norm)
    w = _prepare(p)
    return (_trunk(x_prompt, mem_prompt, p, w), _trunk(x_sample, mem_sample, p, w))
```

```python
import functools
import math

import numpy as np
import jax
import jax.numpy as jnp
from jax import lax
from jax.experimental import pallas as pl
from jax.experimental.pallas import tpu as pltpu

F32 = jnp.float32
BF16 = jnp.bfloat16

D_MODEL = 1024
HEAD_DIM = 64
GRID_W = 64
N_MEM = 256
A_HEADS = 8
A_PATTERNS = ((128, 1), (512, 4), (2048, 16))
A_HALF = 64
B_HEADS = 8
NA_KH = 8
NA_KW = 16
C_Q_HEADS = 16
C_KV_HEADS = 4
ROPE_THETA = 10000.0
N_BUCKETS = 32
MAX_DISTANCE = 1024
X_HEADS = 4
N_EXPERTS = 8
EPS = 1e-6
NEG = -1e30
A_W = A_HEADS * HEAD_DIM
X_W = X_HEADS * HEAD_DIM
Q_SCALE = HEAD_DIM ** -0.5
QK_SCALE = Q_SCALE * math.log2(math.e)

LANES = 128
MXU_N = 256
GQA_ONES_ROWS = 16
VMEM_LIMIT = 48 * 1024 * 1024

NA_QROWS = 4
NA_KROWS = 12


def _params(*sem):
    return pltpu.CompilerParams(dimension_semantics=sem, vmem_limit_bytes=VMEM_LIMIT)


def _rms(x, g):
    ms = jnp.mean(x * x, axis=-1, keepdims=True)
    return x * lax.rsqrt(ms + EPS) * g


def _dot(a, b):
    return jnp.dot(a, b, preferred_element_type=F32)


def _dot_nt(a, b):
    return lax.dot_general(a, b, (((1,), (1,)), ((), ())), preferred_element_type=F32)


def _half_masks(rows):
    lane = lax.broadcasted_iota(jnp.int32, (rows, LANES), 1)
    lo = lane < HEAD_DIM
    return lo, jnp.logical_not(lo)


def _softmax_pv(s, v):
    m = jnp.max(s, axis=1, keepdims=True)
    e = jnp.exp(s - m)
    l = jnp.sum(e, axis=1, keepdims=True)
    o = _dot(e.astype(BF16), v) / l
    return o, m, l


def _proj_kernel(x_ref, g_ref, w_ref, o_ref):
    h = _rms(x_ref[...], g_ref[...]).astype(BF16)
    o_ref[...] = _dot(h, w_ref[...]).astype(o_ref.dtype)


def _proj(x, g, w, tm):
    n, d = x.shape
    nout = w.shape[1]
    return pl.pallas_call(
        _proj_kernel,
        grid=(n // tm,),
        in_specs=[pl.BlockSpec((tm, d), lambda i: (i, 0)),
                  pl.BlockSpec((1, d), lambda i: (0, 0)),
                  pl.BlockSpec((d, nout), lambda i: (0, 0))],
        out_specs=pl.BlockSpec((tm, nout), lambda i: (i, 0)),
        out_shape=jax.ShapeDtypeStruct((n, nout), BF16),
        compiler_params=_params("parallel"),
        name="proj",
    )(x, g.reshape(1, d), w)


def _proj_odd_kernel(x_ref, g_ref, w_ref, qg_ref, kg_ref, cos_ref, sin_ref, q_ref, k_ref, v_ref):
    h = _rms(x_ref[...], g_ref[...]).astype(BF16)
    y = _dot(h, w_ref[...])
    tm = y.shape[0]
    cos = cos_ref[...]
    sin = sin_ref[...]
    lane = lax.broadcasted_iota(jnp.int32, (tm, LANES), 1)
    lo = lane < HEAD_DIM
    first = (lane & 31) < 16

    def norm_rope(c, gain, scale):
        ss = c * c
        s_lo = jnp.sum(jnp.where(lo, ss, 0.0), axis=1, keepdims=True)
        s_hi = jnp.sum(jnp.where(lo, 0.0, ss), axis=1, keepdims=True)
        ms = jnp.where(lo, s_lo, s_hi) * (1.0 / HEAD_DIM)
        c = c * lax.rsqrt(ms + EPS) * gain
        partner = jnp.where(first, pltpu.roll(c, LANES - 16, 1), pltpu.roll(c, 16, 1))
        return (c * cos + partner * sin) * scale

    nq = q_ref.shape[1] // LANES
    nk = k_ref.shape[1] // LANES
    for j in range(nq):
        c = y[:, j * LANES:(j + 1) * LANES]
        q_ref[0, j * LANES:(j + 1) * LANES, :] = norm_rope(c, qg_ref[...], QK_SCALE).T.astype(BF16)
    for j in range(nk):
        c = y[:, (nq + j) * LANES:(nq + j + 1) * LANES]
        k_ref[:, j * LANES:(j + 1) * LANES] = norm_rope(c, kg_ref[...], 1.0).astype(BF16)
    for j in range(nk):
        c = y[:, (nq + nk + j) * LANES:(nq + nk + j + 1) * LANES]
        v_ref[0, j * LANES:(j + 1) * LANES, :] = c.T.astype(BF16)


def _proj_odd(x, g, w, qg, kg, cos, sin, seq, tm):
    n, d = x.shape
    nq = C_Q_HEADS * HEAD_DIM
    nk = C_KV_HEADS * HEAD_DIM
    spb = seq // tm
    batch = n // seq
    return pl.pallas_call(
        _proj_odd_kernel,
        grid=(n // tm,),
        in_specs=[pl.BlockSpec((tm, d), lambda i: (i, 0)),
                  pl.BlockSpec((1, d), lambda i: (0, 0)),
                  pl.BlockSpec((d, nq + 2 * nk), lambda i: (0, 0)),
                  pl.BlockSpec((1, LANES), lambda i: (0, 0)),
                  pl.BlockSpec((1, LANES), lambda i: (0, 0)),
                  pl.BlockSpec((tm, LANES), lambda i: (i % spb, 0)),
                  pl.BlockSpec((tm, LANES), lambda i: (i % spb, 0))],
        out_specs=[pl.BlockSpec((1, nq, tm), lambda i: (i // spb, 0, i % spb)),
                   pl.BlockSpec((tm, nk), lambda i: (i, 0)),
                   pl.BlockSpec((1, nk, tm), lambda i: (i // spb, 0, i % spb))],
        out_shape=[jax.ShapeDtypeStruct((batch, nq, seq), BF16),
                   jax.ShapeDtypeStruct((n, nk), BF16),
                   jax.ShapeDtypeStruct((batch, nk, seq), BF16)],
        compiler_params=_params("parallel"),
        name="proj_odd",
    )(x, g.reshape(1, d), w, qg, kg, cos, sin)


def _dil_kernel(*refs, nsub, mix):
    if mix:
        (q_ref, kp_ref, kc_ref, kn_ref, vp_ref, vc_ref, vn_ref, bias_ref,
         o0_ref, l0_ref, o1_ref, l1_ref, o_ref, kwin, vwin) = refs
    else:
        (q_ref, kp_ref, kc_ref, kn_ref, vp_ref, vc_ref, vn_ref, bias_ref,
         o_ref, lse_ref, kwin, vwin) = refs
    tq = q_ref.shape[1]
    i = pl.program_id(2)
    nblk = pl.num_programs(2)
    kwin[0:A_HALF] = kp_ref[0]
    kwin[A_HALF:A_HALF + tq] = kc_ref[0]
    kwin[A_HALF + tq:] = kn_ref[0]
    vwin[0:A_HALF] = vp_ref[0]
    vwin[A_HALF:A_HALF + tq] = vc_ref[0]
    vwin[A_HALF + tq:] = vn_ref[0]
    lo, hi = _half_masks(LANES)
    lane = lax.broadcasted_iota(jnp.int32, (LANES, LANES), 1)
    kj = lax.broadcasted_iota(jnp.int32, (LANES, 2 * LANES), 1)
    zero = jnp.zeros((), BF16)
    for a in range(nsub):
        rows = slice(a * LANES, (a + 1) * LANES)
        valid = None
        if a == 0:
            valid = kj >= jnp.where(i == 0, A_HALF, 0)
        if a == nsub - 1:
            v2 = kj < jnp.where(i == nblk - 1, LANES + A_HALF, 2 * LANES)
            valid = v2 if valid is None else jnp.logical_and(valid, v2)
        lse_all = jnp.zeros((LANES, LANES), F32)
        for p in range(A_HEADS // 2):
            cols = slice(p * LANES, (p + 1) * LANES)
            qp = q_ref[0, rows, cols]
            kk = kwin[a * LANES:(a + 2) * LANES, cols]
            vv = vwin[a * LANES:(a + 2) * LANES, cols]
            halves = []
            for hf in range(2):
                h = 2 * p + hf
                qm = jnp.where(lo if hf == 0 else hi, qp, zero)
                s = _dot_nt(qm, kk) + bias_ref[h]
                if valid is not None:
                    s = jnp.where(valid, s, NEG)
                o, m, l = _softmax_pv(s, vv)
                lse = m + jnp.log(l)
                if mix:
                    la = l0_ref[0, rows, 16 * h:16 * h + 1]
                    lb = l1_ref[0, rows, 16 * h:16 * h + 1]
                    mx = jnp.maximum(jnp.maximum(la, lb), lse)
                    wa = jnp.exp(la - mx)
                    wb = jnp.exp(lb - mx)
                    wc = jnp.exp(lse - mx)
                    den = wa + wb + wc
                    o = ((wa / den) * o0_ref[0, rows, cols].astype(F32)
                         + (wb / den) * o1_ref[0, rows, cols].astype(F32)
                         + (wc / den) * o)
                else:
                    lse_all = jnp.where(lane // 16 == h, lse, lse_all)
                halves.append(o)
            o_ref[0, rows, cols] = jnp.where(lo, halves[0], halves[1]).astype(o_ref.dtype)
        if not mix:
            lse_ref[0, rows, :] = lse_all


def _dilated_pattern(pr, batch, seq, dil, bias, prev):
    n = batch * seq
    sub = seq // dil
    tq = min(sub, 4 * LANES)
    nsub = tq // LANES
    nblk = sub // tq
    hb = tq // A_HALF
    nhalo = sub // A_HALF
    ncol = pr.shape[1] // A_W
    prv = pr.reshape(batch, sub, dil * pr.shape[1])

    def main(c):
        return pl.BlockSpec((1, tq, A_W), lambda b, r, i: (b, i, r * ncol + c))

    def before(c):
        return pl.BlockSpec((1, A_HALF, A_W),
                            lambda b, r, i: (b, jnp.maximum(i * hb - 1, 0), r * ncol + c))

    def after(c):
        return pl.BlockSpec((1, A_HALF, A_W),
                            lambda b, r, i: (b, jnp.minimum((i + 1) * hb, nhalo - 1), r * ncol + c))

    o_spec = pl.BlockSpec((1, tq, A_W), lambda b, r, i: (b, i, r))
    l_spec = pl.BlockSpec((1, tq, LANES), lambda b, r, i: (b, i, r))
    in_specs = [main(0), before(1), main(1), after(1), before(2), main(2), after(2),
                pl.BlockSpec((A_HEADS, LANES, 2 * LANES), lambda b, r, i: (0, 0, 0))]
    args = [prv] * 7 + [bias]
    o_shape = jax.ShapeDtypeStruct((batch, sub, dil * A_W), BF16)
    l_shape = jax.ShapeDtypeStruct((batch, sub, dil * LANES), F32)
    if prev is None:
        out_specs, out_shape = [o_spec, l_spec], [o_shape, l_shape]
    else:
        o0, l0, o1, l1 = prev
        in_specs += [o_spec, l_spec, o_spec, l_spec]
        args += [o0.reshape(o_shape.shape), l0.reshape(l_shape.shape),
                 o1.reshape(o_shape.shape), l1.reshape(l_shape.shape)]
        out_specs, out_shape = o_spec, o_shape
    out = pl.pallas_call(
        functools.partial(_dil_kernel, nsub=nsub, mix=prev is not None),
        grid=(batch, dil, nblk),
        in_specs=in_specs,
        out_specs=out_specs,
        out_shape=out_shape,
        scratch_shapes=[pltpu.VMEM((tq + 2 * A_HALF, A_W), BF16),
                        pltpu.VMEM((tq + 2 * A_HALF, A_W), BF16)],
        compiler_params=_params("parallel", "parallel", "parallel"),
        name=f"dilated_d{dil}",
    )(*args)
    if prev is None:
        return out[0].reshape(n, A_W), out[1].reshape(n, LANES)
    return out.reshape(n, A_W)


def _dilated(pr, batch, seq, biases):
    o0, l0 = _dilated_pattern(pr, batch, seq, A_PATTERNS[0][1], biases[0], None)
    o1, l1 = _dilated_pattern(pr, batch, seq, A_PATTERNS[1][1], biases[1], None)
    return _dilated_pattern(pr, batch, seq, A_PATTERNS[2][1], biases[2], (o0, l0, o1, l1))


def _na_kernel(q_ref, k0_ref, k1_ref, k2_ref, v0_ref, v1_ref, v2_ref, bias_ref, o_ref):
    tq = q_ref.shape[1]
    lo, hi = _half_masks(tq)
    zero = jnp.zeros((), BF16)
    for p in range(B_HEADS // 2):
        cols = slice(p * LANES, (p + 1) * LANES)
        qp = q_ref[0, :, cols]
        kk = jnp.concatenate([k0_ref[0, :, cols], k1_ref[0, :, cols], k2_ref[0, :, cols]], axis=0)
        vv = jnp.concatenate([v0_ref[0, :, cols], v1_ref[0, :, cols], v2_ref[0, :, cols]], axis=0)
        halves = []
        for hf in range(2):
            qm = jnp.where(lo if hf == 0 else hi, qp, zero)
            s = _dot_nt(qm, kk) + bias_ref[0, 2 * p + hf]
            o, _, _ = _softmax_pv(s, vv)
            halves.append(o)
        o_ref[0, :, cols] = jnp.where(lo, halves[0], halves[1]).astype(o_ref.dtype)


def _natten(pr, batch, seq, bias):
    n = batch * seq
    tq = NA_QROWS * GRID_W
    nblk = seq // tq
    nkb = NA_KROWS // NA_QROWS
    ncol = pr.shape[1] // A_W
    prv = pr.reshape(batch, seq, pr.shape[1])

    def kv_spec(c, j):
        return pl.BlockSpec((1, tq, A_W),
                            lambda b, i: (b, jnp.clip(i - 1, 0, nblk - nkb) + j, c))

    def variant(i):
        return jnp.where(i == 0, 0, jnp.where(i == nblk - 1, 2, 1))

    in_specs = ([pl.BlockSpec((1, tq, A_W), lambda b, i: (b, i, 3))]
                + [kv_spec(4, j) for j in range(nkb)] + [kv_spec(5, j) for j in range(nkb)]
                + [pl.BlockSpec((1, B_HEADS, tq, nkb * tq), lambda b, i: (variant(i), 0, 0, 0))])
    out = pl.pallas_call(
        _na_kernel,
        grid=(batch, nblk),
        in_specs=in_specs,
        out_specs=pl.BlockSpec((1, tq, A_W), lambda b, i: (b, i, 0)),
        out_shape=jax.ShapeDtypeStruct((batch, seq, A_W), BF16),
        compiler_params=_params("parallel", "parallel"),
        name="natten",
    )(*([prv] * (1 + 2 * nkb) + [bias]))
    return out.reshape(n, A_W)


def _gqa_kernel(qt_ref, k_ref, vt_ref, o_ref, st_ref, e_ref, *, tk):
    tq = qt_ref.shape[2]
    nchunk = k_ref.shape[1] // tk
    heads = qt_ref.shape[1] // HEAD_DIM
    second = pl.program_id(1) % 2 == 1
    zeros = jnp.zeros((HEAD_DIM, tq), BF16)
    cols = []
    for h in range(heads):
        qh = qt_ref[0, h * HEAD_DIM:(h + 1) * HEAD_DIM, :]
        cols.append(jnp.where(second, jnp.concatenate([zeros, qh], axis=0),
                              jnp.concatenate([qh, zeros], axis=0)))
    qst = jnp.concatenate(cols, axis=1)
    width = heads * tq
    nchain = width // MXU_N
    qsts = [qst[:, j * MXU_N:(j + 1) * MXU_N] for j in range(nchain)]

    ones = jnp.ones((GQA_ONES_ROWS, tk), BF16)

    def chunk(c):
        return pl.ds(c * tk if isinstance(c, int) else pl.multiple_of(c * tk, tk), tk)

    def scores(c, slot):
        kc = k_ref[0, chunk(c), :]
        for j in range(nchain):
            st_ref[slot, j] = _dot(kc, qsts[j])

    def values(c, slot, acc, alpha):
        vc = jnp.concatenate([vt_ref[0, :, chunk(c)], ones], axis=0)
        return tuple(acc[j] * alpha[j] + _dot(vc, e_ref[slot, j]) for j in range(nchain))

    def softmax(slot, m):
        m_new, alpha = [], []
        for j in range(nchain):
            s = st_ref[slot, j]
            mn = jnp.maximum(m[j], jnp.max(s, axis=0, keepdims=True))
            e_ref[slot, j] = jnp.exp2((s - mn).astype(BF16))
            alpha.append(jnp.exp2(m[j] - mn))
            m_new.append(mn)
        return tuple(m_new), tuple(alpha)

    def step(c, carry, ahead=True):
        slot, m, acc, a2, a1 = carry[0] % 4, *carry[1:]
        before = max(c - 2, 0) if isinstance(c, int) else jnp.maximum(c - 2, 0)
        acc = values(before, (slot + 2) % 4, acc, a2)
        if ahead:
            scores(c + 2, (slot + 2) % 4)
        m, a0 = softmax(slot, m)
        return carry[0] + 1, m, acc, a1, a0

    def body(i, carry):
        state = (0,) + carry
        for u in range(4):
            state = step(4 * i + u, state)
        return state[1:]

    row = lambda v: tuple(jnp.full((1, MXU_N), v, F32) for _ in range(nchain))
    carry = (row(NEG), tuple(jnp.zeros((HEAD_DIM + GQA_ONES_ROWS, MXU_N), F32) for _ in range(nchain)),
             row(1.0), row(1.0))
    scores(0, 0)
    scores(1, 1)
    for slot in (2, 3):
        e_ref[slot] = jnp.zeros(e_ref.shape[1:], BF16)
    carry = lax.fori_loop(0, nchunk // 4 - 1, body, carry)
    state = (0,) + carry
    for u in range(4):
        state = step(nchunk - 4 + u, state, ahead=u < 2)
    _, m, acc, a2, a1 = state
    acc = values(nchunk - 2, 2, acc, a2)
    acc = values(nchunk - 1, 3, acc, a1)
    o = jnp.concatenate([a[:HEAD_DIM] / a[HEAD_DIM:HEAD_DIM + 1] for a in acc], axis=1)
    ot = jnp.concatenate([o[:, h * tq:(h + 1) * tq] for h in range(heads)], axis=0)
    o_ref[0] = ot.T.astype(o_ref.dtype)


def _gqa(qt, k, vt, batch, seq, tq=128, tk=512):
    n = batch * seq
    gw = (C_Q_HEADS // C_KV_HEADS) * HEAD_DIM
    nchain = (C_Q_HEADS // C_KV_HEADS) * tq // MXU_N
    assert seq % (4 * tk) == 0
    out = pl.pallas_call(
        functools.partial(_gqa_kernel, tk=tk),
        grid=(batch, C_KV_HEADS, seq // tq),
        in_specs=[pl.BlockSpec((1, gw, tq), lambda b, g, i: (b, g, i)),
                  pl.BlockSpec((1, seq, LANES), lambda b, g, i: (b, 0, g // 2)),
                  pl.BlockSpec((1, HEAD_DIM, seq), lambda b, g, i: (b, g, 0))],
        out_specs=pl.BlockSpec((1, tq, gw), lambda b, g, i: (b, i, g)),
        out_shape=jax.ShapeDtypeStruct((batch, seq, C_Q_HEADS * HEAD_DIM), BF16),
        scratch_shapes=[pltpu.VMEM((4, nchain, tk, MXU_N), F32),
                        pltpu.VMEM((4, nchain, tk, MXU_N), BF16)],
        compiler_params=_params("parallel", "parallel", "parallel"),
        name="gqa",
    )(qt, k.reshape(batch, seq, -1), vt)
    return out.reshape(n, C_Q_HEADS * HEAD_DIM)


def _post_kernel(*refs, route):
    if route:
        (x_ref, a1_ref, a2_ref, wo1_ref, wo2_ref, gxa_ref, wq_ref, kv_ref, wox_ref, gffn_ref,
         wr_ref, x2_ref, hn_ref, info_ref) = refs
    else:
        (x_ref, a1_ref, a2_ref, wo1_ref, wo2_ref, gxa_ref, wq_ref, kv_ref, wox_ref, gffn_ref,
         x2_ref, hn_ref) = refs
    x1 = x_ref[...] + _dot(a1_ref[...], wo1_ref[...]) + _dot(a2_ref[...], wo2_ref[...])
    tm = x1.shape[0]
    h = _rms(x1, gxa_ref[...]).astype(BF16)
    q = _dot(h, wq_ref[...]).astype(BF16)
    lo, hi = _half_masks(tm)
    zero = jnp.zeros((), BF16)
    outs = []
    for p in range(X_HEADS // 2):
        qp = q[:, p * LANES:(p + 1) * LANES]
        kk = kv_ref[0, :, p * LANES:(p + 1) * LANES]
        vv = kv_ref[0, :, X_W + p * LANES:X_W + (p + 1) * LANES]
        halves = []
        for hf in range(2):
            qm = jnp.where(lo if hf == 0 else hi, qp, zero)
            o, _, _ = _softmax_pv(_dot_nt(qm, kk), vv)
            halves.append(o)
        outs.append(jnp.where(lo, halves[0], halves[1]).astype(BF16))
    x2 = x1 + _dot(jnp.concatenate(outs, axis=1), wox_ref[...])
    x2_ref[...] = x2
    hf32 = _rms(x2, gffn_ref[...])
    hn_ref[...] = hf32.astype(hn_ref.dtype)
    if route:
        logits = jnp.dot(hf32, wr_ref[...], preferred_element_type=F32,
                         precision=lax.Precision.HIGHEST)
        lane = lax.broadcasted_iota(jnp.int32, logits.shape, 1).astype(F32)
        lg = jnp.where(lane < N_EXPERTS, logits, NEG)
        m1 = jnp.max(lg, axis=1, keepdims=True)
        i1 = jnp.min(jnp.where(lg == m1, lane, float(LANES)), axis=1, keepdims=True)
        lg2 = jnp.where(lane == i1, NEG, lg)
        m2 = jnp.max(lg2, axis=1, keepdims=True)
        i2 = jnp.min(jnp.where(lg2 == m2, lane, float(LANES)), axis=1, keepdims=True)
        e = jnp.exp(m2 - m1)
        den = 1.0 + e
        info_ref[...] = (jnp.where(lane == 0.0, i1, 0.0) + jnp.where(lane == 1.0, i2, 0.0)
                         + jnp.where(lane == 2.0, 1.0 / den, 0.0) + jnp.where(lane == 3.0, e / den, 0.0))


def _post(x, a1, a2, c1, c2, wo1, wo2, gxa, wq, kvm, wox, gffn, wr, seq, tm):
    n, d = x.shape
    spb = seq // tm
    row = lambda i: (i, 0)
    const = lambda i: (0, 0)
    in_specs = [pl.BlockSpec((tm, d), row),
                pl.BlockSpec((tm, A_W), lambda i: (i, c1)),
                pl.BlockSpec((tm, A_W), lambda i: (i, c2)),
                pl.BlockSpec((A_W, d), const),
                pl.BlockSpec((A_W, d), const),
                pl.BlockSpec((1, d), const),
                pl.BlockSpec((d, X_W), const),
                pl.BlockSpec((1, N_MEM, 2 * X_W), lambda i: (i // spb, 0, 0)),
                pl.BlockSpec((X_W, d), const),
                pl.BlockSpec((1, d), const)]
    args = [x, a1, a2, wo1, wo2, gxa.reshape(1, d), wq, kvm, wox, gffn.reshape(1, d)]
    out_specs = [pl.BlockSpec((tm, d), row), pl.BlockSpec((tm, d), row)]
    out_shape = [jax.ShapeDtypeStruct((n, d), F32),
                 jax.ShapeDtypeStruct((n, d), BF16 if wr is None else F32)]
    if wr is not None:
        in_specs.append(pl.BlockSpec((d, LANES), const))
        args.append(wr)
        out_specs.append(pl.BlockSpec((tm, LANES), row))
        out_shape.append(jax.ShapeDtypeStruct((n, LANES), F32))
    return pl.pallas_call(
        functools.partial(_post_kernel, route=wr is not None),
        grid=(n // tm,),
        in_specs=in_specs,
        out_specs=out_specs,
        out_shape=out_shape,
        compiler_params=_params("parallel"),
        name="post_route" if wr is not None else "post",
    )(*args)


def _swiglu_chunk(h, wg, wu, wd):
    g = _dot(h, wg)
    u = _dot(h, wu)
    a = (g / (1.0 + jnp.exp(-g))) * u
    return _dot(a.astype(BF16), wd)


def _ffn_kernel(x_ref, h_ref, wg_ref, wu_ref, wd_ref, o_ref):
    f = pl.program_id(1)
    y = _swiglu_chunk(h_ref[...], wg_ref[...], wu_ref[...], wd_ref[...])

    @pl.when(f == 0)
    def _():
        o_ref[...] = x_ref[...] + y

    @pl.when(f > 0)
    def _():
        o_ref[...] += y


def _ffn(x, hn, wg, wu, wd, tm, tf):
    n, d = x.shape
    ff = wg.shape[1]
    return pl.pallas_call(
        _ffn_kernel,
        grid=(n // tm, ff // tf),
        in_specs=[pl.BlockSpec((tm, d), lambda i, f: (i, 0)),
                  pl.BlockSpec((tm, d), lambda i, f: (i, 0)),
                  pl.BlockSpec((d, tf), lambda i, f: (0, f)),
                  pl.BlockSpec((d, tf), lambda i, f: (0, f)),
                  pl.BlockSpec((tf, d), lambda i, f: (f, 0))],
        out_specs=pl.BlockSpec((tm, d), lambda i, f: (i, 0)),
        out_shape=jax.ShapeDtypeStruct((n, d), F32),
        compiler_params=_params("parallel", "arbitrary"),
        name="ffn",
    )(x, hn, wg, wu, wd)


def _route(info, tmg):
    n = info.shape[0]
    ea = info[:, 0:2].astype(jnp.int32).reshape(-1)
    onehot = (ea[:, None] == jnp.arange(N_EXPERTS, dtype=jnp.int32)[None, :]).astype(jnp.int32)
    csum = jnp.cumsum(onehot, axis=0)
    counts = csum[-1]
    padded = (counts + tmg - 1) // tmg * tmg
    ends = jnp.cumsum(padded)
    starts = ends - padded
    dest = jnp.sum(onehot * (csum - 1 + starts[None, :]), axis=1)
    r_max = 2 * n + N_EXPERTS * tmg
    token = jnp.arange(2 * n, dtype=jnp.int32) // 2
    src = jnp.zeros((r_max,), jnp.int32).at[dest].set(token, unique_indices=True)
    tile_start = jnp.arange(r_max // tmg, dtype=jnp.int32) * tmg
    tile_e = jnp.sum((tile_start[:, None] >= ends[None, :]).astype(jnp.int32), axis=1)
    tile_e = jnp.minimum(tile_e, N_EXPERTS - 1)
    tile_ok = (tile_start < ends[-1]).astype(jnp.int32)
    return src, dest, tile_e, tile_ok


def _row_copy(src_ref, row, dst_ref, r, sem):
    return pltpu.make_async_copy(src_ref.at[pl.ds(row, 1)], dst_ref.at[pl.ds(r, 1)], sem)


def _gather_kernel(idx_ref, src_ref, o_ref, sem):
    tg = o_ref.shape[0]

    def start(r, c):
        _row_copy(src_ref, idx_ref[0, 0, r], o_ref, r, sem).start()
        return c

    def wait(r, c):
        _row_copy(src_ref, idx_ref[0, 0, r], o_ref, r, sem).wait()
        return c

    lax.fori_loop(0, tg, start, 0, unroll=8)
    lax.fori_loop(0, tg, wait, 0, unroll=8)


def _gather_rows(src, idx, tg):
    n, d = src.shape
    r = idx.shape[0]
    return pl.pallas_call(
        _gather_kernel,
        grid=(r // tg,),
        in_specs=[pl.BlockSpec((1, 1, tg), lambda i: (i, 0, 0), memory_space=pltpu.SMEM),
                  pl.BlockSpec(memory_space=pl.ANY)],
        out_specs=pl.BlockSpec((tg, d), lambda i: (i, 0)),
        out_shape=jax.ShapeDtypeStruct((r, d), src.dtype),
        scratch_shapes=[pltpu.SemaphoreType.DMA(())],
        compiler_params=_params("arbitrary"),
        name="moe_gather",
    )(idx.reshape(r // tg, 1, tg), src)


def _expert_kernel(te_ref, ok_ref, x_ref, wg_ref, wu_ref, wd_ref, o_ref, hb_ref):
    t = pl.program_id(0)
    f = pl.program_id(1)
    ok = ok_ref[t] == 1

    @pl.when(jnp.logical_and(ok, f == 0))
    def _():
        hb_ref[...] = x_ref[...].astype(BF16)
        o_ref[...] = _swiglu_chunk(hb_ref[...], wg_ref[0], wu_ref[0], wd_ref[0])

    @pl.when(jnp.logical_and(ok, f > 0))
    def _():
        o_ref[...] += _swiglu_chunk(hb_ref[...], wg_ref[0], wu_ref[0], wd_ref[0])

    @pl.when(jnp.logical_and(jnp.logical_not(ok), f == 0))
    def _():
        o_ref[...] = jnp.zeros_like(o_ref)


def _expert_ffn(xs, tile_e, tile_ok, wg, wu, wd, tmg, tf):
    r, d = xs.shape
    ff = wg.shape[2]
    grid_spec = pltpu.PrefetchScalarGridSpec(
        num_scalar_prefetch=2,
        grid=(r // tmg, ff // tf),
        in_specs=[pl.BlockSpec((tmg, d), lambda t, f, te, ok: (t, 0)),
                  pl.BlockSpec((1, d, tf), lambda t, f, te, ok: (te[t], 0, f * ok[t])),
                  pl.BlockSpec((1, d, tf), lambda t, f, te, ok: (te[t], 0, f * ok[t])),
                  pl.BlockSpec((1, tf, d), lambda t, f, te, ok: (te[t], f * ok[t], 0))],
        out_specs=pl.BlockSpec((tmg, d), lambda t, f, te, ok: (t, 0)),
        scratch_shapes=[pltpu.VMEM((tmg, d), BF16)],
    )
    return pl.pallas_call(
        _expert_kernel,
        grid_spec=grid_spec,
        out_shape=jax.ShapeDtypeStruct((r, d), F32),
        compiler_params=_params("arbitrary", "arbitrary"),
        name="moe_experts",
    )(tile_e, tile_ok, xs, wg, wu, wd)


def _combine_kernel(d_ref, x_ref, info_ref, ys_ref, o_ref, buf, sem):
    tc = o_ref.shape[0]

    def copies(r):
        return (_row_copy(ys_ref, d_ref[0, 0, 2 * r], buf.at[0], r, sem),
                _row_copy(ys_ref, d_ref[0, 0, 2 * r + 1], buf.at[1], r, sem))

    def start(r, c):
        for cp in copies(r):
            cp.start()
        return c

    def wait(r, c):
        for cp in copies(r):
            cp.wait()
        return c

    lax.fori_loop(0, tc, start, 0, unroll=4)
    lax.fori_loop(0, tc, wait, 0, unroll=4)
    g1 = info_ref[:, 2:3]
    g2 = info_ref[:, 3:4]
    o_ref[...] = x_ref[...] + (g1 * buf[0] + g2 * buf[1])


def _combine(x, info, dest, ys, tc):
    n, d = x.shape
    return pl.pallas_call(
        _combine_kernel,
        grid=(n // tc,),
        in_specs=[pl.BlockSpec((1, 1, 2 * tc), lambda i: (i, 0, 0), memory_space=pltpu.SMEM),
                  pl.BlockSpec((tc, d), lambda i: (i, 0)),
                  pl.BlockSpec((tc, LANES), lambda i: (i, 0)),
                  pl.BlockSpec(memory_space=pl.ANY)],
        out_specs=pl.BlockSpec((tc, d), lambda i: (i, 0)),
        out_shape=jax.ShapeDtypeStruct((n, d), F32),
        scratch_shapes=[pltpu.VMEM((2, tc, d), F32), pltpu.SemaphoreType.DMA(())],
        compiler_params=_params("arbitrary"),
        name="moe_combine",
    )(dest.reshape(n // tc, 1, 2 * tc), x, info, ys)


def _moe(x2, hf, info, wg, wu, wd, tmg, tf, tg, tc):
    src, dest, tile_e, tile_ok = _route(info, tmg)
    xs = _gather_rows(hf, src, tg)
    ys = _expert_ffn(xs, tile_e, tile_ok, wg, wu, wd, tmg, tf)
    return _combine(x2, info, dest, ys, tc)


def _final_norm_kernel(x_ref, g_ref, o_ref):
    o_ref[...] = _rms(x_ref[...], g_ref[...])


def _final_norm(x, g, tm):
    n, d = x.shape
    return pl.pallas_call(
        _final_norm_kernel,
        grid=(n // tm,),
        in_specs=[pl.BlockSpec((tm, d), lambda i: (i, 0)), pl.BlockSpec((1, d), lambda i: (0, 0))],
        out_specs=pl.BlockSpec((tm, d), lambda i: (i, 0)),
        out_shape=jax.ShapeDtypeStruct((n, d), F32),
        compiler_params=_params("parallel"),
        name="final_norm",
    )(x, g.reshape(1, d))


def _t5_bucket_np(rel):
    nb = N_BUCKETS // 2
    max_exact = nb // 2
    ret = np.where(rel > 0, nb, 0)
    n = np.abs(rel)
    large = max_exact + (np.log(np.maximum(n, 1).astype(np.float32) / max_exact)
                         / math.log(MAX_DISTANCE / max_exact) * (nb - max_exact)).astype(np.int32)
    large = np.minimum(large, nb - 1)
    return ret + np.where(n < max_exact, n, large)


def _dilated_bias(rel_table, dil):
    off = np.arange(2 * LANES)[None, :] - A_HALF - np.arange(LANES)[:, None]
    bucket = _t5_bucket_np(off * dil)
    bias = rel_table[jnp.asarray(bucket)].transpose(2, 0, 1).astype(F32)
    return jnp.where(jnp.asarray(np.abs(off) <= A_HALF)[None], bias, NEG)


def _natten_bias(rpb):
    rows = 4 * NA_KROWS
    c = np.arange(GRID_W)
    cs = np.clip(c - NA_KW // 2, 0, GRID_W - NA_KW)
    col_ok = (c[None, :] >= cs[:, None]) & (c[None, :] < cs[:, None] + NA_KW)
    dc = np.clip(c[None, :] - c[:, None], -(NA_KW - 1), NA_KW - 1) + NA_KW - 1
    col_sel = (dc[:, :, None] == np.arange(2 * NA_KW - 1)).astype(np.float32)
    row_sel, row_ok = [], []
    for r0 in (0, 2 * NA_QROWS, rows - NA_QROWS):
        ks = int(np.clip(r0 - NA_KH // 2, 0, rows - NA_KROWS))
        qr = r0 + np.arange(NA_QROWS)
        kr = ks + np.arange(NA_KROWS)
        rs = np.clip(qr - NA_KH // 2, 0, rows - NA_KH)
        row_ok.append((kr[None, :] >= rs[:, None]) & (kr[None, :] < rs[:, None] + NA_KH))
        dr = kr[None, :] - qr[:, None] + NA_KH - 1
        row_sel.append((dr[:, :, None] == np.arange(2 * NA_KH - 1)).astype(np.float32))
    row_sel = np.stack(row_sel)
    ok = np.stack(row_ok)[:, :, None, :, None] & col_ok[None, None, :, None, :]
    by_col = jnp.einsum('hab,xyb->haxy', rpb.astype(F32), col_sel, precision=lax.Precision.HIGHEST)
    table = jnp.einsum('vqka,haxy->vhqxky', row_sel, by_col, precision=lax.Precision.HIGHEST)
    table = jnp.where(jnp.asarray(ok)[:, None], table, NEG)
    tq = NA_QROWS * GRID_W
    return table.reshape(len(row_ok), rpb.shape[0], tq, NA_KROWS * GRID_W)


def _rope_tables(seq):
    t = jnp.arange(seq)
    half = HEAD_DIM // 2
    freqs = ROPE_THETA ** (-jnp.arange(0, half, 2, dtype=F32) / half)
    ang_r = (t // GRID_W).astype(F32)[:, None] * freqs[None, :]
    ang_c = (t % GRID_W).astype(F32)[:, None] * freqs[None, :]
    cos_h = jnp.concatenate([jnp.cos(ang_r)] * 2 + [jnp.cos(ang_c)] * 2, axis=1)
    sin_h = jnp.concatenate([-jnp.sin(ang_r), jnp.sin(ang_r), -jnp.sin(ang_c), jnp.sin(ang_c)], axis=1)
    return jnp.concatenate([cos_h, cos_h], axis=1), jnp.concatenate([sin_h, sin_h], axis=1)


def _prepare(p):
    w = {}
    ev_in = p['ev_w_in']
    scale = np.ones((ev_in.shape[-1],), np.float32)
    scale[0:A_W] = Q_SCALE
    scale[3 * A_W:4 * A_W] = Q_SCALE
    w['ev_w_in'] = (ev_in * scale).astype(BF16)
    w['ev_w_out'] = p['ev_w_out'].astype(BF16)
    w['ev_w_gate'] = p['ev_w_gate'].astype(BF16)
    w['ev_w_up'] = p['ev_w_up'].astype(BF16)
    w['ev_w_down'] = p['ev_w_down'].astype(BF16)
    w['od_w_in'] = p['od_w_in'].astype(BF16)
    w['od_w_out'] = p['od_w_out'].astype(BF16)
    w['od_q_norm'] = jnp.concatenate([p['od_q_norm']] * 2, axis=-1)[:, None, :]
    w['od_k_norm'] = jnp.concatenate([p['od_k_norm']] * 2, axis=-1)[:, None, :]
    w['od_router'] = jnp.pad(p['od_router'], ((0, 0), (0, 0), (0, LANES - N_EXPERTS)))
    w['od_moe_gate'] = p['od_moe_gate'].astype(BF16)
    w['od_moe_up'] = p['od_moe_up'].astype(BF16)
    w['od_moe_down'] = p['od_moe_down'].astype(BF16)
    w['xa_w_q'] = (p['xa_w_q'] * Q_SCALE).astype(BF16)
    w['xa_w_kv'] = p['xa_w_kv'].astype(BF16)
    w['xa_w_out'] = p['xa_w_out'].astype(BF16)
    w['dil_bias'] = [_dilated_bias(p['rel_table'], dil) for _, dil in A_PATTERNS]
    w['na_bias'] = [_natten_bias(p['ev_na_rpb'][i]) for i in range(p['ev_na_rpb'].shape[0])]
    return w


def _trunk(x, mem, p, w):
    batch, seq, d = x.shape
    n = batch * seq
    xf = x.reshape(n, d)
    memf = mem.reshape(batch * N_MEM, d)
    cos, sin = _rope_tables(seq)
    depth = p['xa_norm'].shape[0]
    for layer in range(depth):
        i = layer // 2
        if layer % 2 == 0:
            pr = _proj(xf, p['ev_norm_mix'][i], w['ev_w_in'][i], tm=512)
            a1 = _dilated(pr, batch, seq, w['dil_bias'])
            a2 = _natten(pr, batch, seq, w['na_bias'][i])
            c1 = c2 = 0
            wo = w['ev_w_out'][i]
            wr = None
        else:
            q, kk, vv = _proj_odd(xf, p['od_norm_mix'][i], w['od_w_in'][i], w['od_q_norm'][i],
                                  w['od_k_norm'][i], cos, sin, seq, tm=512)
            a1 = a2 = _gqa(q, kk, vv, batch, seq)
            c1, c2 = 0, 1
            wo = w['od_w_out'][i]
            wr = w['od_router'][i]
        kvm = _proj(memf, p['xa_mem_norm'][layer], w['xa_w_kv'][layer], tm=512)
        kvm = kvm.reshape(batch, N_MEM, 2 * X_W)
        gffn = p['ev_norm_ffn'][i] if layer % 2 == 0 else p['od_norm_ffn'][i]
        outs = _post(xf, a1, a2, c1, c2, wo[:A_W], wo[A_W:], p['xa_norm'][layer], w['xa_w_q'][layer],
                     kvm, w['xa_w_out'][layer], gffn, wr, seq, tm=512)
        if layer % 2 == 0:
            x2, hn = outs
            xf = _ffn(x2, hn, w['ev_w_gate'][i], w['ev_w_up'][i], w['ev_w_down'][i], tm=1024, tf=256)
        else:
            x2, hf, info = outs
            xf = _moe(x2, hf, info, w['od_moe_gate'][i], w['od_moe_up'][i], w['od_moe_down'][i],
                      tmg=1024, tf=512, tg=512, tc=256)
    return _final_norm(xf, p['final_norm'], tm=1024).reshape(batch, seq, d)


def kernel(x_prompt, x_sample, mem_prompt, mem_sample, rel_table, ev_norm_mix, ev_w_in, ev_na_rpb, ev_w_out, ev_norm_ffn, ev_w_gate, ev_w_up, ev_w_down, od_norm_mix, od_w_in, od_q_norm, od_k_norm, od_w_out, od_norm_ffn, od_router, od_moe_gate, od_moe_up, od_moe_down, xa_norm, xa_mem_norm, xa_w_q, xa_w_kv, xa_w_out, final_norm):
    p = dict(rel_table=rel_table,
             ev_norm_mix=ev_norm_mix, ev_w_in=ev_w_in, ev_na_rpb=ev_na_rpb, ev_w_out=ev_w_out,
             ev_norm_ffn=ev_norm_ffn, ev_w_gate=ev_w_gate, ev_w_up=ev_w_up, ev_w_down=ev_w_down,
             od_norm_mix=od_norm_mix, od_w_in=od_w_in, od_q_norm=od_q_norm, od_k_norm=od_k_norm,
             od_w_out=od_w_out, od_norm_ffn=od_norm_ffn, od_router=od_router,
             od_moe_gate=od_moe_gate, od_moe_up=od_moe_up, od_moe_down=od_moe_down,
             xa_norm=xa_norm, xa_mem_norm=xa_mem_norm, xa_w_q=xa_w_q, xa_w_kv=xa_w_kv,
             xa_w_out=xa_w_out, final_norm=final_norm)
    w = _prepare(p)
    return (_trunk(x_prompt, mem_prompt, p, w), _trunk(x_sample, mem_sample, p, w))
```

```python
import functools
import math

import numpy as np
import jax
import jax.numpy as jnp
from jax import lax
from jax.experimental import pallas as pl
from jax.experimental.pallas import tpu as pltpu

F32 = jnp.float32
BF16 = jnp.bfloat16

D_MODEL = 1024
HEAD_DIM = 64
GRID_W = 64
N_MEM = 256
A_HEADS = 8
A_PATTERNS = ((128, 1), (512, 4), (2048, 16))
A_HALF = 64
B_HEADS = 8
NA_KH = 8
NA_KW = 16
C_Q_HEADS = 16
C_KV_HEADS = 4
ROPE_THETA = 10000.0
N_BUCKETS = 32
MAX_DISTANCE = 1024
X_HEADS = 4
N_EXPERTS = 8
EPS = 1e-6
NEG = -1e30
A_W = A_HEADS * HEAD_DIM
X_W = X_HEADS * HEAD_DIM
Q_SCALE = HEAD_DIM ** -0.5
QK_SCALE = Q_SCALE * math.log2(math.e)

LANES = 128
MXU_N = 256
GQA_ONES_ROWS = 16
VMEM_LIMIT = 48 * 1024 * 1024

NA_QROWS = 4
NA_KROWS = 12


def _params(*sem):
    return pltpu.CompilerParams(dimension_semantics=sem, vmem_limit_bytes=VMEM_LIMIT)


def _rms(x, g):
    ms = jnp.mean(x * x, axis=-1, keepdims=True)
    return x * lax.rsqrt(ms + EPS) * g


def _dot(a, b):
    return jnp.dot(a, b, preferred_element_type=F32)


def _dot_nt(a, b):
    return lax.dot_general(a, b, (((1,), (1,)), ((), ())), preferred_element_type=F32)


def _half_masks(rows):
    lane = lax.broadcasted_iota(jnp.int32, (rows, LANES), 1)
    lo = lane < HEAD_DIM
    return lo, jnp.logical_not(lo)


def _softmax_pv(s, v):
    m = jnp.max(s, axis=1, keepdims=True)
    e = jnp.exp(s - m)
    l = jnp.sum(e, axis=1, keepdims=True)
    o = _dot(e.astype(BF16), v) / l
    return o, m, l


def _proj_kernel(x_ref, g_ref, w_ref, o_ref):
    h = _rms(x_ref[...], g_ref[...]).astype(BF16)
    o_ref[...] = _dot(h, w_ref[...]).astype(o_ref.dtype)


def _proj(x, g, w, tm):
    n, d = x.shape
    nout = w.shape[1]
    return pl.pallas_call(
        _proj_kernel,
        grid=(n // tm,),
        in_specs=[pl.BlockSpec((tm, d), lambda i: (i, 0)),
                  pl.BlockSpec((1, d), lambda i: (0, 0)),
                  pl.BlockSpec((d, nout), lambda i: (0, 0))],
        out_specs=pl.BlockSpec((tm, nout), lambda i: (i, 0)),
        out_shape=jax.ShapeDtypeStruct((n, nout), BF16),
        compiler_params=_params("parallel"),
        name="proj",
    )(x, g.reshape(1, d), w)


def _proj_even_kernel(x_ref, g_ref, w_ref, o_ref, *rest):
    views, y_ref = rest[:-1], rest[-1]
    h = _rms(x_ref[...], g_ref[...]).astype(BF16)
    y = _dot(h, w_ref[...])
    o_ref[...] = y.astype(o_ref.dtype)
    tm = y.shape[0]
    ncol = y_ref.shape[0]
    for c in range(ncol):
        y_ref[c] = y[:, c * LANES:(c + 1) * LANES]
    for v_ref in views:
        dil = v_ref.shape[1]
        for r in range(dil):
            for c in range(ncol):
                rows = y_ref[c, pl.ds(r, tm // dil, stride=dil), :]
                v_ref[0, r, :, c * LANES:(c + 1) * LANES] = rows.astype(v_ref.dtype)


def _proj_even(x, g, w, seq, tm):
    n, d = x.shape
    nout = w.shape[1]
    spb = seq // tm
    batch = n // seq
    dils = [dil for _, dil in A_PATTERNS if dil > 1]
    aw = 3 * A_W
    return pl.pallas_call(
        _proj_even_kernel,
        grid=(n // tm,),
        in_specs=[pl.BlockSpec((tm, d), lambda i: (i, 0)),
                  pl.BlockSpec((1, d), lambda i: (0, 0)),
                  pl.BlockSpec((d, nout), lambda i: (0, 0))],
        out_specs=[pl.BlockSpec((tm, nout), lambda i: (i, 0))]
        + [pl.BlockSpec((1, dil, tm // dil, aw), lambda i: (i // spb, 0, i % spb, 0)) for dil in dils],
        out_shape=[jax.ShapeDtypeStruct((n, nout), BF16)]
        + [jax.ShapeDtypeStruct((batch, dil, seq // dil, aw), BF16) for dil in dils],
        scratch_shapes=[pltpu.VMEM((aw // LANES, tm, LANES), F32)],
        compiler_params=_params("parallel"),
        name="proj_even",
    )(x, g.reshape(1, d), w)


def _proj_odd_kernel(x_ref, g_ref, w_ref, qg_ref, kg_ref, cos_ref, sin_ref, q_ref, k_ref, v_ref):
    h = _rms(x_ref[...], g_ref[...]).astype(BF16)
    y = _dot(h, w_ref[...])
    tm = y.shape[0]
    cos = cos_ref[...]
    sin = sin_ref[...]
    lane = lax.broadcasted_iota(jnp.int32, (tm, LANES), 1)
    lo = lane < HEAD_DIM
    first = (lane & 31) < 16

    def norm_rope(c, gain, scale):
        ss = c * c
        s_lo = jnp.sum(jnp.where(lo, ss, 0.0), axis=1, keepdims=True)
        s_hi = jnp.sum(jnp.where(lo, 0.0, ss), axis=1, keepdims=True)
        ms = jnp.where(lo, s_lo, s_hi) * (1.0 / HEAD_DIM)
        c = c * lax.rsqrt(ms + EPS) * gain
        partner = jnp.where(first, pltpu.roll(c, LANES - 16, 1), pltpu.roll(c, 16, 1))
        return (c * cos + partner * sin) * scale

    nq = q_ref.shape[1] // LANES
    nk = k_ref.shape[1] // LANES
    for j in range(nq):
        c = y[:, j * LANES:(j + 1) * LANES]
        q_ref[0, j * LANES:(j + 1) * LANES, :] = norm_rope(c, qg_ref[...], QK_SCALE).T.astype(BF16)
    for j in range(nk):
        c = y[:, (nq + j) * LANES:(nq + j + 1) * LANES]
        k_ref[:, j * LANES:(j + 1) * LANES] = norm_rope(c, kg_ref[...], 1.0).astype(BF16)
    for j in range(nk):
        c = y[:, (nq + nk + j) * LANES:(nq + nk + j + 1) * LANES]
        v_ref[0, j * LANES:(j + 1) * LANES, :] = c.T.astype(BF16)


def _proj_odd(x, g, w, qg, kg, cos, sin, seq, tm):
    n, d = x.shape
    nq = C_Q_HEADS * HEAD_DIM
    nk = C_KV_HEADS * HEAD_DIM
    spb = seq // tm
    batch = n // seq
    return pl.pallas_call(
        _proj_odd_kernel,
        grid=(n // tm,),
        in_specs=[pl.BlockSpec((tm, d), lambda i: (i, 0)),
                  pl.BlockSpec((1, d), lambda i: (0, 0)),
                  pl.BlockSpec((d, nq + 2 * nk), lambda i: (0, 0)),
                  pl.BlockSpec((1, LANES), lambda i: (0, 0)),
                  pl.BlockSpec((1, LANES), lambda i: (0, 0)),
                  pl.BlockSpec((tm, LANES), lambda i: (i % spb, 0)),
                  pl.BlockSpec((tm, LANES), lambda i: (i % spb, 0))],
        out_specs=[pl.BlockSpec((1, nq, tm), lambda i: (i // spb, 0, i % spb)),
                   pl.BlockSpec((tm, nk), lambda i: (i, 0)),
                   pl.BlockSpec((1, nk, tm), lambda i: (i // spb, 0, i % spb))],
        out_shape=[jax.ShapeDtypeStruct((batch, nq, seq), BF16),
                   jax.ShapeDtypeStruct((n, nk), BF16),
                   jax.ShapeDtypeStruct((batch, nk, seq), BF16)],
        compiler_params=_params("parallel"),
        name="proj_odd",
    )(x, g.reshape(1, d), w, qg, kg, cos, sin)


def _dil_kernel(*refs, nsub, mix):
    if mix:
        (q_ref, kp_ref, kc_ref, kn_ref, vp_ref, vc_ref, vn_ref, bias_ref,
         o0_ref, l0_ref, o1_ref, l1_ref, o_ref, kwin, vwin) = refs
    else:
        (q_ref, kp_ref, kc_ref, kn_ref, vp_ref, vc_ref, vn_ref, bias_ref,
         o_ref, lse_ref, kwin, vwin) = refs
    tq = q_ref.shape[2]
    i = pl.program_id(2)
    nblk = pl.num_programs(2)
    kwin[0:A_HALF] = kp_ref[0, 0]
    kwin[A_HALF:A_HALF + tq] = kc_ref[0, 0]
    kwin[A_HALF + tq:] = kn_ref[0, 0]
    vwin[0:A_HALF] = vp_ref[0, 0]
    vwin[A_HALF:A_HALF + tq] = vc_ref[0, 0]
    vwin[A_HALF + tq:] = vn_ref[0, 0]
    lo, hi = _half_masks(LANES)
    lane = lax.broadcasted_iota(jnp.int32, (LANES, LANES), 1)
    kj = lax.broadcasted_iota(jnp.int32, (LANES, 2 * LANES), 1)
    zero = jnp.zeros((), BF16)
    for a in range(nsub):
        rows = slice(a * LANES, (a + 1) * LANES)
        valid = None
        if a == 0:
            valid = kj >= jnp.where(i == 0, A_HALF, 0)
        if a == nsub - 1:
            v2 = kj < jnp.where(i == nblk - 1, LANES + A_HALF, 2 * LANES)
            valid = v2 if valid is None else jnp.logical_and(valid, v2)
        lse_all = jnp.zeros((LANES, LANES), F32)
        for p in range(A_HEADS // 2):
            cols = slice(p * LANES, (p + 1) * LANES)
            qp = q_ref[0, 0, rows, cols]
            kk = kwin[a * LANES:(a + 2) * LANES, cols]
            vv = vwin[a * LANES:(a + 2) * LANES, cols]
            halves = []
            for hf in range(2):
                h = 2 * p + hf
                qm = jnp.where(lo if hf == 0 else hi, qp, zero)
                s = _dot_nt(qm, kk) + bias_ref[h]
                if valid is not None:
                    s = jnp.where(valid, s, NEG)
                o, m, l = _softmax_pv(s, vv)
                lse = m + jnp.log(l)
                if mix:
                    la = l0_ref[0, rows, 16 * h:16 * h + 1]
                    lb = l1_ref[0, rows, 16 * h:16 * h + 1]
                    mx = jnp.maximum(jnp.maximum(la, lb), lse)
                    wa = jnp.exp(la - mx)
                    wb = jnp.exp(lb - mx)
                    wc = jnp.exp(lse - mx)
                    den = wa + wb + wc
                    o = ((wa / den) * o0_ref[0, rows, cols].astype(F32)
                         + (wb / den) * o1_ref[0, rows, cols].astype(F32)
                         + (wc / den) * o)
                else:
                    lse_all = jnp.where(lane // 16 == h, lse, lse_all)
                halves.append(o)
            o_ref[0, rows, cols] = jnp.where(lo, halves[0], halves[1]).astype(o_ref.dtype)
        if not mix:
            lse_ref[0, rows, :] = lse_all


def _dilated_pattern(qkv, batch, seq, dil, bias, prev):
    n = batch * seq
    sub = seq // dil
    tq = min(sub, 4 * LANES)
    nsub = tq // LANES
    nblk = sub // tq
    hb = tq // A_HALF
    nhalo = sub // A_HALF

    def main(c):
        return pl.BlockSpec((1, 1, tq, A_W), lambda b, r, i: (b, r, i, c))

    def before(c):
        return pl.BlockSpec((1, 1, A_HALF, A_W),
                            lambda b, r, i: (b, r, jnp.maximum(i * hb - 1, 0), c))

    def after(c):
        return pl.BlockSpec((1, 1, A_HALF, A_W),
                            lambda b, r, i: (b, r, jnp.minimum((i + 1) * hb, nhalo - 1), c))

    o_spec = pl.BlockSpec((1, tq, A_W), lambda b, r, i: (b, i, r))
    l_spec = pl.BlockSpec((1, tq, LANES), lambda b, r, i: (b, i, r))
    in_specs = [main(0), before(1), main(1), after(1), before(2), main(2), after(2),
                pl.BlockSpec((A_HEADS, LANES, 2 * LANES), lambda b, r, i: (0, 0, 0))]
    args = [qkv] * 7 + [bias]
    o_shape = jax.ShapeDtypeStruct((batch, sub, dil * A_W), BF16)
    l_shape = jax.ShapeDtypeStruct((batch, sub, dil * LANES), F32)
    if prev is None:
        out_specs, out_shape = [o_spec, l_spec], [o_shape, l_shape]
    else:
        o0, l0, o1, l1 = prev
        in_specs += [o_spec, l_spec, o_spec, l_spec]
        args += [o0.reshape(o_shape.shape), l0.reshape(l_shape.shape),
                 o1.reshape(o_shape.shape), l1.reshape(l_shape.shape)]
        out_specs, out_shape = o_spec, o_shape
    out = pl.pallas_call(
        functools.partial(_dil_kernel, nsub=nsub, mix=prev is not None),
        grid=(batch, dil, nblk),
        in_specs=in_specs,
        out_specs=out_specs,
        out_shape=out_shape,
        scratch_shapes=[pltpu.VMEM((tq + 2 * A_HALF, A_W), BF16),
                        pltpu.VMEM((tq + 2 * A_HALF, A_W), BF16)],
        compiler_params=_params("parallel", "parallel", "parallel"),
        name=f"dilated_d{dil}",
    )(*args)
    if prev is None:
        return out[0].reshape(n, A_W), out[1].reshape(n, LANES)
    return out.reshape(n, A_W)


def _dilated(views, batch, seq, biases):
    o0, l0 = _dilated_pattern(views[0], batch, seq, A_PATTERNS[0][1], biases[0], None)
    o1, l1 = _dilated_pattern(views[1], batch, seq, A_PATTERNS[1][1], biases[1], None)
    return _dilated_pattern(views[2], batch, seq, A_PATTERNS[2][1], biases[2], (o0, l0, o1, l1))


def _na_kernel(q_ref, k0_ref, k1_ref, k2_ref, v0_ref, v1_ref, v2_ref, bias_ref, o_ref):
    tq = q_ref.shape[1]
    lo, hi = _half_masks(tq)
    zero = jnp.zeros((), BF16)
    for p in range(B_HEADS // 2):
        cols = slice(p * LANES, (p + 1) * LANES)
        qp = q_ref[0, :, cols]
        kk = jnp.concatenate([k0_ref[0, :, cols], k1_ref[0, :, cols], k2_ref[0, :, cols]], axis=0)
        vv = jnp.concatenate([v0_ref[0, :, cols], v1_ref[0, :, cols], v2_ref[0, :, cols]], axis=0)
        halves = []
        for hf in range(2):
            qm = jnp.where(lo if hf == 0 else hi, qp, zero)
            s = _dot_nt(qm, kk) + bias_ref[0, 2 * p + hf]
            o, _, _ = _softmax_pv(s, vv)
            halves.append(o)
        o_ref[0, :, cols] = jnp.where(lo, halves[0], halves[1]).astype(o_ref.dtype)


def _natten(pr, batch, seq, bias):
    n = batch * seq
    tq = NA_QROWS * GRID_W
    nblk = seq // tq
    nkb = NA_KROWS // NA_QROWS
    ncol = pr.shape[1] // A_W
    prv = pr.reshape(batch, seq, pr.shape[1])

    def kv_spec(c, j):
        return pl.BlockSpec((1, tq, A_W),
                            lambda b, i: (b, jnp.clip(i - 1, 0, nblk - nkb) + j, c))

    def variant(i):
        return jnp.where(i == 0, 0, jnp.where(i == nblk - 1, 2, 1))

    in_specs = ([pl.BlockSpec((1, tq, A_W), lambda b, i: (b, i, 3))]
                + [kv_spec(4, j) for j in range(nkb)] + [kv_spec(5, j) for j in range(nkb)]
                + [pl.BlockSpec((1, B_HEADS, tq, nkb * tq), lambda b, i: (variant(i), 0, 0, 0))])
    out = pl.pallas_call(
        _na_kernel,
        grid=(batch, nblk),
        in_specs=in_specs,
        out_specs=pl.BlockSpec((1, tq, A_W), lambda b, i: (b, i, 0)),
        out_shape=jax.ShapeDtypeStruct((batch, seq, A_W), BF16),
        compiler_params=_params("parallel", "parallel"),
        name="natten",
    )(*([prv] * (1 + 2 * nkb) + [bias]))
    return out.reshape(n, A_W)


def _gqa_kernel(qt_ref, k_ref, vt_ref, o_ref, st_ref, e_ref, *, tk):
    tq = qt_ref.shape[2]
    nchunk = k_ref.shape[1] // tk
    heads = qt_ref.shape[1] // HEAD_DIM
    second = pl.program_id(1) % 2 == 1
    zeros = jnp.zeros((HEAD_DIM, tq), BF16)
    cols = []
    for h in range(heads):
        qh = qt_ref[0, h * HEAD_DIM:(h + 1) * HEAD_DIM, :]
        cols.append(jnp.where(second, jnp.concatenate([zeros, qh], axis=0),
                              jnp.concatenate([qh, zeros], axis=0)))
    qst = jnp.concatenate(cols, axis=1)
    width = heads * tq
    nchain = width // MXU_N
    qsts = [qst[:, j * MXU_N:(j + 1) * MXU_N] for j in range(nchain)]

    ones = jnp.ones((GQA_ONES_ROWS, tk), BF16)

    def chunk(c):
        return pl.ds(c * tk if isinstance(c, int) else pl.multiple_of(c * tk, tk), tk)

    def scores(c, slot):
        kc = k_ref[0, chunk(c), :]
        for j in range(nchain):
            st_ref[slot, j] = _dot(kc, qsts[j])

    def values(c, slot, acc, alpha):
        vc = jnp.concatenate([vt_ref[0, :, chunk(c)], ones], axis=0)
        return tuple(acc[j] * alpha[j] + _dot(vc, e_ref[slot, j]) for j in range(nchain))

    def softmax(slot, m):
        m_new, alpha = [], []
        for j in range(nchain):
            s = st_ref[slot, j]
            mn = jnp.maximum(m[j], jnp.max(s, axis=0, keepdims=True))
            e_ref[slot, j] = jnp.exp2((s - mn).astype(BF16))
            alpha.append(jnp.exp2(m[j] - mn))
            m_new.append(mn)
        return tuple(m_new), tuple(alpha)

    def step(c, carry, ahead=True):
        slot, m, acc, a2, a1 = carry[0] % 4, *carry[1:]
        before = max(c - 2, 0) if isinstance(c, int) else jnp.maximum(c - 2, 0)
        acc = values(before, (slot + 2) % 4, acc, a2)
        if ahead:
            scores(c + 2, (slot + 2) % 4)
        m, a0 = softmax(slot, m)
        return carry[0] + 1, m, acc, a1, a0

    def body(i, carry):
        state = (0,) + carry
        for u in range(4):
            state = step(4 * i + u, state)
        return state[1:]

    row = lambda v: tuple(jnp.full((1, MXU_N), v, F32) for _ in range(nchain))
    carry = (row(NEG), tuple(jnp.zeros((HEAD_DIM + GQA_ONES_ROWS, MXU_N), F32) for _ in range(nchain)),
             row(1.0), row(1.0))
    scores(0, 0)
    scores(1, 1)
    for slot in (2, 3):
        e_ref[slot] = jnp.zeros(e_ref.shape[1:], BF16)
    carry = lax.fori_loop(0, nchunk // 4 - 1, body, carry)
    state = (0,) + carry
    for u in range(4):
        state = step(nchunk - 4 + u, state, ahead=u < 2)
    _, m, acc, a2, a1 = state
    acc = values(nchunk - 2, 2, acc, a2)
    acc = values(nchunk - 1, 3, acc, a1)
    o = jnp.concatenate([a[:HEAD_DIM] / a[HEAD_DIM:HEAD_DIM + 1] for a in acc], axis=1)
    ot = jnp.concatenate([o[:, h * tq:(h + 1) * tq] for h in range(heads)], axis=0)
    o_ref[0] = ot.T.astype(o_ref.dtype)


def _gqa(qt, k, vt, batch, seq, tq=256, tk=512):
    n = batch * seq
    gw = (C_Q_HEADS // C_KV_HEADS) * HEAD_DIM
    nchain = (C_Q_HEADS // C_KV_HEADS) * tq // MXU_N
    assert seq % (4 * tk) == 0
    out = pl.pallas_call(
        functools.partial(_gqa_kernel, tk=tk),
        grid=(batch, C_KV_HEADS, seq // tq),
        in_specs=[pl.BlockSpec((1, gw, tq), lambda b, g, i: (b, g, i)),
                  pl.BlockSpec((1, seq, LANES), lambda b, g, i: (b, 0, g // 2)),
                  pl.BlockSpec((1, HEAD_DIM, seq), lambda b, g, i: (b, g, 0))],
        out_specs=pl.BlockSpec((1, tq, gw), lambda b, g, i: (b, i, g)),
        out_shape=jax.ShapeDtypeStruct((batch, seq, C_Q_HEADS * HEAD_DIM), BF16),
        scratch_shapes=[pltpu.VMEM((4, nchain, tk, MXU_N), F32),
                        pltpu.VMEM((4, nchain, tk, MXU_N), BF16)],
        compiler_params=_params("parallel", "parallel", "parallel"),
        name="gqa",
    )(qt, k.reshape(batch, seq, -1), vt)
    return out.reshape(n, C_Q_HEADS * HEAD_DIM)


def _post_kernel(*refs, route):
    if route:
        (x_ref, a1_ref, a2_ref, wo1_ref, wo2_ref, gxa_ref, wq_ref, kv_ref, wox_ref, gffn_ref,
         wr_ref, x2_ref, hn_ref, info_ref) = refs
    else:
        (x_ref, a1_ref, a2_ref, wo1_ref, wo2_ref, gxa_ref, wq_ref, kv_ref, wox_ref, gffn_ref,
         x2_ref, hn_ref) = refs
    x1 = x_ref[...] + _dot(a1_ref[...], wo1_ref[...]) + _dot(a2_ref[...], wo2_ref[...])
    tm = x1.shape[0]
    h = _rms(x1, gxa_ref[...]).astype(BF16)
    q = _dot(h, wq_ref[...]).astype(BF16)
    lo, hi = _half_masks(tm)
    zero = jnp.zeros((), BF16)
    outs = []
    for p in range(X_HEADS // 2):
        qp = q[:, p * LANES:(p + 1) * LANES]
        kk = kv_ref[0, :, p * LANES:(p + 1) * LANES]
        vv = kv_ref[0, :, X_W + p * LANES:X_W + (p + 1) * LANES]
        halves = []
        for hf in range(2):
            qm = jnp.where(lo if hf == 0 else hi, qp, zero)
            o, _, _ = _softmax_pv(_dot_nt(qm, kk), vv)
            halves.append(o)
        outs.append(jnp.where(lo, halves[0], halves[1]).astype(BF16))
    x2 = x1 + _dot(jnp.concatenate(outs, axis=1), wox_ref[...])
    x2_ref[...] = x2
    hf32 = _rms(x2, gffn_ref[...])
    hn_ref[...] = hf32.astype(hn_ref.dtype)
    if route:
        lane = lax.broadcasted_iota(jnp.int32, (tm, LANES), 1).astype(F32)
        lg = jnp.full((tm, LANES), NEG, F32)
        for ex in range(N_EXPERTS):
            le = jnp.sum(hf32 * wr_ref[ex:ex + 1, :], axis=1, keepdims=True)
            lg = jnp.where(lane == float(ex), le, lg)
        m1 = jnp.max(lg, axis=1, keepdims=True)
        i1 = jnp.min(jnp.where(lg == m1, lane, float(LANES)), axis=1, keepdims=True)
        lg2 = jnp.where(lane == i1, NEG, lg)
        m2 = jnp.max(lg2, axis=1, keepdims=True)
        i2 = jnp.min(jnp.where(lg2 == m2, lane, float(LANES)), axis=1, keepdims=True)
        e = jnp.exp(m2 - m1)
        den = 1.0 + e
        info_ref[...] = (jnp.where(lane == 0.0, i1, 0.0) + jnp.where(lane == 1.0, i2, 0.0)
                         + jnp.where(lane == 2.0, 1.0 / den, 0.0) + jnp.where(lane == 3.0, e / den, 0.0))


def _post(x, a1, a2, c1, c2, wo1, wo2, gxa, wq, kvm, wox, gffn, wr, seq, tm):
    n, d = x.shape
    spb = seq // tm
    row = lambda i: (i, 0)
    const = lambda i: (0, 0)
    in_specs = [pl.BlockSpec((tm, d), row),
                pl.BlockSpec((tm, A_W), lambda i: (i, c1)),
                pl.BlockSpec((tm, A_W), lambda i: (i, c2)),
                pl.BlockSpec((A_W, d), const),
                pl.BlockSpec((A_W, d), const),
                pl.BlockSpec((1, d), const),
                pl.BlockSpec((d, X_W), const),
                pl.BlockSpec((1, N_MEM, 2 * X_W), lambda i: (i // spb, 0, 0)),
                pl.BlockSpec((X_W, d), const),
                pl.BlockSpec((1, d), const)]
    args = [x, a1, a2, wo1, wo2, gxa.reshape(1, d), wq, kvm, wox, gffn.reshape(1, d)]
    out_specs = [pl.BlockSpec((tm, d), row), pl.BlockSpec((tm, d), row)]
    out_shape = [jax.ShapeDtypeStruct((n, d), F32),
                 jax.ShapeDtypeStruct((n, d), BF16 if wr is None else F32)]
    if wr is not None:
        in_specs.append(pl.BlockSpec((N_EXPERTS, d), const))
        args.append(wr)
        out_specs.append(pl.BlockSpec((tm, LANES), row))
        out_shape.append(jax.ShapeDtypeStruct((n, LANES), F32))
    return pl.pallas_call(
        functools.partial(_post_kernel, route=wr is not None),
        grid=(n // tm,),
        in_specs=in_specs,
        out_specs=out_specs,
        out_shape=out_shape,
        compiler_params=_params("parallel"),
        name="post_route" if wr is not None else "post",
    )(*args)


def _swiglu_chunk(h, wg, wu, wd):
    g = _dot(h, wg)
    u = _dot(h, wu)
    a = (g / (1.0 + jnp.exp(-g))) * u
    return _dot(a.astype(BF16), wd)


def _ffn_kernel(x_ref, h_ref, wg_ref, wu_ref, wd_ref, o_ref):
    f = pl.program_id(1)
    y = _swiglu_chunk(h_ref[...], wg_ref[...], wu_ref[...], wd_ref[...])

    @pl.when(f == 0)
    def _():
        o_ref[...] = x_ref[...] + y

    @pl.when(f > 0)
    def _():
        o_ref[...] += y


def _ffn(x, hn, wg, wu, wd, tm, tf):
    n, d = x.shape
    ff = wg.shape[1]
    return pl.pallas_call(
        _ffn_kernel,
        grid=(n // tm, ff // tf),
        in_specs=[pl.BlockSpec((tm, d), lambda i, f: (i, 0)),
                  pl.BlockSpec((tm, d), lambda i, f: (i, 0)),
                  pl.BlockSpec((d, tf), lambda i, f: (0, f)),
                  pl.BlockSpec((d, tf), lambda i, f: (0, f)),
                  pl.BlockSpec((tf, d), lambda i, f: (f, 0))],
        out_specs=pl.BlockSpec((tm, d), lambda i, f: (i, 0)),
        out_shape=jax.ShapeDtypeStruct((n, d), F32),
        compiler_params=_params("parallel", "arbitrary"),
        name="ffn",
    )(x, hn, wg, wu, wd)


def _route(info, tmg):
    n = info.shape[0]
    ea = info[:, 0:2].astype(jnp.int32).reshape(-1)
    onehot = (ea[:, None] == jnp.arange(N_EXPERTS, dtype=jnp.int32)[None, :]).astype(jnp.int32)
    csum = jnp.cumsum(onehot, axis=0)
    counts = csum[-1]
    padded = (counts + tmg - 1) // tmg * tmg
    ends = jnp.cumsum(padded)
    starts = ends - padded
    dest = jnp.sum(onehot * (csum - 1 + starts[None, :]), axis=1)
    r_max = 2 * n + N_EXPERTS * tmg
    token = jnp.arange(2 * n, dtype=jnp.int32) // 2
    src = jnp.zeros((r_max,), jnp.int32).at[dest].set(token, unique_indices=True)
    tile_start = jnp.arange(r_max // tmg, dtype=jnp.int32) * tmg
    tile_e = jnp.sum((tile_start[:, None] >= ends[None, :]).astype(jnp.int32), axis=1)
    tile_e = jnp.minimum(tile_e, N_EXPERTS - 1)
    tile_ok = (tile_start < ends[-1]).astype(jnp.int32)
    return src, dest, tile_e, tile_ok


def _row_copy(src_ref, row, dst_ref, r, sem):
    return pltpu.make_async_copy(src_ref.at[pl.ds(row, 1)], dst_ref.at[pl.ds(r, 1)], sem)


def _gather_kernel(idx_ref, src_ref, o_ref, sem):
    i = pl.program_id(0)
    last = pl.num_programs(0) - 1
    tg = idx_ref.shape[2]
    base = i * tg

    def start(r, c):
        _row_copy(src_ref, idx_ref[0, 0, r], o_ref, base + r, sem.at[i % 2]).start()
        return c

    def drain(parity):
        def wait(r, c):
            _row_copy(src_ref, 0, o_ref, 0, sem.at[parity]).wait()
            return c
        lax.fori_loop(0, tg, wait, 0, unroll=8)

    lax.fori_loop(0, tg, start, 0, unroll=8)

    @pl.when(i > 0)
    def _():
        drain((i + 1) % 2)

    @pl.when(i == last)
    def _():
        drain(i % 2)


def _gather_rows(src, idx, tg):
    n, d = src.shape
    r = idx.shape[0]
    return pl.pallas_call(
        _gather_kernel,
        grid=(r // tg,),
        in_specs=[pl.BlockSpec((1, 1, tg), lambda i: (i, 0, 0), memory_space=pltpu.SMEM),
                  pl.BlockSpec(memory_space=pl.ANY)],
        out_specs=pl.BlockSpec(memory_space=pl.ANY),
        out_shape=jax.ShapeDtypeStruct((r, d), src.dtype),
        scratch_shapes=[pltpu.SemaphoreType.DMA((2,))],
        compiler_params=_params("arbitrary"),
        name="moe_gather",
    )(idx.reshape(r // tg, 1, tg), src)


def _expert_kernel(te_ref, ok_ref, x_ref, wg_ref, wu_ref, wd_ref, o_ref, hb_ref):
    t = pl.program_id(0)
    f = pl.program_id(1)
    ok = ok_ref[t] == 1

    @pl.when(jnp.logical_and(ok, f == 0))
    def _():
        hb_ref[...] = x_ref[...].astype(BF16)
        o_ref[...] = _swiglu_chunk(hb_ref[...], wg_ref[0], wu_ref[0], wd_ref[0])

    @pl.when(jnp.logical_and(ok, f > 0))
    def _():
        o_ref[...] += _swiglu_chunk(hb_ref[...], wg_ref[0], wu_ref[0], wd_ref[0])

    @pl.when(jnp.logical_and(jnp.logical_not(ok), f == 0))
    def _():
        o_ref[...] = jnp.zeros_like(o_ref)


def _expert_ffn(xs, tile_e, tile_ok, wg, wu, wd, tmg, tf):
    r, d = xs.shape
    ff = wg.shape[2]
    grid_spec = pltpu.PrefetchScalarGridSpec(
        num_scalar_prefetch=2,
        grid=(r // tmg, ff // tf),
        in_specs=[pl.BlockSpec((tmg, d), lambda t, f, te, ok: (t, 0)),
                  pl.BlockSpec((1, d, tf), lambda t, f, te, ok: (te[t], 0, f * ok[t])),
                  pl.BlockSpec((1, d, tf), lambda t, f, te, ok: (te[t], 0, f * ok[t])),
                  pl.BlockSpec((1, tf, d), lambda t, f, te, ok: (te[t], f * ok[t], 0))],
        out_specs=pl.BlockSpec((tmg, d), lambda t, f, te, ok: (t, 0)),
        scratch_shapes=[pltpu.VMEM((tmg, d), BF16)],
    )
    return pl.pallas_call(
        _expert_kernel,
        grid_spec=grid_spec,
        out_shape=jax.ShapeDtypeStruct((r, d), F32),
        compiler_params=_params("arbitrary", "arbitrary"),
        name="moe_experts",
    )(tile_e, tile_ok, xs, wg, wu, wd)


def _combine_kernel(d_ref, x_ref, info_ref, ys_ref, o_ref, buf, sem):
    tc = o_ref.shape[0]

    def copies(r):
        return (_row_copy(ys_ref, d_ref[0, 0, 2 * r], buf.at[0], r, sem),
                _row_copy(ys_ref, d_ref[0, 0, 2 * r + 1], buf.at[1], r, sem))

    def start(r, c):
        for cp in copies(r):
            cp.start()
        return c

    def wait(r, c):
        for cp in copies(r):
            cp.wait()
        return c

    lax.fori_loop(0, tc, start, 0, unroll=4)
    lax.fori_loop(0, tc, wait, 0, unroll=4)
    g1 = info_ref[:, 2:3]
    g2 = info_ref[:, 3:4]
    o_ref[...] = x_ref[...] + (g1 * buf[0] + g2 * buf[1])


def _combine(x, info, dest, ys, tc):
    n, d = x.shape
    return pl.pallas_call(
        _combine_kernel,
        grid=(n // tc,),
        in_specs=[pl.BlockSpec((1, 1, 2 * tc), lambda i: (i, 0, 0), memory_space=pltpu.SMEM),
                  pl.BlockSpec((tc, d), lambda i: (i, 0)),
                  pl.BlockSpec((tc, LANES), lambda i: (i, 0)),
                  pl.BlockSpec(memory_space=pl.ANY)],
        out_specs=pl.BlockSpec((tc, d), lambda i: (i, 0)),
        out_shape=jax.ShapeDtypeStruct((n, d), F32),
        scratch_shapes=[pltpu.VMEM((2, tc, d), F32), pltpu.SemaphoreType.DMA(())],
        compiler_params=_params("arbitrary"),
        name="moe_combine",
    )(dest.reshape(n // tc, 1, 2 * tc), x, info, ys)


def _moe(x2, hf, info, wg, wu, wd, tmg, tf, tg, tc):
    src, dest, tile_e, tile_ok = _route(info, tmg)
    xs = _gather_rows(hf, src, tg)
    ys = _expert_ffn(xs, tile_e, tile_ok, wg, wu, wd, tmg, tf)
    return _combine(x2, info, dest, ys, tc)


def _final_norm_kernel(x_ref, g_ref, o_ref):
    o_ref[...] = _rms(x_ref[...], g_ref[...])


def _final_norm(x, g, tm):
    n, d = x.shape
    return pl.pallas_call(
        _final_norm_kernel,
        grid=(n // tm,),
        in_specs=[pl.BlockSpec((tm, d), lambda i: (i, 0)), pl.BlockSpec((1, d), lambda i: (0, 0))],
        out_specs=pl.BlockSpec((tm, d), lambda i: (i, 0)),
        out_shape=jax.ShapeDtypeStruct((n, d), F32),
        compiler_params=_params("parallel"),
        name="final_norm",
    )(x, g.reshape(1, d))


def _t5_bucket_np(rel):
    nb = N_BUCKETS // 2
    max_exact = nb // 2
    ret = np.where(rel > 0, nb, 0)
    n = np.abs(rel)
    large = max_exact + (np.log(np.maximum(n, 1).astype(np.float32) / max_exact)
                         / math.log(MAX_DISTANCE / max_exact) * (nb - max_exact)).astype(np.int32)
    large = np.minimum(large, nb - 1)
    return ret + np.where(n < max_exact, n, large)


def _dilated_bias(rel_table, dil):
    off = np.arange(2 * LANES)[None, :] - A_HALF - np.arange(LANES)[:, None]
    bucket = _t5_bucket_np(off * dil)
    bias = rel_table[jnp.asarray(bucket)].transpose(2, 0, 1).astype(F32)
    return jnp.where(jnp.asarray(np.abs(off) <= A_HALF)[None], bias, NEG)


def _natten_bias(rpb):
    rows = 4 * NA_KROWS
    c = np.arange(GRID_W)
    cs = np.clip(c - NA_KW // 2, 0, GRID_W - NA_KW)
    col_ok = (c[None, :] >= cs[:, None]) & (c[None, :] < cs[:, None] + NA_KW)
    dc = np.clip(c[None, :] - c[:, None], -(NA_KW - 1), NA_KW - 1) + NA_KW - 1
    col_sel = (dc[:, :, None] == np.arange(2 * NA_KW - 1)).astype(np.float32)
    row_sel, row_ok = [], []
    for r0 in (0, 2 * NA_QROWS, rows - NA_QROWS):
        ks = int(np.clip(r0 - NA_KH // 2, 0, rows - NA_KROWS))
        qr = r0 + np.arange(NA_QROWS)
        kr = ks + np.arange(NA_KROWS)
        rs = np.clip(qr - NA_KH // 2, 0, rows - NA_KH)
        row_ok.append((kr[None, :] >= rs[:, None]) & (kr[None, :] < rs[:, None] + NA_KH))
        dr = kr[None, :] - qr[:, None] + NA_KH - 1
        row_sel.append((dr[:, :, None] == np.arange(2 * NA_KH - 1)).astype(np.float32))
    row_sel = np.stack(row_sel)
    ok = np.stack(row_ok)[:, :, None, :, None] & col_ok[None, None, :, None, :]
    by_col = jnp.einsum('hab,xyb->haxy', rpb.astype(F32), col_sel, precision=lax.Precision.HIGHEST)
    table = jnp.einsum('vqka,haxy->vhqxky', row_sel, by_col, precision=lax.Precision.HIGHEST)
    table = jnp.where(jnp.asarray(ok)[:, None], table, NEG)
    tq = NA_QROWS * GRID_W
    return table.reshape(len(row_ok), rpb.shape[0], tq, NA_KROWS * GRID_W)


def _rope_tables(seq):
    t = jnp.arange(seq)
    half = HEAD_DIM // 2
    freqs = ROPE_THETA ** (-jnp.arange(0, half, 2, dtype=F32) / half)
    ang_r = (t // GRID_W).astype(F32)[:, None] * freqs[None, :]
    ang_c = (t % GRID_W).astype(F32)[:, None] * freqs[None, :]
    cos_h = jnp.concatenate([jnp.cos(ang_r)] * 2 + [jnp.cos(ang_c)] * 2, axis=1)
    sin_h = jnp.concatenate([-jnp.sin(ang_r), jnp.sin(ang_r), -jnp.sin(ang_c), jnp.sin(ang_c)], axis=1)
    return jnp.concatenate([cos_h, cos_h], axis=1), jnp.concatenate([sin_h, sin_h], axis=1)


def _prepare(p):
    w = {}
    ev_in = p['ev_w_in']
    scale = np.ones((ev_in.shape[-1],), np.float32)
    scale[0:A_W] = Q_SCALE
    scale[3 * A_W:4 * A_W] = Q_SCALE
    w['ev_w_in'] = (ev_in * scale).astype(BF16)
    w['ev_w_out'] = p['ev_w_out'].astype(BF16)
    w['ev_w_gate'] = p['ev_w_gate'].astype(BF16)
    w['ev_w_up'] = p['ev_w_up'].astype(BF16)
    w['ev_w_down'] = p['ev_w_down'].astype(BF16)
    w['od_w_in'] = p['od_w_in'].astype(BF16)
    w['od_w_out'] = p['od_w_out'].astype(BF16)
    w['od_q_norm'] = jnp.concatenate([p['od_q_norm']] * 2, axis=-1)[:, None, :]
    w['od_k_norm'] = jnp.concatenate([p['od_k_norm']] * 2, axis=-1)[:, None, :]
    w['od_router'] = jnp.swapaxes(p['od_router'], 1, 2)
    w['od_moe_gate'] = p['od_moe_gate'].astype(BF16)
    w['od_moe_up'] = p['od_moe_up'].astype(BF16)
    w['od_moe_down'] = p['od_moe_down'].astype(BF16)
    w['xa_w_q'] = (p['xa_w_q'] * Q_SCALE).astype(BF16)
    w['xa_w_kv'] = p['xa_w_kv'].astype(BF16)
    w['xa_w_out'] = p['xa_w_out'].astype(BF16)
    w['dil_bias'] = [_dilated_bias(p['rel_table'], dil) for _, dil in A_PATTERNS]
    w['na_bias'] = [_natten_bias(p['ev_na_rpb'][i]) for i in range(p['ev_na_rpb'].shape[0])]
    return w


def _trunk(x, mem, p, w):
    batch, seq, d = x.shape
    n = batch * seq
    xf = x.reshape(n, d)
    memf = mem.reshape(batch * N_MEM, d)
    cos, sin = _rope_tables(seq)
    depth = p['xa_norm'].shape[0]
    for layer in range(depth):
        i = layer // 2
        if layer % 2 == 0:
            pr, v4, v16 = _proj_even(xf, p['ev_norm_mix'][i], w['ev_w_in'][i], seq, tm=512)
            a1 = _dilated([pr.reshape(batch, 1, seq, -1), v4, v16], batch, seq, w['dil_bias'])
            a2 = _natten(pr, batch, seq, w['na_bias'][i])
            c1 = c2 = 0
            wo = w['ev_w_out'][i]
            wr = None
        else:
            q, kk, vv = _proj_odd(xf, p['od_norm_mix'][i], w['od_w_in'][i], w['od_q_norm'][i],
                                  w['od_k_norm'][i], cos, sin, seq, tm=512)
            a1 = a2 = _gqa(q, kk, vv, batch, seq)
            c1, c2 = 0, 1
            wo = w['od_w_out'][i]
            wr = w['od_router'][i]
        kvm = _proj(memf, p['xa_mem_norm'][layer], w['xa_w_kv'][layer], tm=512)
        kvm = kvm.reshape(batch, N_MEM, 2 * X_W)
        gffn = p['ev_norm_ffn'][i] if layer % 2 == 0 else p['od_norm_ffn'][i]
        outs = _post(xf, a1, a2, c1, c2, wo[:A_W], wo[A_W:], p['xa_norm'][layer], w['xa_w_q'][layer],
                     kvm, w['xa_w_out'][layer], gffn, wr, seq, tm=512)
        if layer % 2 == 0:
            x2, hn = outs
            xf = _ffn(x2, hn, w['ev_w_gate'][i], w['ev_w_up'][i], w['ev_w_down'][i], tm=1024, tf=256)
        else:
            x2, hf, info = outs
            xf = _moe(x2, hf, info, w['od_moe_gate'][i], w['od_moe_up'][i], w['od_moe_down'][i],
                      tmg=1024, tf=512, tg=512, tc=256)
    return _final_norm(xf, p['final_norm'], tm=1024).reshape(batch, seq, d)


def kernel(x_prompt, x_sample, mem_prompt, mem_sample, rel_table, ev_norm_mix, ev_w_in, ev_na_rpb, ev_w_out, ev_norm_ffn, ev_w_gate, ev_w_up, ev_w_down, od_norm_mix, od_w_in, od_q_norm, od_k_norm, od_w_out, od_norm_ffn, od_router, od_moe_gate, od_moe_up, od_moe_down, xa_norm, xa_mem_norm, xa_w_q, xa_w_kv, xa_w_out, final_norm):
    p = dict(rel_table=rel_table,
             ev_norm_mix=ev_norm_mix, ev_w_in=ev_w_in, ev_na_rpb=ev_na_rpb, ev_w_out=ev_w_out,
             ev_norm_ffn=ev_norm_ffn, ev_w_gate=ev_w_gate, ev_w_up=ev_w_up, ev_w_down=ev_w_down,
             od_norm_mix=od_norm_mix, od_w_in=od_w_in, od_q_norm=od_q_norm, od_k_norm=od_k_norm,
             od_w_out=od_w_out, od_norm_ffn=od_norm_ffn, od_router=od_router,
             od_moe_gate=od_moe_gate, od_moe_up=od_moe_up, od_moe_down=od_moe_down,
             xa_norm=xa_norm, xa_mem_norm=xa_mem_norm, xa_w_q=xa_w_q, xa_w_kv=xa_w_kv,
             xa_w_out=xa_w_out, final_norm=final_norm)
    w = _prepare(p)
    return (_trunk(x_prompt, mem_prompt, p, w), _trunk(x_sample, mem_sample, p, w))
```

```python
import functools
import math

import numpy as np
import jax
import jax.numpy as jnp
from jax import lax
from jax.experimental import pallas as pl
from jax.experimental.pallas import tpu as pltpu

F32 = jnp.float32
BF16 = jnp.bfloat16

D_MODEL = 1024
HEAD_DIM = 64
GRID_W = 64
N_MEM = 256
A_HEADS = 8
A_PATTERNS = ((128, 1), (512, 4), (2048, 16))
A_HALF = 64
B_HEADS = 8
NA_KH = 8
NA_KW = 16
C_Q_HEADS = 16
C_KV_HEADS = 4
ROPE_THETA = 10000.0
N_BUCKETS = 32
MAX_DISTANCE = 1024
X_HEADS = 4
N_EXPERTS = 8
EPS = 1e-6
NEG = -1e30
A_W = A_HEADS * HEAD_DIM
X_W = X_HEADS * HEAD_DIM
Q_SCALE = HEAD_DIM ** -0.5
QK_SCALE = Q_SCALE * math.log2(math.e)

LANES = 128
MXU_N = 256
GQA_ONES_ROWS = 16
VMEM_LIMIT = 48 * 1024 * 1024

NA_QROWS = 4
NA_KROWS = 12


def _params(*sem):
    return pltpu.CompilerParams(dimension_semantics=sem, vmem_limit_bytes=VMEM_LIMIT)


def _rms(x, g):
    ms = jnp.mean(x * x, axis=-1, keepdims=True)
    return x * lax.rsqrt(ms + EPS) * g


def _dot(a, b):
    return jnp.dot(a, b, preferred_element_type=F32)


def _dot_nt(a, b):
    return lax.dot_general(a, b, (((1,), (1,)), ((), ())), preferred_element_type=F32)


def _half_masks(rows):
    lane = lax.broadcasted_iota(jnp.int32, (rows, LANES), 1)
    lo = lane < HEAD_DIM
    return lo, jnp.logical_not(lo)


def _softmax_pv(s, v):
    m = jnp.max(s, axis=1, keepdims=True)
    e = jnp.exp(s - m)
    l = jnp.sum(e, axis=1, keepdims=True)
    o = _dot(e.astype(BF16), v) / l
    return o, m, l


def _proj_kernel(x_ref, g_ref, w_ref, o_ref):
    h = _rms(x_ref[...], g_ref[...]).astype(BF16)
    o_ref[...] = _dot(h, w_ref[...]).astype(o_ref.dtype)


def _proj(x, g, w, tm):
    n, d = x.shape
    nout = w.shape[1]
    return pl.pallas_call(
        _proj_kernel,
        grid=(n // tm,),
        in_specs=[pl.BlockSpec((tm, d), lambda i: (i, 0)),
                  pl.BlockSpec((1, d), lambda i: (0, 0)),
                  pl.BlockSpec((d, nout), lambda i: (0, 0))],
        out_specs=pl.BlockSpec((tm, nout), lambda i: (i, 0)),
        out_shape=jax.ShapeDtypeStruct((n, nout), BF16),
        compiler_params=_params("parallel"),
        name="proj",
    )(x, g.reshape(1, d), w)


def _proj_even_kernel(x_ref, g_ref, w_ref, o_ref, *rest):
    views, y_ref = rest[:-1], rest[-1]
    h = _rms(x_ref[...], g_ref[...]).astype(BF16)
    y = _dot(h, w_ref[...])
    o_ref[...] = y.astype(o_ref.dtype)
    tm = y.shape[0]
    ncol = y_ref.shape[0]
    for c in range(ncol):
        y_ref[c] = y[:, c * LANES:(c + 1) * LANES]
    for v_ref in views:
        dil = v_ref.shape[1]
        for r in range(dil):
            for c in range(ncol):
                rows = y_ref[c, pl.ds(r, tm // dil, stride=dil), :]
                v_ref[0, r, :, c * LANES:(c + 1) * LANES] = rows.astype(v_ref.dtype)


def _proj_even(x, g, w, seq, tm):
    n, d = x.shape
    nout = w.shape[1]
    spb = seq // tm
    batch = n // seq
    dils = [dil for _, dil in A_PATTERNS if dil > 1]
    aw = 3 * A_W
    return pl.pallas_call(
        _proj_even_kernel,
        grid=(n // tm,),
        in_specs=[pl.BlockSpec((tm, d), lambda i: (i, 0)),
                  pl.BlockSpec((1, d), lambda i: (0, 0)),
                  pl.BlockSpec((d, nout), lambda i: (0, 0))],
        out_specs=[pl.BlockSpec((tm, nout), lambda i: (i, 0))]
        + [pl.BlockSpec((1, dil, tm // dil, aw), lambda i: (i // spb, 0, i % spb, 0)) for dil in dils],
        out_shape=[jax.ShapeDtypeStruct((n, nout), BF16)]
        + [jax.ShapeDtypeStruct((batch, dil, seq // dil, aw), BF16) for dil in dils],
        scratch_shapes=[pltpu.VMEM((aw // LANES, tm, LANES), F32)],
        compiler_params=_params("parallel"),
        name="proj_even",
    )(x, g.reshape(1, d), w)


def _proj_odd_kernel(x_ref, g_ref, w_ref, qg_ref, kg_ref, cos_ref, sin_ref, q_ref, k_ref, v_ref):
    h = _rms(x_ref[...], g_ref[...]).astype(BF16)
    y = _dot(h, w_ref[...])
    tm = y.shape[0]
    cos = cos_ref[...]
    sin = sin_ref[...]
    lane = lax.broadcasted_iota(jnp.int32, (tm, LANES), 1)
    lo = lane < HEAD_DIM
    first = (lane & 31) < 16

    def norm_rope(c, gain, scale):
        ss = c * c
        s_lo = jnp.sum(jnp.where(lo, ss, 0.0), axis=1, keepdims=True)
        s_hi = jnp.sum(jnp.where(lo, 0.0, ss), axis=1, keepdims=True)
        ms = jnp.where(lo, s_lo, s_hi) * (1.0 / HEAD_DIM)
        c = c * lax.rsqrt(ms + EPS) * gain
        partner = jnp.where(first, pltpu.roll(c, LANES - 16, 1), pltpu.roll(c, 16, 1))
        return (c * cos + partner * sin) * scale

    nq = q_ref.shape[1] // LANES
    nk = k_ref.shape[1] // LANES
    for j in range(nq):
        c = y[:, j * LANES:(j + 1) * LANES]
        q_ref[0, j * LANES:(j + 1) * LANES, :] = norm_rope(c, qg_ref[...], QK_SCALE).T.astype(BF16)
    for j in range(nk):
        c = y[:, (nq + j) * LANES:(nq + j + 1) * LANES]
        k_ref[:, j * LANES:(j + 1) * LANES] = norm_rope(c, kg_ref[...], 1.0).astype(BF16)
    for j in range(nk):
        c = y[:, (nq + nk + j) * LANES:(nq + nk + j + 1) * LANES]
        v_ref[0, j * LANES:(j + 1) * LANES, :] = c.T.astype(BF16)


def _proj_odd(x, g, w, qg, kg, cos, sin, seq, tm):
    n, d = x.shape
    nq = C_Q_HEADS * HEAD_DIM
    nk = C_KV_HEADS * HEAD_DIM
    spb = seq // tm
    batch = n // seq
    return pl.pallas_call(
        _proj_odd_kernel,
        grid=(n // tm,),
        in_specs=[pl.BlockSpec((tm, d), lambda i: (i, 0)),
                  pl.BlockSpec((1, d), lambda i: (0, 0)),
                  pl.BlockSpec((d, nq + 2 * nk), lambda i: (0, 0)),
                  pl.BlockSpec((1, LANES), lambda i: (0, 0)),
                  pl.BlockSpec((1, LANES), lambda i: (0, 0)),
                  pl.BlockSpec((tm, LANES), lambda i: (i % spb, 0)),
                  pl.BlockSpec((tm, LANES), lambda i: (i % spb, 0))],
        out_specs=[pl.BlockSpec((1, nq, tm), lambda i: (i // spb, 0, i % spb)),
                   pl.BlockSpec((tm, nk), lambda i: (i, 0)),
                   pl.BlockSpec((1, nk, tm), lambda i: (i // spb, 0, i % spb))],
        out_shape=[jax.ShapeDtypeStruct((batch, nq, seq), BF16),
                   jax.ShapeDtypeStruct((n, nk), BF16),
                   jax.ShapeDtypeStruct((batch, nk, seq), BF16)],
        compiler_params=_params("parallel"),
        name="proj_odd",
    )(x, g.reshape(1, d), w, qg, kg, cos, sin)


def _dil_kernel(*refs, nsub, mix):
    if mix:
        (q_ref, kp_ref, kc_ref, kn_ref, vp_ref, vc_ref, vn_ref, bias_ref,
         o0_ref, l0_ref, o1_ref, l1_ref, o_ref, kwin, vwin) = refs
    else:
        (q_ref, kp_ref, kc_ref, kn_ref, vp_ref, vc_ref, vn_ref, bias_ref,
         o_ref, lse_ref, kwin, vwin) = refs
    tq = q_ref.shape[2]
    i = pl.program_id(2)
    nblk = pl.num_programs(2)
    kwin[0:A_HALF] = kp_ref[0, 0]
    kwin[A_HALF:A_HALF + tq] = kc_ref[0, 0]
    kwin[A_HALF + tq:] = kn_ref[0, 0]
    vwin[0:A_HALF] = vp_ref[0, 0]
    vwin[A_HALF:A_HALF + tq] = vc_ref[0, 0]
    vwin[A_HALF + tq:] = vn_ref[0, 0]
    lo, hi = _half_masks(LANES)
    lane = lax.broadcasted_iota(jnp.int32, (LANES, LANES), 1)
    kj = lax.broadcasted_iota(jnp.int32, (LANES, 2 * LANES), 1)
    zero = jnp.zeros((), BF16)
    for a in range(nsub):
        rows = slice(a * LANES, (a + 1) * LANES)
        valid = None
        if a == 0:
            valid = kj >= jnp.where(i == 0, A_HALF, 0)
        if a == nsub - 1:
            v2 = kj < jnp.where(i == nblk - 1, LANES + A_HALF, 2 * LANES)
            valid = v2 if valid is None else jnp.logical_and(valid, v2)
        lse_all = jnp.zeros((LANES, LANES), F32)
        for p in range(A_HEADS // 2):
            cols = slice(p * LANES, (p + 1) * LANES)
            qp = q_ref[0, 0, rows, cols]
            kk = kwin[a * LANES:(a + 2) * LANES, cols]
            vv = vwin[a * LANES:(a + 2) * LANES, cols]
            halves = []
            for hf in range(2):
                h = 2 * p + hf
                qm = jnp.where(lo if hf == 0 else hi, qp, zero)
                s = _dot_nt(qm, kk) + bias_ref[h]
                if valid is not None:
                    s = jnp.where(valid, s, NEG)
                o, m, l = _softmax_pv(s, vv)
                lse = m + jnp.log(l)
                if mix:
                    la = l0_ref[0, rows, 16 * h:16 * h + 1]
                    lb = l1_ref[0, rows, 16 * h:16 * h + 1]
                    mx = jnp.maximum(jnp.maximum(la, lb), lse)
                    wa = jnp.exp(la - mx)
                    wb = jnp.exp(lb - mx)
                    wc = jnp.exp(lse - mx)
                    den = wa + wb + wc
                    o = ((wa / den) * o0_ref[0, rows, cols].astype(F32)
                         + (wb / den) * o1_ref[0, rows, cols].astype(F32)
                         + (wc / den) * o)
                else:
                    lse_all = jnp.where(lane // 16 == h, lse, lse_all)
                halves.append(o)
            o_ref[0, rows, cols] = jnp.where(lo, halves[0], halves[1]).astype(o_ref.dtype)
        if not mix:
            lse_ref[0, rows, :] = lse_all


def _dilated_pattern(qkv, batch, seq, dil, bias, prev):
    n = batch * seq
    sub = seq // dil
    tq = min(sub, 4 * LANES)
    nsub = tq // LANES
    nblk = sub // tq
    hb = tq // A_HALF
    nhalo = sub // A_HALF

    def main(c):
        return pl.BlockSpec((1, 1, tq, A_W), lambda b, r, i: (b, r, i, c))

    def before(c):
        return pl.BlockSpec((1, 1, A_HALF, A_W),
                            lambda b, r, i: (b, r, jnp.maximum(i * hb - 1, 0), c))

    def after(c):
        return pl.BlockSpec((1, 1, A_HALF, A_W),
                            lambda b, r, i: (b, r, jnp.minimum((i + 1) * hb, nhalo - 1), c))

    o_spec = pl.BlockSpec((1, tq, A_W), lambda b, r, i: (b, i, r))
    l_spec = pl.BlockSpec((1, tq, LANES), lambda b, r, i: (b, i, r))
    in_specs = [main(0), before(1), main(1), after(1), before(2), main(2), after(2),
                pl.BlockSpec((A_HEADS, LANES, 2 * LANES), lambda b, r, i: (0, 0, 0))]
    args = [qkv] * 7 + [bias]
    o_shape = jax.ShapeDtypeStruct((batch, sub, dil * A_W), BF16)
    l_shape = jax.ShapeDtypeStruct((batch, sub, dil * LANES), F32)
    if prev is None:
        out_specs, out_shape = [o_spec, l_spec], [o_shape, l_shape]
    else:
        o0, l0, o1, l1 = prev
        in_specs += [o_spec, l_spec, o_spec, l_spec]
        args += [o0.reshape(o_shape.shape), l0.reshape(l_shape.shape),
                 o1.reshape(o_shape.shape), l1.reshape(l_shape.shape)]
        out_specs, out_shape = o_spec, o_shape
    out = pl.pallas_call(
        functools.partial(_dil_kernel, nsub=nsub, mix=prev is not None),
        grid=(batch, dil, nblk),
        in_specs=in_specs,
        out_specs=out_specs,
        out_shape=out_shape,
        scratch_shapes=[pltpu.VMEM((tq + 2 * A_HALF, A_W), BF16),
                        pltpu.VMEM((tq + 2 * A_HALF, A_W), BF16)],
        compiler_params=_params("parallel", "parallel", "parallel"),
        name=f"dilated_d{dil}",
    )(*args)
    if prev is None:
        return out[0].reshape(n, A_W), out[1].reshape(n, LANES)
    return out.reshape(n, A_W)


def _dilated(views, batch, seq, biases):
    o0, l0 = _dilated_pattern(views[0], batch, seq, A_PATTERNS[0][1], biases[0], None)
    o1, l1 = _dilated_pattern(views[1], batch, seq, A_PATTERNS[1][1], biases[1], None)
    return _dilated_pattern(views[2], batch, seq, A_PATTERNS[2][1], biases[2], (o0, l0, o1, l1))


def _na_kernel(q_ref, k0_ref, k1_ref, k2_ref, v0_ref, v1_ref, v2_ref, bias_ref, o_ref):
    tq = q_ref.shape[1]
    lo, hi = _half_masks(tq)
    zero = jnp.zeros((), BF16)
    for p in range(B_HEADS // 2):
        cols = slice(p * LANES, (p + 1) * LANES)
        qp = q_ref[0, :, cols]
        kk = jnp.concatenate([k0_ref[0, :, cols], k1_ref[0, :, cols], k2_ref[0, :, cols]], axis=0)
        vv = jnp.concatenate([v0_ref[0, :, cols], v1_ref[0, :, cols], v2_ref[0, :, cols]], axis=0)
        halves = []
        for hf in range(2):
            qm = jnp.where(lo if hf == 0 else hi, qp, zero)
            s = _dot_nt(qm, kk) + bias_ref[0, 2 * p + hf]
            o, _, _ = _softmax_pv(s, vv)
            halves.append(o)
        o_ref[0, :, cols] = jnp.where(lo, halves[0], halves[1]).astype(o_ref.dtype)


def _natten(pr, batch, seq, bias):
    n = batch * seq
    tq = NA_QROWS * GRID_W
    nblk = seq // tq
    nkb = NA_KROWS // NA_QROWS
    ncol = pr.shape[1] // A_W
    prv = pr.reshape(batch, seq, pr.shape[1])

    def kv_spec(c, j):
        return pl.BlockSpec((1, tq, A_W),
                            lambda b, i: (b, jnp.clip(i - 1, 0, nblk - nkb) + j, c))

    def variant(i):
        return jnp.where(i == 0, 0, jnp.where(i == nblk - 1, 2, 1))

    in_specs = ([pl.BlockSpec((1, tq, A_W), lambda b, i: (b, i, 3))]
                + [kv_spec(4, j) for j in range(nkb)] + [kv_spec(5, j) for j in range(nkb)]
                + [pl.BlockSpec((1, B_HEADS, tq, nkb * tq), lambda b, i: (variant(i), 0, 0, 0))])
    out = pl.pallas_call(
        _na_kernel,
        grid=(batch, nblk),
        in_specs=in_specs,
        out_specs=pl.BlockSpec((1, tq, A_W), lambda b, i: (b, i, 0)),
        out_shape=jax.ShapeDtypeStruct((batch, seq, A_W), BF16),
        compiler_params=_params("parallel", "parallel"),
        name="natten",
    )(*([prv] * (1 + 2 * nkb) + [bias]))
    return out.reshape(n, A_W)


def _gqa_kernel(qt_ref, k_ref, vt_ref, o_ref, st_ref, e_ref, *, tk):
    tq = qt_ref.shape[2]
    nchunk = k_ref.shape[1] // tk
    heads = qt_ref.shape[1] // HEAD_DIM
    second = pl.program_id(1) % 2 == 1
    zeros = jnp.zeros((HEAD_DIM, tq), BF16)
    cols = []
    for h in range(heads):
        qh = qt_ref[0, h * HEAD_DIM:(h + 1) * HEAD_DIM, :]
        cols.append(jnp.where(second, jnp.concatenate([zeros, qh], axis=0),
                              jnp.concatenate([qh, zeros], axis=0)))
    qst = jnp.concatenate(cols, axis=1)
    width = heads * tq
    nchain = width // MXU_N
    qsts = [qst[:, j * MXU_N:(j + 1) * MXU_N] for j in range(nchain)]

    ones = jnp.ones((GQA_ONES_ROWS, tk), BF16)

    def chunk(c):
        return pl.ds(c * tk if isinstance(c, int) else pl.multiple_of(c * tk, tk), tk)

    def scores(c, slot):
        kc = k_ref[0, chunk(c), :]
        for j in range(nchain):
            st_ref[slot, j] = _dot(kc, qsts[j])

    def values(c, slot, acc, alpha):
        vc = jnp.concatenate([vt_ref[0, :, chunk(c)], ones], axis=0)
        return tuple(acc[j] * alpha[j] + _dot(vc, e_ref[slot, j]) for j in range(nchain))

    def softmax(slot, m):
        m_new, alpha = [], []
        for j in range(nchain):
            s = st_ref[slot, j]
            mn = jnp.maximum(m[j], jnp.max(s, axis=0, keepdims=True))
            e_ref[slot, j] = jnp.exp2((s - mn).astype(BF16))
            alpha.append(jnp.exp2(m[j] - mn))
            m_new.append(mn)
        return tuple(m_new), tuple(alpha)

    def step(c, carry, ahead=True):
        slot, m, acc, a2, a1 = carry[0] % 4, *carry[1:]
        before = max(c - 2, 0) if isinstance(c, int) else jnp.maximum(c - 2, 0)
        acc = values(before, (slot + 2) % 4, acc, a2)
        if ahead:
            scores(c + 2, (slot + 2) % 4)
        m, a0 = softmax(slot, m)
        return carry[0] + 1, m, acc, a1, a0

    def body(i, carry):
        state = (0,) + carry
        for u in range(4):
            state = step(4 * i + u, state)
        return state[1:]

    row = lambda v: tuple(jnp.full((1, MXU_N), v, F32) for _ in range(nchain))
    carry = (row(NEG), tuple(jnp.zeros((HEAD_DIM + GQA_ONES_ROWS, MXU_N), F32) for _ in range(nchain)),
             row(1.0), row(1.0))
    scores(0, 0)
    scores(1, 1)
    for slot in (2, 3):
        e_ref[slot] = jnp.zeros(e_ref.shape[1:], BF16)
    carry = lax.fori_loop(0, nchunk // 4 - 1, body, carry)
    state = (0,) + carry
    for u in range(4):
        state = step(nchunk - 4 + u, state, ahead=u < 2)
    _, m, acc, a2, a1 = state
    acc = values(nchunk - 2, 2, acc, a2)
    acc = values(nchunk - 1, 3, acc, a1)
    o = jnp.concatenate([a[:HEAD_DIM] / a[HEAD_DIM:HEAD_DIM + 1] for a in acc], axis=1)
    ot = jnp.concatenate([o[:, h * tq:(h + 1) * tq] for h in range(heads)], axis=0)
    o_ref[0] = ot.T.astype(o_ref.dtype)


def _gqa(qt, k, vt, batch, seq, tq=256, tk=512):
    n = batch * seq
    gw = (C_Q_HEADS // C_KV_HEADS) * HEAD_DIM
    nchain = (C_Q_HEADS // C_KV_HEADS) * tq // MXU_N
    assert seq % (4 * tk) == 0
    out = pl.pallas_call(
        functools.partial(_gqa_kernel, tk=tk),
        grid=(batch, C_KV_HEADS, seq // tq),
        in_specs=[pl.BlockSpec((1, gw, tq), lambda b, g, i: (b, g, i)),
                  pl.BlockSpec((1, seq, LANES), lambda b, g, i: (b, 0, g // 2)),
                  pl.BlockSpec((1, HEAD_DIM, seq), lambda b, g, i: (b, g, 0))],
        out_specs=pl.BlockSpec((1, tq, gw), lambda b, g, i: (b, i, g)),
        out_shape=jax.ShapeDtypeStruct((batch, seq, C_Q_HEADS * HEAD_DIM), BF16),
        scratch_shapes=[pltpu.VMEM((4, nchain, tk, MXU_N), F32),
                        pltpu.VMEM((4, nchain, tk, MXU_N), BF16)],
        compiler_params=_params("parallel", "parallel", "parallel"),
        name="gqa",
    )(qt, k.reshape(batch, seq, -1), vt)
    return out.reshape(n, C_Q_HEADS * HEAD_DIM)


def _post_kernel(*refs, route):
    if route:
        (x_ref, a1_ref, a2_ref, wo1_ref, wo2_ref, gxa_ref, wq_ref, kv_ref, wox_ref, gffn_ref,
         wr_ref, x2_ref, hn_ref, info_ref) = refs
    else:
        (x_ref, a1_ref, a2_ref, wo1_ref, wo2_ref, gxa_ref, wq_ref, kv_ref, wox_ref, gffn_ref,
         x2_ref, hn_ref) = refs
    x1 = x_ref[...] + _dot(a1_ref[...], wo1_ref[...]) + _dot(a2_ref[...], wo2_ref[...])
    tm = x1.shape[0]
    h = _rms(x1, gxa_ref[...]).astype(BF16)
    q = _dot(h, wq_ref[...]).astype(BF16)
    lo, hi = _half_masks(tm)
    zero = jnp.zeros((), BF16)
    outs = []
    for p in range(X_HEADS // 2):
        qp = q[:, p * LANES:(p + 1) * LANES]
        kk = kv_ref[0, :, p * LANES:(p + 1) * LANES]
        vv = kv_ref[0, :, X_W + p * LANES:X_W + (p + 1) * LANES]
        halves = []
        for hf in range(2):
            qm = jnp.where(lo if hf == 0 else hi, qp, zero)
            o, _, _ = _softmax_pv(_dot_nt(qm, kk), vv)
            halves.append(o)
        outs.append(jnp.where(lo, halves[0], halves[1]).astype(BF16))
    x2 = x1 + _dot(jnp.concatenate(outs, axis=1), wox_ref[...])
    x2_ref[...] = x2
    hf32 = _rms(x2, gffn_ref[...])
    hn_ref[...] = hf32.astype(hn_ref.dtype)
    if route:
        lane = lax.broadcasted_iota(jnp.int32, (tm, LANES), 1).astype(F32)
        lg = jnp.full((tm, LANES), NEG, F32)
        for ex in range(N_EXPERTS):
            le = jnp.sum(hf32 * wr_ref[ex:ex + 1, :], axis=1, keepdims=True)
            lg = jnp.where(lane == float(ex), le, lg)
        m1 = jnp.max(lg, axis=1, keepdims=True)
        i1 = jnp.min(jnp.where(lg == m1, lane, float(LANES)), axis=1, keepdims=True)
        lg2 = jnp.where(lane == i1, NEG, lg)
        m2 = jnp.max(lg2, axis=1, keepdims=True)
        i2 = jnp.min(jnp.where(lg2 == m2, lane, float(LANES)), axis=1, keepdims=True)
        e = jnp.exp(m2 - m1)
        den = 1.0 + e
        info_ref[...] = (jnp.where(lane == 0.0, i1, 0.0) + jnp.where(lane == 1.0, i2, 0.0)
                         + jnp.where(lane == 2.0, 1.0 / den, 0.0) + jnp.where(lane == 3.0, e / den, 0.0))


def _post(x, a1, a2, c1, c2, wo1, wo2, gxa, wq, kvm, wox, gffn, wr, seq, tm):
    n, d = x.shape
    spb = seq // tm
    row = lambda i: (i, 0)
    const = lambda i: (0, 0)
    in_specs = [pl.BlockSpec((tm, d), row),
                pl.BlockSpec((tm, A_W), lambda i: (i, c1)),
                pl.BlockSpec((tm, A_W), lambda i: (i, c2)),
                pl.BlockSpec((A_W, d), const),
                pl.BlockSpec((A_W, d), const),
                pl.BlockSpec((1, d), const),
                pl.BlockSpec((d, X_W), const),
                pl.BlockSpec((1, N_MEM, 2 * X_W), lambda i: (i // spb, 0, 0)),
                pl.BlockSpec((X_W, d), const),
                pl.BlockSpec((1, d), const)]
    args = [x, a1, a2, wo1, wo2, gxa.reshape(1, d), wq, kvm, wox, gffn.reshape(1, d)]
    out_specs = [pl.BlockSpec((tm, d), row), pl.BlockSpec((tm, d), row)]
    out_shape = [jax.ShapeDtypeStruct((n, d), F32),
                 jax.ShapeDtypeStruct((n, d), BF16 if wr is None else F32)]
    if wr is not None:
        in_specs.append(pl.BlockSpec((N_EXPERTS, d), const))
        args.append(wr)
        out_specs.append(pl.BlockSpec((tm, LANES), row))
        out_shape.append(jax.ShapeDtypeStruct((n, LANES), F32))
    return pl.pallas_call(
        functools.partial(_post_kernel, route=wr is not None),
        grid=(n // tm,),
        in_specs=in_specs,
        out_specs=out_specs,
        out_shape=out_shape,
        compiler_params=_params("parallel"),
        name="post_route" if wr is not None else "post",
    )(*args)


def _swiglu_chunk(h, wg, wu, wd):
    g = _dot(h, wg)
    u = _dot(h, wu)
    a = (g / (1.0 + jnp.exp(-g))) * u
    return _dot(a.astype(BF16), wd)


def _ffn_kernel(x_ref, h_ref, wg_ref, wu_ref, wd_ref, o_ref):
    f = pl.program_id(1)
    y = _swiglu_chunk(h_ref[...], wg_ref[...], wu_ref[...], wd_ref[...])

    @pl.when(f == 0)
    def _():
        o_ref[...] = x_ref[...] + y

    @pl.when(f > 0)
    def _():
        o_ref[...] += y


def _ffn(x, hn, wg, wu, wd, tm, tf):
    n, d = x.shape
    ff = wg.shape[1]
    return pl.pallas_call(
        _ffn_kernel,
        grid=(n // tm, ff // tf),
        in_specs=[pl.BlockSpec((tm, d), lambda i, f: (i, 0)),
                  pl.BlockSpec((tm, d), lambda i, f: (i, 0)),
                  pl.BlockSpec((d, tf), lambda i, f: (0, f)),
                  pl.BlockSpec((d, tf), lambda i, f: (0, f)),
                  pl.BlockSpec((tf, d), lambda i, f: (f, 0))],
        out_specs=pl.BlockSpec((tm, d), lambda i, f: (i, 0)),
        out_shape=jax.ShapeDtypeStruct((n, d), F32),
        compiler_params=_params("parallel", "arbitrary"),
        name="ffn",
    )(x, hn, wg, wu, wd)


def _route(info, tmg):
    n = info.shape[0]
    ea = info[:, 0:2].astype(jnp.int32).reshape(-1)
    onehot = (ea[:, None] == jnp.arange(N_EXPERTS, dtype=jnp.int32)[None, :]).astype(jnp.int32)
    csum = jnp.cumsum(onehot, axis=0)
    counts = csum[-1]
    padded = (counts + tmg - 1) // tmg * tmg
    ends = jnp.cumsum(padded)
    starts = ends - padded
    dest = jnp.sum(onehot * (csum - 1 + starts[None, :]), axis=1)
    r_max = 2 * n + N_EXPERTS * tmg
    token = jnp.arange(2 * n, dtype=jnp.int32) // 2
    src = jnp.zeros((r_max,), jnp.int32).at[dest].set(token, unique_indices=True)
    tile_start = jnp.arange(r_max // tmg, dtype=jnp.int32) * tmg
    tile_e = jnp.sum((tile_start[:, None] >= ends[None, :]).astype(jnp.int32), axis=1)
    tile_e = jnp.minimum(tile_e, N_EXPERTS - 1)
    tile_ok = (tile_start < ends[-1]).astype(jnp.int32)
    return src, dest, tile_e, tile_ok


def _row_copy(src_ref, row, dst_ref, r, sem):
    return pltpu.make_async_copy(src_ref.at[pl.ds(row, 1)], dst_ref.at[pl.ds(r, 1)], sem)


def _gather_kernel(idx_ref, src_ref, o_ref, buf, sem):
    i = pl.program_id(0)
    ntile = pl.num_programs(0) - 1
    tg = o_ref.shape[0]

    @pl.when(i < ntile)
    def _():
        slot = i % 2

        def start(r, c):
            _row_copy(src_ref, idx_ref[0, 0, r], buf.at[slot], r, sem.at[slot]).start()
            return c

        lax.fori_loop(0, tg, start, 0, unroll=8)

    @pl.when(i > 0)
    def _():
        slot = (i + 1) % 2

        def wait(r, c):
            _row_copy(src_ref, 0, buf.at[slot], 0, sem.at[slot]).wait()
            return c

        lax.fori_loop(0, tg, wait, 0, unroll=8)
        o_ref[...] = buf[slot]


def _gather_rows(src, idx, tg):
    n, d = src.shape
    r = idx.shape[0]
    ntile = r // tg
    return pl.pallas_call(
        _gather_kernel,
        grid=(ntile + 1,),
        in_specs=[pl.BlockSpec((1, 1, tg), lambda i: (jnp.minimum(i, ntile - 1), 0, 0),
                               memory_space=pltpu.SMEM),
                  pl.BlockSpec(memory_space=pl.ANY)],
        out_specs=pl.BlockSpec((tg, d), lambda i: (jnp.maximum(i - 1, 0), 0)),
        out_shape=jax.ShapeDtypeStruct((r, d), src.dtype),
        scratch_shapes=[pltpu.VMEM((2, tg, d), src.dtype), pltpu.SemaphoreType.DMA((2,))],
        compiler_params=_params("arbitrary"),
        name="moe_gather",
    )(idx.reshape(ntile, 1, tg), src)


def _expert_kernel(te_ref, ok_ref, x_ref, wg_ref, wu_ref, wd_ref, o_ref, hb_ref):
    t = pl.program_id(0)
    f = pl.program_id(1)
    ok = ok_ref[t] == 1

    @pl.when(jnp.logical_and(ok, f == 0))
    def _():
        hb_ref[...] = x_ref[...].astype(BF16)
        o_ref[...] = _swiglu_chunk(hb_ref[...], wg_ref[0], wu_ref[0], wd_ref[0])

    @pl.when(jnp.logical_and(ok, f > 0))
    def _():
        o_ref[...] += _swiglu_chunk(hb_ref[...], wg_ref[0], wu_ref[0], wd_ref[0])

    @pl.when(jnp.logical_and(jnp.logical_not(ok), f == 0))
    def _():
        o_ref[...] = jnp.zeros_like(o_ref)


def _expert_ffn(xs, tile_e, tile_ok, wg, wu, wd, tmg, tf):
    r, d = xs.shape
    ff = wg.shape[2]
    grid_spec = pltpu.PrefetchScalarGridSpec(
        num_scalar_prefetch=2,
        grid=(r // tmg, ff // tf),
        in_specs=[pl.BlockSpec((tmg, d), lambda t, f, te, ok: (t, 0)),
                  pl.BlockSpec((1, d, tf), lambda t, f, te, ok: (te[t], 0, f * ok[t])),
                  pl.BlockSpec((1, d, tf), lambda t, f, te, ok: (te[t], 0, f * ok[t])),
                  pl.BlockSpec((1, tf, d), lambda t, f, te, ok: (te[t], f * ok[t], 0))],
        out_specs=pl.BlockSpec((tmg, d), lambda t, f, te, ok: (t, 0)),
        scratch_shapes=[pltpu.VMEM((tmg, d), BF16)],
    )
    return pl.pallas_call(
        _expert_kernel,
        grid_spec=grid_spec,
        out_shape=jax.ShapeDtypeStruct((r, d), F32),
        compiler_params=_params("arbitrary", "arbitrary"),
        name="moe_experts",
    )(tile_e, tile_ok, xs, wg, wu, wd)


def _combine_kernel(d_ref, x_ref, info_ref, ys_ref, o_ref, buf, sem):
    tc = o_ref.shape[0]

    def copies(r):
        return (_row_copy(ys_ref, d_ref[0, 0, 2 * r], buf.at[0], r, sem),
                _row_copy(ys_ref, d_ref[0, 0, 2 * r + 1], buf.at[1], r, sem))

    def start(r, c):
        for cp in copies(r):
            cp.start()
        return c

    def wait(r, c):
        for cp in copies(r):
            cp.wait()
        return c

    lax.fori_loop(0, tc, start, 0, unroll=4)
    lax.fori_loop(0, tc, wait, 0, unroll=4)
    g1 = info_ref[:, 2:3]
    g2 = info_ref[:, 3:4]
    o_ref[...] = x_ref[...] + (g1 * buf[0] + g2 * buf[1])


def _combine(x, info, dest, ys, tc):
    n, d = x.shape
    return pl.pallas_call(
        _combine_kernel,
        grid=(n // tc,),
        in_specs=[pl.BlockSpec((1, 1, 2 * tc), lambda i: (i, 0, 0), memory_space=pltpu.SMEM),
                  pl.BlockSpec((tc, d), lambda i: (i, 0)),
                  pl.BlockSpec((tc, LANES), lambda i: (i, 0)),
                  pl.BlockSpec(memory_space=pl.ANY)],
        out_specs=pl.BlockSpec((tc, d), lambda i: (i, 0)),
        out_shape=jax.ShapeDtypeStruct((n, d), F32),
        scratch_shapes=[pltpu.VMEM((2, tc, d), F32), pltpu.SemaphoreType.DMA(())],
        compiler_params=_params("arbitrary"),
        name="moe_combine",
    )(dest.reshape(n // tc, 1, 2 * tc), x, info, ys)


def _moe(x2, hf, info, wg, wu, wd, tmg, tf, tg, tc):
    src, dest, tile_e, tile_ok = _route(info, tmg)
    xs = _gather_rows(hf, src, tg)
    ys = _expert_ffn(xs, tile_e, tile_ok, wg, wu, wd, tmg, tf)
    return _combine(x2, info, dest, ys, tc)


def _final_norm_kernel(x_ref, g_ref, o_ref):
    o_ref[...] = _rms(x_ref[...], g_ref[...])


def _final_norm(x, g, tm):
    n, d = x.shape
    return pl.pallas_call(
        _final_norm_kernel,
        grid=(n // tm,),
        in_specs=[pl.BlockSpec((tm, d), lambda i: (i, 0)), pl.BlockSpec((1, d), lambda i: (0, 0))],
        out_specs=pl.BlockSpec((tm, d), lambda i: (i, 0)),
        out_shape=jax.ShapeDtypeStruct((n, d), F32),
        compiler_params=_params("parallel"),
        name="final_norm",
    )(x, g.reshape(1, d))


def _t5_bucket_np(rel):
    nb = N_BUCKETS // 2
    max_exact = nb // 2
    ret = np.where(rel > 0, nb, 0)
    n = np.abs(rel)
    large = max_exact + (np.log(np.maximum(n, 1).astype(np.float32) / max_exact)
                         / math.log(MAX_DISTANCE / max_exact) * (nb - max_exact)).astype(np.int32)
    large = np.minimum(large, nb - 1)
    return ret + np.where(n < max_exact, n, large)


def _dilated_bias(rel_table, dil):
    off = np.arange(2 * LANES)[None, :] - A_HALF - np.arange(LANES)[:, None]
    bucket = _t5_bucket_np(off * dil)
    bias = rel_table[jnp.asarray(bucket)].transpose(2, 0, 1).astype(F32)
    return jnp.where(jnp.asarray(np.abs(off) <= A_HALF)[None], bias, NEG)


def _natten_bias(rpb):
    rows = 4 * NA_KROWS
    c = np.arange(GRID_W)
    cs = np.clip(c - NA_KW // 2, 0, GRID_W - NA_KW)
    col_ok = (c[None, :] >= cs[:, None]) & (c[None, :] < cs[:, None] + NA_KW)
    dc = np.clip(c[None, :] - c[:, None], -(NA_KW - 1), NA_KW - 1) + NA_KW - 1
    col_sel = (dc[:, :, None] == np.arange(2 * NA_KW - 1)).astype(np.float32)
    row_sel, row_ok = [], []
    for r0 in (0, 2 * NA_QROWS, rows - NA_QROWS):
        ks = int(np.clip(r0 - NA_KH // 2, 0, rows - NA_KROWS))
        qr = r0 + np.arange(NA_QROWS)
        kr = ks + np.arange(NA_KROWS)
        rs = np.clip(qr - NA_KH // 2, 0, rows - NA_KH)
        row_ok.append((kr[None, :] >= rs[:, None]) & (kr[None, :] < rs[:, None] + NA_KH))
        dr = kr[None, :] - qr[:, None] + NA_KH - 1
        row_sel.append((dr[:, :, None] == np.arange(2 * NA_KH - 1)).astype(np.float32))
    row_sel = np.stack(row_sel)
    ok = np.stack(row_ok)[:, :, None, :, None] & col_ok[None, None, :, None, :]
    by_col = jnp.einsum('hab,xyb->haxy', rpb.astype(F32), col_sel, precision=lax.Precision.HIGHEST)
    table = jnp.einsum('vqka,haxy->vhqxky', row_sel, by_col, precision=lax.Precision.HIGHEST)
    table = jnp.where(jnp.asarray(ok)[:, None], table, NEG)
    tq = NA_QROWS * GRID_W
    return table.reshape(len(row_ok), rpb.shape[0], tq, NA_KROWS * GRID_W)


def _rope_tables(seq):
    t = jnp.arange(seq)
    half = HEAD_DIM // 2
    freqs = ROPE_THETA ** (-jnp.arange(0, half, 2, dtype=F32) / half)
    ang_r = (t // GRID_W).astype(F32)[:, None] * freqs[None, :]
    ang_c = (t % GRID_W).astype(F32)[:, None] * freqs[None, :]
    cos_h = jnp.concatenate([jnp.cos(ang_r)] * 2 + [jnp.cos(ang_c)] * 2, axis=1)
    sin_h = jnp.concatenate([-jnp.sin(ang_r), jnp.sin(ang_r), -jnp.sin(ang_c), jnp.sin(ang_c)], axis=1)
    return jnp.concatenate([cos_h, cos_h], axis=1), jnp.concatenate([sin_h, sin_h], axis=1)


def _prepare(p):
    w = {}
    ev_in = p['ev_w_in']
    scale = np.ones((ev_in.shape[-1],), np.float32)
    scale[0:A_W] = Q_SCALE
    scale[3 * A_W:4 * A_W] = Q_SCALE
    w['ev_w_in'] = (ev_in * scale).astype(BF16)
    w['ev_w_out'] = p['ev_w_out'].astype(BF16)
    w['ev_w_gate'] = p['ev_w_gate'].astype(BF16)
    w['ev_w_up'] = p['ev_w_up'].astype(BF16)
    w['ev_w_down'] = p['ev_w_down'].astype(BF16)
    w['od_w_in'] = p['od_w_in'].astype(BF16)
    w['od_w_out'] = p['od_w_out'].astype(BF16)
    w['od_q_norm'] = jnp.concatenate([p['od_q_norm']] * 2, axis=-1)[:, None, :]
    w['od_k_norm'] = jnp.concatenate([p['od_k_norm']] * 2, axis=-1)[:, None, :]
    w['od_router'] = jnp.swapaxes(p['od_router'], 1, 2)
    w['od_moe_gate'] = p['od_moe_gate'].astype(BF16)
    w['od_moe_up'] = p['od_moe_up'].astype(BF16)
    w['od_moe_down'] = p['od_moe_down'].astype(BF16)
    w['xa_w_q'] = (p['xa_w_q'] * Q_SCALE).astype(BF16)
    w['xa_w_kv'] = p['xa_w_kv'].astype(BF16)
    w['xa_w_out'] = p['xa_w_out'].astype(BF16)
    w['dil_bias'] = [_dilated_bias(p['rel_table'], dil) for _, dil in A_PATTERNS]
    w['na_bias'] = [_natten_bias(p['ev_na_rpb'][i]) for i in range(p['ev_na_rpb'].shape[0])]
    return w


def _trunk(x, mem, p, w):
    batch, seq, d = x.shape
    n = batch * seq
    xf = x.reshape(n, d)
    memf = mem.reshape(batch * N_MEM, d)
    cos, sin = _rope_tables(seq)
    depth = p['xa_norm'].shape[0]
    for layer in range(depth):
        i = layer // 2
        if layer % 2 == 0:
            pr, v4, v16 = _proj_even(xf, p['ev_norm_mix'][i], w['ev_w_in'][i], seq, tm=512)
            a1 = _dilated([pr.reshape(batch, 1, seq, -1), v4, v16], batch, seq, w['dil_bias'])
            a2 = _natten(pr, batch, seq, w['na_bias'][i])
            c1 = c2 = 0
            wo = w['ev_w_out'][i]
            wr = None
        else:
            q, kk, vv = _proj_odd(xf, p['od_norm_mix'][i], w['od_w_in'][i], w['od_q_norm'][i],
                                  w['od_k_norm'][i], cos, sin, seq, tm=512)
            a1 = a2 = _gqa(q, kk, vv, batch, seq)
            c1, c2 = 0, 1
            wo = w['od_w_out'][i]
            wr = w['od_router'][i]
        kvm = _proj(memf, p['xa_mem_norm'][layer], w['xa_w_kv'][layer], tm=512)
        kvm = kvm.reshape(batch, N_MEM, 2 * X_W)
        gffn = p['ev_norm_ffn'][i] if layer % 2 == 0 else p['od_norm_ffn'][i]
        outs = _post(xf, a1, a2, c1, c2, wo[:A_W], wo[A_W:], p['xa_norm'][layer], w['xa_w_q'][layer],
                     kvm, w['xa_w_out'][layer], gffn, wr, seq, tm=512)
        if layer % 2 == 0:
            x2, hn = outs
            xf = _ffn(x2, hn, w['ev_w_gate'][i], w['ev_w_up'][i], w['ev_w_down'][i], tm=512, tf=1408)
        else:
            x2, hf, info = outs
            xf = _moe(x2, hf, info, w['od_moe_gate'][i], w['od_moe_up'][i], w['od_moe_down'][i],
                      tmg=1024, tf=896, tg=512, tc=256)
    return _final_norm(xf, p['final_norm'], tm=1024).reshape(batch, seq, d)


def kernel(x_prompt, x_sample, mem_prompt, mem_sample, rel_table, ev_norm_mix, ev_w_in, ev_na_rpb, ev_w_out, ev_norm_ffn, ev_w_gate, ev_w_up, ev_w_down, od_norm_mix, od_w_in, od_q_norm, od_k_norm, od_w_out, od_norm_ffn, od_router, od_moe_gate, od_moe_up, od_moe_down, xa_norm, xa_mem_norm, xa_w_q, xa_w_kv, xa_w_out, final_norm):
    p = dict(rel_table=rel_table,
             ev_norm_mix=ev_norm_mix, ev_w_in=ev_w_in, ev_na_rpb=ev_na_rpb, ev_w_out=ev_w_out,
             ev_norm_ffn=ev_norm_ffn, ev_w_gate=ev_w_gate, ev_w_up=ev_w_up, ev_w_down=ev_w_down,
             od_norm_mix=od_norm_mix, od_w_in=od_w_in, od_q_norm=od_q_norm, od_k_norm=od_k_norm,
             od_w_out=od_w_out, od_norm_ffn=od_norm_ffn, od_router=od_router,
             od_moe_gate=od_moe_gate, od_moe_up=od_moe_up, od_moe_down=od_moe_down,
             xa_norm=xa_norm, xa_mem_norm=xa_mem_norm, xa_w_q=xa_w_q, xa_w_kv=xa_w_kv,
             xa_w_out=xa_w_out, final_norm=final_norm)
    w = _prepare(p)
    return (_trunk(x_prompt, mem_prompt, p, w), _trunk(x_sample, mem_sample, p, w))
```

```python
import functools
import math

import numpy as np
import jax
import jax.numpy as jnp
from jax import lax
from jax.experimental import pallas as pl
from jax.experimental.pallas import tpu as pltpu

F32 = jnp.float32
BF16 = jnp.bfloat16

D_MODEL = 1024
HEAD_DIM = 64
GRID_W = 64
N_MEM = 256
A_HEADS = 8
A_PATTERNS = ((128, 1), (512, 4), (2048, 16))
A_HALF = 64
B_HEADS = 8
NA_KH = 8
NA_KW = 16
C_Q_HEADS = 16
C_KV_HEADS = 4
ROPE_THETA = 10000.0
N_BUCKETS = 32
MAX_DISTANCE = 1024
X_HEADS = 4
N_EXPERTS = 8
EPS = 1e-6
NEG = -1e30
A_W = A_HEADS * HEAD_DIM
X_W = X_HEADS * HEAD_DIM
Q_SCALE = HEAD_DIM ** -0.5
QK_SCALE = Q_SCALE * math.log2(math.e)

LANES = 128
MXU_N = 256
GQA_ONES_ROWS = 16
VMEM_LIMIT = 48 * 1024 * 1024

NA_QROWS = 4
NA_KROWS = 12


def _params(*sem):
    return pltpu.CompilerParams(dimension_semantics=sem, vmem_limit_bytes=VMEM_LIMIT)


def _rms(x, g):
    ms = jnp.mean(x * x, axis=-1, keepdims=True)
    return x * lax.rsqrt(ms + EPS) * g


def _dot(a, b):
    return jnp.dot(a, b, preferred_element_type=F32)


def _dot_nt(a, b):
    return lax.dot_general(a, b, (((1,), (1,)), ((), ())), preferred_element_type=F32)


def _half_masks(rows):
    lane = lax.broadcasted_iota(jnp.int32, (rows, LANES), 1)
    lo = lane < HEAD_DIM
    return lo, jnp.logical_not(lo)


def _softmax_pv(s, v):
    m = jnp.max(s, axis=1, keepdims=True)
    e = jnp.exp(s - m)
    l = jnp.sum(e, axis=1, keepdims=True)
    o = _dot(e.astype(BF16), v) / l
    return o, m, l


def _proj_kernel(x_ref, g_ref, w_ref, o_ref):
    h = _rms(x_ref[...], g_ref[...]).astype(BF16)
    o_ref[...] = _dot(h, w_ref[...]).astype(o_ref.dtype)


def _proj(x, g, w, tm):
    n, d = x.shape
    nout = w.shape[1]
    return pl.pallas_call(
        _proj_kernel,
        grid=(n // tm,),
        in_specs=[pl.BlockSpec((tm, d), lambda i: (i, 0)),
                  pl.BlockSpec((1, d), lambda i: (0, 0)),
                  pl.BlockSpec((d, nout), lambda i: (0, 0))],
        out_specs=pl.BlockSpec((tm, nout), lambda i: (i, 0)),
        out_shape=jax.ShapeDtypeStruct((n, nout), BF16),
        compiler_params=_params("parallel"),
        name="proj",
    )(x, g.reshape(1, d), w)


def _proj_even_kernel(x_ref, g_ref, w_ref, o_ref, *rest):
    views, y_ref = rest[:-1], rest[-1]
    h = _rms(x_ref[...], g_ref[...]).astype(BF16)
    y = _dot(h, w_ref[...])
    o_ref[...] = y.astype(o_ref.dtype)
    tm = y.shape[0]
    ncol = y_ref.shape[0]
    for c in range(ncol):
        y_ref[c] = y[:, c * LANES:(c + 1) * LANES]
    for v_ref in views:
        dil = v_ref.shape[1]
        for r in range(dil):
            for c in range(ncol):
                rows = y_ref[c, pl.ds(r, tm // dil, stride=dil), :]
                v_ref[0, r, :, c * LANES:(c + 1) * LANES] = rows.astype(v_ref.dtype)


def _proj_even(x, g, w, seq, tm):
    n, d = x.shape
    nout = w.shape[1]
    spb = seq // tm
    batch = n // seq
    dils = [dil for _, dil in A_PATTERNS if dil > 1]
    aw = 3 * A_W
    return pl.pallas_call(
        _proj_even_kernel,
        grid=(n // tm,),
        in_specs=[pl.BlockSpec((tm, d), lambda i: (i, 0)),
                  pl.BlockSpec((1, d), lambda i: (0, 0)),
                  pl.BlockSpec((d, nout), lambda i: (0, 0))],
        out_specs=[pl.BlockSpec((tm, nout), lambda i: (i, 0))]
        + [pl.BlockSpec((1, dil, tm // dil, aw), lambda i: (i // spb, 0, i % spb, 0)) for dil in dils],
        out_shape=[jax.ShapeDtypeStruct((n, nout), BF16)]
        + [jax.ShapeDtypeStruct((batch, dil, seq // dil, aw), BF16) for dil in dils],
        scratch_shapes=[pltpu.VMEM((aw // LANES, tm, LANES), F32)],
        compiler_params=_params("parallel"),
        name="proj_even",
    )(x, g.reshape(1, d), w)


def _proj_odd_kernel(x_ref, g_ref, w_ref, qg_ref, kg_ref, cos_ref, sin_ref, q_ref, k_ref, v_ref):
    h = _rms(x_ref[...], g_ref[...]).astype(BF16)
    y = _dot(h, w_ref[...])
    tm = y.shape[0]
    cos = cos_ref[...]
    sin = sin_ref[...]
    lane = lax.broadcasted_iota(jnp.int32, (tm, LANES), 1)
    lo = lane < HEAD_DIM
    first = (lane & 31) < 16

    def norm_rope(c, gain, scale):
        ss = c * c
        s_lo = jnp.sum(jnp.where(lo, ss, 0.0), axis=1, keepdims=True)
        s_hi = jnp.sum(jnp.where(lo, 0.0, ss), axis=1, keepdims=True)
        ms = jnp.where(lo, s_lo, s_hi) * (1.0 / HEAD_DIM)
        c = c * lax.rsqrt(ms + EPS) * gain
        partner = jnp.where(first, pltpu.roll(c, LANES - 16, 1), pltpu.roll(c, 16, 1))
        return (c * cos + partner * sin) * scale

    nq = q_ref.shape[1] // LANES
    nk = k_ref.shape[1] // LANES
    for j in range(nq):
        c = y[:, j * LANES:(j + 1) * LANES]
        q_ref[0, j * LANES:(j + 1) * LANES, :] = norm_rope(c, qg_ref[...], QK_SCALE).T.astype(BF16)
    for j in range(nk):
        c = y[:, (nq + j) * LANES:(nq + j + 1) * LANES]
        k_ref[:, j * LANES:(j + 1) * LANES] = norm_rope(c, kg_ref[...], 1.0).astype(BF16)
    for j in range(nk):
        c = y[:, (nq + nk + j) * LANES:(nq + nk + j + 1) * LANES]
        v_ref[0, j * LANES:(j + 1) * LANES, :] = c.T.astype(BF16)


def _proj_odd(x, g, w, qg, kg, cos, sin, seq, tm):
    n, d = x.shape
    nq = C_Q_HEADS * HEAD_DIM
    nk = C_KV_HEADS * HEAD_DIM
    spb = seq // tm
    batch = n // seq
    return pl.pallas_call(
        _proj_odd_kernel,
        grid=(n // tm,),
        in_specs=[pl.BlockSpec((tm, d), lambda i: (i, 0)),
                  pl.BlockSpec((1, d), lambda i: (0, 0)),
                  pl.BlockSpec((d, nq + 2 * nk), lambda i: (0, 0)),
                  pl.BlockSpec((1, LANES), lambda i: (0, 0)),
                  pl.BlockSpec((1, LANES), lambda i: (0, 0)),
                  pl.BlockSpec((tm, LANES), lambda i: (i % spb, 0)),
                  pl.BlockSpec((tm, LANES), lambda i: (i % spb, 0))],
        out_specs=[pl.BlockSpec((1, nq, tm), lambda i: (i // spb, 0, i % spb)),
                   pl.BlockSpec((tm, nk), lambda i: (i, 0)),
                   pl.BlockSpec((1, nk, tm), lambda i: (i // spb, 0, i % spb))],
        out_shape=[jax.ShapeDtypeStruct((batch, nq, seq), BF16),
                   jax.ShapeDtypeStruct((n, nk), BF16),
                   jax.ShapeDtypeStruct((batch, nk, seq), BF16)],
        compiler_params=_params("parallel"),
        name="proj_odd",
    )(x, g.reshape(1, d), w, qg, kg, cos, sin)


def _dil_kernel(*refs, nsub, mix):
    if mix:
        (q_ref, kp_ref, kc_ref, kn_ref, vp_ref, vc_ref, vn_ref, bias_ref,
         o0_ref, l0_ref, o1_ref, l1_ref, o_ref, kwin, vwin) = refs
    else:
        (q_ref, kp_ref, kc_ref, kn_ref, vp_ref, vc_ref, vn_ref, bias_ref,
         o_ref, lse_ref, kwin, vwin) = refs
    tq = q_ref.shape[2]
    i = pl.program_id(2)
    nblk = pl.num_programs(2)
    kwin[0:A_HALF] = kp_ref[0, 0]
    kwin[A_HALF:A_HALF + tq] = kc_ref[0, 0]
    kwin[A_HALF + tq:] = kn_ref[0, 0]
    vwin[0:A_HALF] = vp_ref[0, 0]
    vwin[A_HALF:A_HALF + tq] = vc_ref[0, 0]
    vwin[A_HALF + tq:] = vn_ref[0, 0]
    lo, hi = _half_masks(LANES)
    lane = lax.broadcasted_iota(jnp.int32, (LANES, LANES), 1)
    kj = lax.broadcasted_iota(jnp.int32, (LANES, 2 * LANES), 1)
    zero = jnp.zeros((), BF16)
    for a in range(nsub):
        rows = slice(a * LANES, (a + 1) * LANES)
        valid = None
        if a == 0:
            valid = kj >= jnp.where(i == 0, A_HALF, 0)
        if a == nsub - 1:
            v2 = kj < jnp.where(i == nblk - 1, LANES + A_HALF, 2 * LANES)
            valid = v2 if valid is None else jnp.logical_and(valid, v2)
        lse_all = jnp.zeros((LANES, LANES), F32)
        for p in range(A_HEADS // 2):
            cols = slice(p * LANES, (p + 1) * LANES)
            qp = q_ref[0, 0, rows, cols]
            kk = kwin[a * LANES:(a + 2) * LANES, cols]
            vv = vwin[a * LANES:(a + 2) * LANES, cols]
            halves = []
            for hf in range(2):
                h = 2 * p + hf
                qm = jnp.where(lo if hf == 0 else hi, qp, zero)
                s = _dot_nt(qm, kk) + bias_ref[h]
                if valid is not None:
                    s = jnp.where(valid, s, NEG)
                o, m, l = _softmax_pv(s, vv)
                lse = m + jnp.log(l)
                if mix:
                    la = l0_ref[0, rows, 16 * h:16 * h + 1]
                    lb = l1_ref[0, rows, 16 * h:16 * h + 1]
                    mx = jnp.maximum(jnp.maximum(la, lb), lse)
                    wa = jnp.exp(la - mx)
                    wb = jnp.exp(lb - mx)
                    wc = jnp.exp(lse - mx)
                    den = wa + wb + wc
                    o = ((wa / den) * o0_ref[0, rows, cols].astype(F32)
                         + (wb / den) * o1_ref[0, rows, cols].astype(F32)
                         + (wc / den) * o)
                else:
                    lse_all = jnp.where(lane // 16 == h, lse, lse_all)
                halves.append(o)
            o_ref[0, rows, cols] = jnp.where(lo, halves[0], halves[1]).astype(o_ref.dtype)
        if not mix:
            lse_ref[0, rows, :] = lse_all


def _dilated_pattern(qkv, batch, seq, dil, bias, prev):
    n = batch * seq
    sub = seq // dil
    tq = min(sub, 4 * LANES)
    nsub = tq // LANES
    nblk = sub // tq
    hb = tq // A_HALF
    nhalo = sub // A_HALF

    def main(c):
        return pl.BlockSpec((1, 1, tq, A_W), lambda b, r, i: (b, r, i, c))

    def before(c):
        return pl.BlockSpec((1, 1, A_HALF, A_W),
                            lambda b, r, i: (b, r, jnp.maximum(i * hb - 1, 0), c))

    def after(c):
        return pl.BlockSpec((1, 1, A_HALF, A_W),
                            lambda b, r, i: (b, r, jnp.minimum((i + 1) * hb, nhalo - 1), c))

    o_spec = pl.BlockSpec((1, tq, A_W), lambda b, r, i: (b, i, r))
    l_spec = pl.BlockSpec((1, tq, LANES), lambda b, r, i: (b, i, r))
    in_specs = [main(0), before(1), main(1), after(1), before(2), main(2), after(2),
                pl.BlockSpec((A_HEADS, LANES, 2 * LANES), lambda b, r, i: (0, 0, 0))]
    args = [qkv] * 7 + [bias]
    o_shape = jax.ShapeDtypeStruct((batch, sub, dil * A_W), BF16)
    l_shape = jax.ShapeDtypeStruct((batch, sub, dil * LANES), F32)
    if prev is None:
        out_specs, out_shape = [o_spec, l_spec], [o_shape, l_shape]
    else:
        o0, l0, o1, l1 = prev
        in_specs += [o_spec, l_spec, o_spec, l_spec]
        args += [o0.reshape(o_shape.shape), l0.reshape(l_shape.shape),
                 o1.reshape(o_shape.shape), l1.reshape(l_shape.shape)]
        out_specs, out_shape = o_spec, o_shape
    out = pl.pallas_call(
        functools.partial(_dil_kernel, nsub=nsub, mix=prev is not None),
        grid=(batch, dil, nblk),
        in_specs=in_specs,
        out_specs=out_specs,
        out_shape=out_shape,
        scratch_shapes=[pltpu.VMEM((tq + 2 * A_HALF, A_W), BF16),
                        pltpu.VMEM((tq + 2 * A_HALF, A_W), BF16)],
        compiler_params=_params("parallel", "parallel", "parallel"),
        name=f"dilated_d{dil}",
    )(*args)
    if prev is None:
        return out[0].reshape(n, A_W), out[1].reshape(n, LANES)
    return out.reshape(n, A_W)


def _dilated(views, batch, seq, biases):
    o0, l0 = _dilated_pattern(views[0], batch, seq, A_PATTERNS[0][1], biases[0], None)
    o1, l1 = _dilated_pattern(views[1], batch, seq, A_PATTERNS[1][1], biases[1], None)
    return _dilated_pattern(views[2], batch, seq, A_PATTERNS[2][1], biases[2], (o0, l0, o1, l1))


def _na_kernel(q_ref, k0_ref, k1_ref, k2_ref, v0_ref, v1_ref, v2_ref, bias_ref, o_ref):
    tq = q_ref.shape[1]
    lo, hi = _half_masks(tq)
    zero = jnp.zeros((), BF16)
    for p in range(B_HEADS // 2):
        cols = slice(p * LANES, (p + 1) * LANES)
        qp = q_ref[0, :, cols]
        kk = jnp.concatenate([k0_ref[0, :, cols], k1_ref[0, :, cols], k2_ref[0, :, cols]], axis=0)
        vv = jnp.concatenate([v0_ref[0, :, cols], v1_ref[0, :, cols], v2_ref[0, :, cols]], axis=0)
        halves = []
        for hf in range(2):
            qm = jnp.where(lo if hf == 0 else hi, qp, zero)
            s = _dot_nt(qm, kk) + bias_ref[0, 2 * p + hf]
            o, _, _ = _softmax_pv(s, vv)
            halves.append(o)
        o_ref[0, :, cols] = jnp.where(lo, halves[0], halves[1]).astype(o_ref.dtype)


def _natten(pr, batch, seq, bias):
    n = batch * seq
    tq = NA_QROWS * GRID_W
    nblk = seq // tq
    nkb = NA_KROWS // NA_QROWS
    ncol = pr.shape[1] // A_W
    prv = pr.reshape(batch, seq, pr.shape[1])

    def kv_spec(c, j):
        return pl.BlockSpec((1, tq, A_W),
                            lambda b, i: (b, jnp.clip(i - 1, 0, nblk - nkb) + j, c))

    def variant(i):
        return jnp.where(i == 0, 0, jnp.where(i == nblk - 1, 2, 1))

    in_specs = ([pl.BlockSpec((1, tq, A_W), lambda b, i: (b, i, 3))]
                + [kv_spec(4, j) for j in range(nkb)] + [kv_spec(5, j) for j in range(nkb)]
                + [pl.BlockSpec((1, B_HEADS, tq, nkb * tq), lambda b, i: (variant(i), 0, 0, 0))])
    out = pl.pallas_call(
        _na_kernel,
        grid=(batch, nblk),
        in_specs=in_specs,
        out_specs=pl.BlockSpec((1, tq, A_W), lambda b, i: (b, i, 0)),
        out_shape=jax.ShapeDtypeStruct((batch, seq, A_W), BF16),
        compiler_params=_params("parallel", "parallel"),
        name="natten",
    )(*([prv] * (1 + 2 * nkb) + [bias]))
    return out.reshape(n, A_W)


def _gqa_kernel(qt_ref, k_ref, vt_ref, o_ref, st_ref, e_ref, *, tk):
    tq = qt_ref.shape[2]
    nchunk = k_ref.shape[1] // tk
    heads = qt_ref.shape[1] // HEAD_DIM
    second = pl.program_id(1) % 2 == 1
    zeros = jnp.zeros((HEAD_DIM, tq), BF16)
    cols = []
    for h in range(heads):
        qh = qt_ref[0, h * HEAD_DIM:(h + 1) * HEAD_DIM, :]
        cols.append(jnp.where(second, jnp.concatenate([zeros, qh], axis=0),
                              jnp.concatenate([qh, zeros], axis=0)))
    qst = jnp.concatenate(cols, axis=1)
    width = heads * tq
    nchain = width // MXU_N
    qsts = [qst[:, j * MXU_N:(j + 1) * MXU_N] for j in range(nchain)]

    ones = jnp.ones((GQA_ONES_ROWS, tk), BF16)

    def chunk(c):
        return pl.ds(c * tk if isinstance(c, int) else pl.multiple_of(c * tk, tk), tk)

    def scores(c, slot):
        kc = k_ref[0, chunk(c), :]
        top = []
        for j in range(nchain):
            st = _dot(kc, qsts[j])
            st_ref[slot, j] = st
            top.append(jnp.max(st, axis=0, keepdims=True))
        return tuple(top)

    def values(c, slot, acc, alpha):
        vc = jnp.concatenate([vt_ref[0, :, chunk(c)], ones], axis=0)
        return tuple(acc[j] * alpha[j] + _dot(vc, e_ref[slot, j]) for j in range(nchain))

    def softmax(slot, m, top):
        m_new, alpha = [], []
        for j in range(nchain):
            s = st_ref[slot, j]
            mn = jnp.maximum(m[j], top[j])
            e_ref[slot, j] = jnp.exp2((s - mn).astype(BF16))
            alpha.append(jnp.exp2(m[j] - mn))
            m_new.append(mn)
        return tuple(m_new), tuple(alpha)

    def step(c, carry, ahead=True):
        slot, m, acc, a2, a1, t0, t1 = carry[0] % 4, *carry[1:]
        before = max(c - 2, 0) if isinstance(c, int) else jnp.maximum(c - 2, 0)
        acc = values(before, (slot + 2) % 4, acc, a2)
        t2 = scores(c + 2, (slot + 2) % 4) if ahead else t1
        m, a0 = softmax(slot, m, t0)
        return carry[0] + 1, m, acc, a1, a0, t1, t2

    def body(i, carry):
        state = (0,) + carry
        for u in range(4):
            state = step(4 * i + u, state)
        return state[1:]

    row = lambda v: tuple(jnp.full((1, MXU_N), v, F32) for _ in range(nchain))
    carry = (row(NEG), tuple(jnp.zeros((HEAD_DIM + GQA_ONES_ROWS, MXU_N), F32) for _ in range(nchain)),
             row(1.0), row(1.0), scores(0, 0), scores(1, 1))
    for slot in (2, 3):
        e_ref[slot] = jnp.zeros(e_ref.shape[1:], BF16)
    carry = lax.fori_loop(0, nchunk // 4 - 1, body, carry)
    state = (0,) + carry
    for u in range(4):
        state = step(nchunk - 4 + u, state, ahead=u < 2)
    _, m, acc, a2, a1, _, _ = state
    acc = values(nchunk - 2, 2, acc, a2)
    acc = values(nchunk - 1, 3, acc, a1)
    o = jnp.concatenate([a[:HEAD_DIM] / a[HEAD_DIM:HEAD_DIM + 1] for a in acc], axis=1)
    ot = jnp.concatenate([o[:, h * tq:(h + 1) * tq] for h in range(heads)], axis=0)
    o_ref[0] = ot.T.astype(o_ref.dtype)


def _gqa(qt, k, vt, batch, seq, tq=256, tk=512):
    n = batch * seq
    gw = (C_Q_HEADS // C_KV_HEADS) * HEAD_DIM
    nchain = (C_Q_HEADS // C_KV_HEADS) * tq // MXU_N
    assert seq % (4 * tk) == 0
    out = pl.pallas_call(
        functools.partial(_gqa_kernel, tk=tk),
        grid=(batch, C_KV_HEADS, seq // tq),
        in_specs=[pl.BlockSpec((1, gw, tq), lambda b, g, i: (b, g, i)),
                  pl.BlockSpec((1, seq, LANES), lambda b, g, i: (b, 0, g // 2)),
                  pl.BlockSpec((1, HEAD_DIM, seq), lambda b, g, i: (b, g, 0))],
        out_specs=pl.BlockSpec((1, tq, gw), lambda b, g, i: (b, i, g)),
        out_shape=jax.ShapeDtypeStruct((batch, seq, C_Q_HEADS * HEAD_DIM), BF16),
        scratch_shapes=[pltpu.VMEM((4, nchain, tk, MXU_N), F32),
                        pltpu.VMEM((4, nchain, tk, MXU_N), BF16)],
        compiler_params=_params("parallel", "parallel", "parallel"),
        name="gqa",
    )(qt, k.reshape(batch, seq, -1), vt)
    return out.reshape(n, C_Q_HEADS * HEAD_DIM)


def _post_kernel(*refs, route):
    if route:
        (x_ref, a1_ref, a2_ref, wo1_ref, wo2_ref, gxa_ref, wq_ref, kv_ref, wox_ref, gffn_ref,
         wr_ref, x2_ref, hn_ref, info_ref) = refs
    else:
        (x_ref, a1_ref, a2_ref, wo1_ref, wo2_ref, gxa_ref, wq_ref, kv_ref, wox_ref, gffn_ref,
         x2_ref, hn_ref) = refs
    x1 = x_ref[...] + _dot(a1_ref[...], wo1_ref[...]) + _dot(a2_ref[...], wo2_ref[...])
    tm = x1.shape[0]
    h = _rms(x1, gxa_ref[...]).astype(BF16)
    q = _dot(h, wq_ref[...]).astype(BF16)
    lo, hi = _half_masks(tm)
    zero = jnp.zeros((), BF16)
    outs = []
    for p in range(X_HEADS // 2):
        qp = q[:, p * LANES:(p + 1) * LANES]
        kk = kv_ref[0, :, p * LANES:(p + 1) * LANES]
        vv = kv_ref[0, :, X_W + p * LANES:X_W + (p + 1) * LANES]
        halves = []
        for hf in range(2):
            qm = jnp.where(lo if hf == 0 else hi, qp, zero)
            o, _, _ = _softmax_pv(_dot_nt(qm, kk), vv)
            halves.append(o)
        outs.append(jnp.where(lo, halves[0], halves[1]).astype(BF16))
    x2 = x1 + _dot(jnp.concatenate(outs, axis=1), wox_ref[...])
    x2_ref[...] = x2
    hf32 = _rms(x2, gffn_ref[...])
    hn_ref[...] = hf32.astype(hn_ref.dtype)
    if route:
        lane = lax.broadcasted_iota(jnp.int32, (tm, LANES), 1).astype(F32)
        lg = jnp.full((tm, LANES), NEG, F32)
        for ex in range(N_EXPERTS):
            le = jnp.sum(hf32 * wr_ref[ex:ex + 1, :], axis=1, keepdims=True)
            lg = jnp.where(lane == float(ex), le, lg)
        m1 = jnp.max(lg, axis=1, keepdims=True)
        i1 = jnp.min(jnp.where(lg == m1, lane, float(LANES)), axis=1, keepdims=True)
        lg2 = jnp.where(lane == i1, NEG, lg)
        m2 = jnp.max(lg2, axis=1, keepdims=True)
        i2 = jnp.min(jnp.where(lg2 == m2, lane, float(LANES)), axis=1, keepdims=True)
        e = jnp.exp(m2 - m1)
        den = 1.0 + e
        info_ref[...] = (jnp.where(lane == 0.0, i1, 0.0) + jnp.where(lane == 1.0, i2, 0.0)
                         + jnp.where(lane == 2.0, 1.0 / den, 0.0) + jnp.where(lane == 3.0, e / den, 0.0))


def _post(x, a1, a2, c1, c2, wo1, wo2, gxa, wq, kvm, wox, gffn, wr, seq, tm):
    n, d = x.shape
    spb = seq // tm
    row = lambda i: (i, 0)
    const = lambda i: (0, 0)
    in_specs = [pl.BlockSpec((tm, d), row),
                pl.BlockSpec((tm, A_W), lambda i: (i, c1)),
                pl.BlockSpec((tm, A_W), lambda i: (i, c2)),
                pl.BlockSpec((A_W, d), const),
                pl.BlockSpec((A_W, d), const),
                pl.BlockSpec((1, d), const),
                pl.BlockSpec((d, X_W), const),
                pl.BlockSpec((1, N_MEM, 2 * X_W), lambda i: (i // spb, 0, 0)),
                pl.BlockSpec((X_W, d), const),
                pl.BlockSpec((1, d), const)]
    args = [x, a1, a2, wo1, wo2, gxa.reshape(1, d), wq, kvm, wox, gffn.reshape(1, d)]
    out_specs = [pl.BlockSpec((tm, d), row), pl.BlockSpec((tm, d), row)]
    out_shape = [jax.ShapeDtypeStruct((n, d), F32),
                 jax.ShapeDtypeStruct((n, d), BF16 if wr is None else F32)]
    if wr is not None:
        in_specs.append(pl.BlockSpec((N_EXPERTS, d), const))
        args.append(wr)
        out_specs.append(pl.BlockSpec((tm, LANES), row))
        out_shape.append(jax.ShapeDtypeStruct((n, LANES), F32))
    return pl.pallas_call(
        functools.partial(_post_kernel, route=wr is not None),
        grid=(n // tm,),
        in_specs=in_specs,
        out_specs=out_specs,
        out_shape=out_shape,
        compiler_params=_params("parallel"),
        name="post_route" if wr is not None else "post",
    )(*args)


def _swiglu_chunk(h, wg, wu, wd):
    g = _dot(h, wg)
    u = _dot(h, wu)
    a = (g / (1.0 + jnp.exp(-g))) * u
    return _dot(a.astype(BF16), wd)


def _ffn_kernel(x_ref, h_ref, wg_ref, wu_ref, wd_ref, o_ref):
    f = pl.program_id(1)
    y = _swiglu_chunk(h_ref[...], wg_ref[...], wu_ref[...], wd_ref[...])

    @pl.when(f == 0)
    def _():
        o_ref[...] = x_ref[...] + y

    @pl.when(f > 0)
    def _():
        o_ref[...] += y


def _ffn(x, hn, wg, wu, wd, tm, tf):
    n, d = x.shape
    ff = wg.shape[1]
    return pl.pallas_call(
        _ffn_kernel,
        grid=(n // tm, ff // tf),
        in_specs=[pl.BlockSpec((tm, d), lambda i, f: (i, 0)),
                  pl.BlockSpec((tm, d), lambda i, f: (i, 0)),
                  pl.BlockSpec((d, tf), lambda i, f: (0, f)),
                  pl.BlockSpec((d, tf), lambda i, f: (0, f)),
                  pl.BlockSpec((tf, d), lambda i, f: (f, 0))],
        out_specs=pl.BlockSpec((tm, d), lambda i, f: (i, 0)),
        out_shape=jax.ShapeDtypeStruct((n, d), F32),
        compiler_params=_params("parallel", "arbitrary"),
        name="ffn",
    )(x, hn, wg, wu, wd)


def _route(info, tmg):
    n = info.shape[0]
    ea = info[:, 0:2].astype(jnp.int32).reshape(-1)
    onehot = (ea[:, None] == jnp.arange(N_EXPERTS, dtype=jnp.int32)[None, :]).astype(jnp.int32)
    csum = jnp.cumsum(onehot, axis=0)
    counts = csum[-1]
    padded = (counts + tmg - 1) // tmg * tmg
    ends = jnp.cumsum(padded)
    starts = ends - padded
    dest = jnp.sum(onehot * (csum - 1 + starts[None, :]), axis=1)
    r_max = 2 * n + N_EXPERTS * tmg
    token = jnp.arange(2 * n, dtype=jnp.int32) // 2
    src = jnp.zeros((r_max,), jnp.int32).at[dest].set(token, unique_indices=True)
    tile_start = jnp.arange(r_max // tmg, dtype=jnp.int32) * tmg
    tile_e = jnp.sum((tile_start[:, None] >= ends[None, :]).astype(jnp.int32), axis=1)
    tile_e = jnp.minimum(tile_e, N_EXPERTS - 1)
    tile_ok = (tile_start < ends[-1]).astype(jnp.int32)
    return src, dest, tile_e, tile_ok


def _row_copy(src_ref, row, dst_ref, r, sem):
    return pltpu.make_async_copy(src_ref.at[pl.ds(row, 1)], dst_ref.at[pl.ds(r, 1)], sem)


def _gather_kernel(idx_ref, src_ref, o_ref, buf, sem):
    i = pl.program_id(0)
    ntile = pl.num_programs(0) - 1
    tg = o_ref.shape[0]

    @pl.when(i < ntile)
    def _():
        slot = i % 2

        def start(pair, c):
            for priority in range(2):
                r = 2 * pair + priority
                _row_copy(src_ref, idx_ref[0, 0, r], buf.at[slot], r, sem.at[slot]).start(priority=priority)
            return c

        lax.fori_loop(0, tg // 2, start, 0, unroll=4)

    @pl.when(i > 0)
    def _():
        slot = (i + 1) % 2

        def wait(r, c):
            _row_copy(src_ref, 0, buf.at[slot], 0, sem.at[slot]).wait()
            return c

        lax.fori_loop(0, tg, wait, 0, unroll=8)
        o_ref[...] = buf[slot]


def _gather_rows(src, idx, tg):
    n, d = src.shape
    r = idx.shape[0]
    ntile = r // tg
    return pl.pallas_call(
        _gather_kernel,
        grid=(ntile + 1,),
        in_specs=[pl.BlockSpec((1, 1, tg), lambda i: (jnp.minimum(i, ntile - 1), 0, 0),
                               memory_space=pltpu.SMEM),
                  pl.BlockSpec(memory_space=pl.ANY)],
        out_specs=pl.BlockSpec((tg, d), lambda i: (jnp.maximum(i - 1, 0), 0)),
        out_shape=jax.ShapeDtypeStruct((r, d), src.dtype),
        scratch_shapes=[pltpu.VMEM((2, tg, d), src.dtype), pltpu.SemaphoreType.DMA((2,))],
        compiler_params=_params("arbitrary"),
        name="moe_gather",
    )(idx.reshape(ntile, 1, tg), src)


def _expert_kernel(te_ref, ok_ref, x_ref, wg_ref, wu_ref, wd_ref, o_ref, hb_ref):
    t = pl.program_id(0)
    f = pl.program_id(1)
    ok = ok_ref[t] == 1

    @pl.when(jnp.logical_and(ok, f == 0))
    def _():
        hb_ref[...] = x_ref[...].astype(BF16)
        o_ref[...] = _swiglu_chunk(hb_ref[...], wg_ref[0], wu_ref[0], wd_ref[0])

    @pl.when(jnp.logical_and(ok, f > 0))
    def _():
        o_ref[...] += _swiglu_chunk(hb_ref[...], wg_ref[0], wu_ref[0], wd_ref[0])

    @pl.when(jnp.logical_and(jnp.logical_not(ok), f == 0))
    def _():
        o_ref[...] = jnp.zeros_like(o_ref)


def _expert_ffn(xs, tile_e, tile_ok, wg, wu, wd, tmg, tf):
    r, d = xs.shape
    ff = wg.shape[2]
    grid_spec = pltpu.PrefetchScalarGridSpec(
        num_scalar_prefetch=2,
        grid=(r // tmg, ff // tf),
        in_specs=[pl.BlockSpec((tmg, d), lambda t, f, te, ok: (t, 0)),
                  pl.BlockSpec((1, d, tf), lambda t, f, te, ok: (te[t], 0, f * ok[t])),
                  pl.BlockSpec((1, d, tf), lambda t, f, te, ok: (te[t], 0, f * ok[t])),
                  pl.BlockSpec((1, tf, d), lambda t, f, te, ok: (te[t], f * ok[t], 0))],
        out_specs=pl.BlockSpec((tmg, d), lambda t, f, te, ok: (t, 0)),
        scratch_shapes=[pltpu.VMEM((tmg, d), BF16)],
    )
    return pl.pallas_call(
        _expert_kernel,
        grid_spec=grid_spec,
        out_shape=jax.ShapeDtypeStruct((r, d), F32),
        compiler_params=_params("arbitrary", "arbitrary"),
        name="moe_experts",
    )(tile_e, tile_ok, xs, wg, wu, wd)


def _combine_kernel(*refs, norm):
    if norm:
        d_ref, x_ref, info_ref, ys_ref, gain_ref, o_ref, buf, sem = refs
    else:
        (d_ref, x_ref, info_ref, ys_ref, o_ref, buf, sem), gain_ref = refs, None
    tc = o_ref.shape[0]

    def copies(r):
        return (_row_copy(ys_ref, d_ref[0, 0, 2 * r], buf.at[0], r, sem),
                _row_copy(ys_ref, d_ref[0, 0, 2 * r + 1], buf.at[1], r, sem))

    def start(r, c):
        for priority, cp in enumerate(copies(r)):
            cp.start(priority=priority)
        return c

    def wait(r, c):
        for cp in copies(r):
            cp.wait()
        return c

    lax.fori_loop(0, tc, start, 0, unroll=4)
    lax.fori_loop(0, tc, wait, 0, unroll=4)
    g1 = info_ref[:, 2:3]
    g2 = info_ref[:, 3:4]
    o = x_ref[...] + (g1 * buf[0] + g2 * buf[1])
    o_ref[...] = o if gain_ref is None else _rms(o, gain_ref[...])


def _combine(x, info, dest, ys, gain, tc):
    n, d = x.shape
    in_specs = [pl.BlockSpec((1, 1, 2 * tc), lambda i: (i, 0, 0), memory_space=pltpu.SMEM),
                pl.BlockSpec((tc, d), lambda i: (i, 0)),
                pl.BlockSpec((tc, LANES), lambda i: (i, 0)),
                pl.BlockSpec(memory_space=pl.ANY)]
    args = [dest.reshape(n // tc, 1, 2 * tc), x, info, ys]
    if gain is not None:
        in_specs.append(pl.BlockSpec((1, d), lambda i: (0, 0)))
        args.append(gain.reshape(1, d))
    return pl.pallas_call(
        functools.partial(_combine_kernel, norm=gain is not None),
        grid=(n // tc,),
        in_specs=in_specs,
        out_specs=pl.BlockSpec((tc, d), lambda i: (i, 0)),
        out_shape=jax.ShapeDtypeStruct((n, d), F32),
        scratch_shapes=[pltpu.VMEM((2, tc, d), F32), pltpu.SemaphoreType.DMA(())],
        compiler_params=_params("arbitrary"),
        name="moe_combine",
    )(*args)


def _moe(x2, hf, info, wg, wu, wd, gain, tmg, tf, tg, tc):
    src, dest, tile_e, tile_ok = _route(info, tmg)
    xs = _gather_rows(hf, src, tg)
    ys = _expert_ffn(xs, tile_e, tile_ok, wg, wu, wd, tmg, tf)
    return _combine(x2, info, dest, ys, gain, tc)


def _final_norm_kernel(x_ref, g_ref, o_ref):
    o_ref[...] = _rms(x_ref[...], g_ref[...])


def _final_norm(x, g, tm):
    n, d = x.shape
    return pl.pallas_call(
        _final_norm_kernel,
        grid=(n // tm,),
        in_specs=[pl.BlockSpec((tm, d), lambda i: (i, 0)), pl.BlockSpec((1, d), lambda i: (0, 0))],
        out_specs=pl.BlockSpec((tm, d), lambda i: (i, 0)),
        out_shape=jax.ShapeDtypeStruct((n, d), F32),
        compiler_params=_params("parallel"),
        name="final_norm",
    )(x, g.reshape(1, d))


def _t5_bucket_np(rel):
    nb = N_BUCKETS // 2
    max_exact = nb // 2
    ret = np.where(rel > 0, nb, 0)
    n = np.abs(rel)
    large = max_exact + (np.log(np.maximum(n, 1).astype(np.float32) / max_exact)
                         / math.log(MAX_DISTANCE / max_exact) * (nb - max_exact)).astype(np.int32)
    large = np.minimum(large, nb - 1)
    return ret + np.where(n < max_exact, n, large)


def _dilated_bias(rel_table, dil):
    off = np.arange(2 * LANES)[None, :] - A_HALF - np.arange(LANES)[:, None]
    bucket = _t5_bucket_np(off * dil)
    sel = (bucket[:, :, None] == np.arange(N_BUCKETS)).astype(np.float32)
    bias = jnp.einsum('qkb,bh->hqk', sel, rel_table.astype(F32), precision=lax.Precision.HIGHEST)
    return jnp.where(jnp.asarray(np.abs(off) <= A_HALF)[None], bias, NEG)


def _natten_bias(rpb):
    rows = 4 * NA_KROWS
    c = np.arange(GRID_W)
    cs = np.clip(c - NA_KW // 2, 0, GRID_W - NA_KW)
    col_ok = (c[None, :] >= cs[:, None]) & (c[None, :] < cs[:, None] + NA_KW)
    dc = np.clip(c[None, :] - c[:, None], -(NA_KW - 1), NA_KW - 1) + NA_KW - 1
    col_sel = (dc[:, :, None] == np.arange(2 * NA_KW - 1)).astype(np.float32)
    row_sel, row_ok = [], []
    for r0 in (0, 2 * NA_QROWS, rows - NA_QROWS):
        ks = int(np.clip(r0 - NA_KH // 2, 0, rows - NA_KROWS))
        qr = r0 + np.arange(NA_QROWS)
        kr = ks + np.arange(NA_KROWS)
        rs = np.clip(qr - NA_KH // 2, 0, rows - NA_KH)
        row_ok.append((kr[None, :] >= rs[:, None]) & (kr[None, :] < rs[:, None] + NA_KH))
        dr = kr[None, :] - qr[:, None] + NA_KH - 1
        row_sel.append((dr[:, :, None] == np.arange(2 * NA_KH - 1)).astype(np.float32))
    row_sel = np.stack(row_sel)
    ok = np.stack(row_ok)[:, :, None, :, None] & col_ok[None, None, :, None, :]
    by_col = jnp.einsum('hab,xyb->haxy', rpb.astype(F32), col_sel, precision=lax.Precision.HIGHEST)
    table = jnp.einsum('vqka,haxy->vhqxky', row_sel, by_col, precision=lax.Precision.HIGHEST)
    table = jnp.where(jnp.asarray(ok)[:, None], table, NEG)
    tq = NA_QROWS * GRID_W
    return table.reshape(len(row_ok), rpb.shape[0], tq, NA_KROWS * GRID_W)


def _rope_tables(seq):
    t = jnp.arange(seq)
    half = HEAD_DIM // 2
    freqs = ROPE_THETA ** (-jnp.arange(0, half, 2, dtype=F32) / half)
    ang_r = (t // GRID_W).astype(F32)[:, None] * freqs[None, :]
    ang_c = (t % GRID_W).astype(F32)[:, None] * freqs[None, :]
    cos_h = jnp.concatenate([jnp.cos(ang_r)] * 2 + [jnp.cos(ang_c)] * 2, axis=1)
    sin_h = jnp.concatenate([-jnp.sin(ang_r), jnp.sin(ang_r), -jnp.sin(ang_c), jnp.sin(ang_c)], axis=1)
    return jnp.concatenate([cos_h, cos_h], axis=1), jnp.concatenate([sin_h, sin_h], axis=1)


def _prepare(p):
    w = {}
    ev_in = p['ev_w_in']
    scale = np.ones((ev_in.shape[-1],), np.float32)
    scale[0:A_W] = Q_SCALE
    scale[3 * A_W:4 * A_W] = Q_SCALE
    w['ev_w_in'] = (ev_in * scale).astype(BF16)
    w['ev_w_out'] = p['ev_w_out'].astype(BF16)
    w['ev_w_gate'] = p['ev_w_gate'].astype(BF16)
    w['ev_w_up'] = p['ev_w_up'].astype(BF16)
    w['ev_w_down'] = p['ev_w_down'].astype(BF16)
    w['od_w_in'] = p['od_w_in'].astype(BF16)
    w['od_w_out'] = p['od_w_out'].astype(BF16)
    w['od_q_norm'] = jnp.concatenate([p['od_q_norm']] * 2, axis=-1)[:, None, :]
    w['od_k_norm'] = jnp.concatenate([p['od_k_norm']] * 2, axis=-1)[:, None, :]
    w['od_router'] = jnp.swapaxes(p['od_router'], 1, 2)
    w['od_moe_gate'] = p['od_moe_gate'].astype(BF16)
    w['od_moe_up'] = p['od_moe_up'].astype(BF16)
    w['od_moe_down'] = p['od_moe_down'].astype(BF16)
    w['xa_w_q'] = (p['xa_w_q'] * Q_SCALE).astype(BF16)
    w['xa_w_kv'] = p['xa_w_kv'].astype(BF16)
    w['xa_w_out'] = p['xa_w_out'].astype(BF16)
    w['dil_bias'] = [_dilated_bias(p['rel_table'], dil) for _, dil in A_PATTERNS]
    w['na_bias'] = [_natten_bias(p['ev_na_rpb'][i]) for i in range(p['ev_na_rpb'].shape[0])]
    return w


def _trunk(x, mem, p, w):
    batch, seq, d = x.shape
    n = batch * seq
    xf = x.reshape(n, d)
    memf = mem.reshape(batch * N_MEM, d)
    cos, sin = _rope_tables(seq)
    depth = p['xa_norm'].shape[0]
    for layer in range(depth):
        i = layer // 2
        if layer % 2 == 0:
            pr, v4, v16 = _proj_even(xf, p['ev_norm_mix'][i], w['ev_w_in'][i], seq, tm=512)
            a1 = _dilated([pr.reshape(batch, 1, seq, -1), v4, v16], batch, seq, w['dil_bias'])
            a2 = _natten(pr, batch, seq, w['na_bias'][i])
            c1 = c2 = 0
            wo = w['ev_w_out'][i]
            wr = None
        else:
            q, kk, vv = _proj_odd(xf, p['od_norm_mix'][i], w['od_w_in'][i], w['od_q_norm'][i],
                                  w['od_k_norm'][i], cos, sin, seq, tm=512)
            a1 = a2 = _gqa(q, kk, vv, batch, seq)
            c1, c2 = 0, 1
            wo = w['od_w_out'][i]
            wr = w['od_router'][i]
        kvm = _proj(memf, p['xa_mem_norm'][layer], w['xa_w_kv'][layer], tm=512)
        kvm = kvm.reshape(batch, N_MEM, 2 * X_W)
        gffn = p['ev_norm_ffn'][i] if layer % 2 == 0 else p['od_norm_ffn'][i]
        outs = _post(xf, a1, a2, c1, c2, wo[:A_W], wo[A_W:], p['xa_norm'][layer], w['xa_w_q'][layer],
                     kvm, w['xa_w_out'][layer], gffn, wr, seq, tm=512)
        if layer % 2 == 0:
            x2, hn = outs
            xf = _ffn(x2, hn, w['ev_w_gate'][i], w['ev_w_up'][i], w['ev_w_down'][i], tm=512, tf=1408)
        else:
            x2, hf, info = outs
            gain = p['final_norm'] if layer == depth - 1 else None
            xf = _moe(x2, hf, info, w['od_moe_gate'][i], w['od_moe_up'][i], w['od_moe_down'][i],
                      gain, tmg=1024, tf=512, tg=512, tc=256)
    if depth % 2 == 1:
        xf = _final_norm(xf, p['final_norm'], tm=1024)
    return xf.reshape(batch, seq, d)


def kernel(x_prompt, x_sample, mem_prompt, mem_sample, rel_table, ev_norm_mix, ev_w_in, ev_na_rpb, ev_w_out, ev_norm_ffn, ev_w_gate, ev_w_up, ev_w_down, od_norm_mix, od_w_in, od_q_norm, od_k_norm, od_w_out, od_norm_ffn, od_router, od_moe_gate, od_moe_up, od_moe_down, xa_norm, xa_mem_norm, xa_w_q, xa_w_kv, xa_w_out, final_norm):
    p = dict(rel_table=rel_table,
             ev_norm_mix=ev_norm_mix, ev_w_in=ev_w_in, ev_na_rpb=ev_na_rpb, ev_w_out=ev_w_out,
             ev_norm_ffn=ev_norm_ffn, ev_w_gate=ev_w_gate, ev_w_up=ev_w_up, ev_w_down=ev_w_down,
             od_norm_mix=od_norm_mix, od_w_in=od_w_in, od_q_norm=od_q_norm, od_k_norm=od_k_norm,
             od_w_out=od_w_out, od_norm_ffn=od_norm_ffn, od_router=od_router,
             od_moe_gate=od_moe_gate, od_moe_up=od_moe_up, od_moe_down=od_moe_down,
             xa_norm=xa_norm, xa_mem_norm=xa_mem_norm, xa_w_q=xa_w_q, xa_w_kv=xa_w_kv,
             xa_w_out=xa_w_out, final_norm=final_norm)
    w = _prepare(p)
    return (_trunk(x_prompt, mem_prompt, p, w), _trunk(x_sample, mem_sample, p, w))
```

```python
import functools
import math

import numpy as np
import jax
import jax.numpy as jnp
from jax import lax
from jax.experimental import pallas as pl
from jax.experimental.pallas import tpu as pltpu

F32 = jnp.float32
BF16 = jnp.bfloat16

D_MODEL = 1024
HEAD_DIM = 64
GRID_W = 64
N_MEM = 256
A_HEADS = 8
A_PATTERNS = ((128, 1), (512, 4), (2048, 16))
A_HALF = 64
B_HEADS = 8
NA_KH = 8
NA_KW = 16
C_Q_HEADS = 16
C_KV_HEADS = 4
ROPE_THETA = 10000.0
N_BUCKETS = 32
MAX_DISTANCE = 1024
X_HEADS = 4
N_EXPERTS = 8
EPS = 1e-6
NEG = -1e30
A_W = A_HEADS * HEAD_DIM
X_W = X_HEADS * HEAD_DIM
Q_SCALE = HEAD_DIM ** -0.5
QK_SCALE = Q_SCALE * math.log2(math.e)

LANES = 128
MXU_N = 256
GQA_ONES_ROWS = 16
VMEM_LIMIT = 48 * 1024 * 1024

NA_QROWS = 4
NA_KROWS = 12


def _params(*sem):
    return pltpu.CompilerParams(dimension_semantics=sem, vmem_limit_bytes=VMEM_LIMIT)


def _rms(x, g):
    ms = jnp.mean(x * x, axis=-1, keepdims=True)
    return x * lax.rsqrt(ms + EPS) * g


def _dot(a, b):
    return jnp.dot(a, b, preferred_element_type=F32)


def _dot_nt(a, b):
    return lax.dot_general(a, b, (((1,), (1,)), ((), ())), preferred_element_type=F32)


def _half_masks(rows):
    lane = lax.broadcasted_iota(jnp.int32, (rows, LANES), 1)
    lo = lane < HEAD_DIM
    return lo, jnp.logical_not(lo)


def _pack_bf16_pairs(x):
    half = x.shape[1] // 2
    bits = lax.bitcast_convert_type(x.astype(BF16).astype(F32), jnp.uint32)
    return (bits[:, half:] & jnp.uint32(0xFFFF0000)) | (bits[:, :half] >> 16)


def _unpack_bf16_pairs(p):
    lo = lax.bitcast_convert_type(p << 16, F32).astype(BF16)
    hi = lax.bitcast_convert_type(p & jnp.uint32(0xFFFF0000), F32).astype(BF16)
    return jnp.concatenate([lo, hi], axis=1)


def _softmax_pv(s, v):
    m = jnp.max(s, axis=1, keepdims=True)
    e = jnp.exp(s - m)
    l = jnp.sum(e, axis=1, keepdims=True)
    o = _dot(e.astype(BF16), v) / l
    return o, m, l


def _proj_kernel(x_ref, g_ref, w_ref, o_ref):
    h = _rms(x_ref[...], g_ref[...]).astype(BF16)
    o_ref[...] = _dot(h, w_ref[...]).astype(o_ref.dtype)


def _proj(x, g, w, tm):
    n, d = x.shape
    nout = w.shape[1]
    return pl.pallas_call(
        _proj_kernel,
        grid=(n // tm,),
        in_specs=[pl.BlockSpec((tm, d), lambda i: (i, 0)),
                  pl.BlockSpec((1, d), lambda i: (0, 0)),
                  pl.BlockSpec((d, nout), lambda i: (0, 0))],
        out_specs=pl.BlockSpec((tm, nout), lambda i: (i, 0)),
        out_shape=jax.ShapeDtypeStruct((n, nout), BF16),
        compiler_params=_params("parallel"),
        name="proj",
    )(x, g.reshape(1, d), w)


def _proj_even_kernel(x_ref, g_ref, w_ref, o_ref, *rest):
    views, y_ref = rest[:-1], rest[-1]
    h = _rms(x_ref[...], g_ref[...]).astype(BF16)
    y = _dot(h, w_ref[...])
    o_ref[...] = y.astype(o_ref.dtype)
    tm = y.shape[0]
    ncol = y_ref.shape[0]
    for c in range(ncol):
        y_ref[c] = y[:, c * LANES:(c + 1) * LANES]
    for v_ref in views:
        dil = v_ref.shape[1]
        for r in range(dil):
            for c in range(ncol):
                rows = y_ref[c, pl.ds(r, tm // dil, stride=dil), :]
                v_ref[0, r, :, c * LANES:(c + 1) * LANES] = rows.astype(v_ref.dtype)


def _proj_even(x, g, w, seq, tm):
    n, d = x.shape
    nout = w.shape[1]
    spb = seq // tm
    batch = n // seq
    dils = [dil for _, dil in A_PATTERNS if dil > 1]
    aw = 3 * A_W
    return pl.pallas_call(
        _proj_even_kernel,
        grid=(n // tm,),
        in_specs=[pl.BlockSpec((tm, d), lambda i: (i, 0)),
                  pl.BlockSpec((1, d), lambda i: (0, 0)),
                  pl.BlockSpec((d, nout), lambda i: (0, 0))],
        out_specs=[pl.BlockSpec((tm, nout), lambda i: (i, 0))]
        + [pl.BlockSpec((1, dil, tm // dil, aw), lambda i: (i // spb, 0, i % spb, 0)) for dil in dils],
        out_shape=[jax.ShapeDtypeStruct((n, nout), BF16)]
        + [jax.ShapeDtypeStruct((batch, dil, seq // dil, aw), BF16) for dil in dils],
        scratch_shapes=[pltpu.VMEM((aw // LANES, tm, LANES), F32)],
        compiler_params=_params("parallel"),
        name="proj_even",
    )(x, g.reshape(1, d), w)


def _proj_odd_kernel(x_ref, g_ref, w_ref, qg_ref, kg_ref, cos_ref, sin_ref, q_ref, k_ref, v_ref):
    h = _rms(x_ref[...], g_ref[...]).astype(BF16)
    y = _dot(h, w_ref[...])
    tm = y.shape[0]
    cos = cos_ref[...]
    sin = sin_ref[...]
    lane = lax.broadcasted_iota(jnp.int32, (tm, LANES), 1)
    lo = lane < HEAD_DIM
    first = (lane & 31) < 16

    def norm_rope(c, gain, scale):
        ss = c * c
        s_lo = jnp.sum(jnp.where(lo, ss, 0.0), axis=1, keepdims=True)
        s_hi = jnp.sum(jnp.where(lo, 0.0, ss), axis=1, keepdims=True)
        ms = jnp.where(lo, s_lo, s_hi) * (1.0 / HEAD_DIM)
        c = c * lax.rsqrt(ms + EPS) * gain
        partner = jnp.where(first, pltpu.roll(c, LANES - 16, 1), pltpu.roll(c, 16, 1))
        return (c * cos + partner * sin) * scale

    nq = q_ref.shape[1] // LANES
    nk = k_ref.shape[1] // LANES
    for j in range(nq):
        c = y[:, j * LANES:(j + 1) * LANES]
        q_ref[0, j * LANES:(j + 1) * LANES, :] = norm_rope(c, qg_ref[...], QK_SCALE).T.astype(BF16)
    for j in range(nk):
        c = y[:, (nq + j) * LANES:(nq + j + 1) * LANES]
        k_ref[:, j * LANES:(j + 1) * LANES] = norm_rope(c, kg_ref[...], 1.0).astype(BF16)
    for j in range(nk):
        c = y[:, (nq + nk + j) * LANES:(nq + nk + j + 1) * LANES]
        v_ref[0, j * LANES:(j + 1) * LANES, :] = c.T.astype(BF16)


def _proj_odd(x, g, w, qg, kg, cos, sin, seq, tm):
    n, d = x.shape
    nq = C_Q_HEADS * HEAD_DIM
    nk = C_KV_HEADS * HEAD_DIM
    spb = seq // tm
    batch = n // seq
    return pl.pallas_call(
        _proj_odd_kernel,
        grid=(n // tm,),
        in_specs=[pl.BlockSpec((tm, d), lambda i: (i, 0)),
                  pl.BlockSpec((1, d), lambda i: (0, 0)),
                  pl.BlockSpec((d, nq + 2 * nk), lambda i: (0, 0)),
                  pl.BlockSpec((1, LANES), lambda i: (0, 0)),
                  pl.BlockSpec((1, LANES), lambda i: (0, 0)),
                  pl.BlockSpec((tm, LANES), lambda i: (i % spb, 0)),
                  pl.BlockSpec((tm, LANES), lambda i: (i % spb, 0))],
        out_specs=[pl.BlockSpec((1, nq, tm), lambda i: (i // spb, 0, i % spb)),
                   pl.BlockSpec((tm, nk), lambda i: (i, 0)),
                   pl.BlockSpec((1, nk, tm), lambda i: (i // spb, 0, i % spb))],
        out_shape=[jax.ShapeDtypeStruct((batch, nq, seq), BF16),
                   jax.ShapeDtypeStruct((n, nk), BF16),
                   jax.ShapeDtypeStruct((batch, nk, seq), BF16)],
        compiler_params=_params("parallel"),
        name="proj_odd",
    )(x, g.reshape(1, d), w, qg, kg, cos, sin)


def _dil_kernel(*refs, nsub, mix):
    if mix:
        (q_ref, kp_ref, kc_ref, kn_ref, vp_ref, vc_ref, vn_ref, bias_ref,
         o0_ref, l0_ref, o1_ref, l1_ref, o_ref, kwin, vwin) = refs
    else:
        (q_ref, kp_ref, kc_ref, kn_ref, vp_ref, vc_ref, vn_ref, bias_ref,
         o_ref, lse_ref, kwin, vwin) = refs
    tq = q_ref.shape[2]
    i = pl.program_id(2)
    nblk = pl.num_programs(2)
    kwin[0:A_HALF] = kp_ref[0, 0]
    kwin[A_HALF:A_HALF + tq] = kc_ref[0, 0]
    kwin[A_HALF + tq:] = kn_ref[0, 0]
    vwin[0:A_HALF] = vp_ref[0, 0]
    vwin[A_HALF:A_HALF + tq] = vc_ref[0, 0]
    vwin[A_HALF + tq:] = vn_ref[0, 0]
    lo, hi = _half_masks(LANES)
    lane = lax.broadcasted_iota(jnp.int32, (LANES, LANES), 1)
    kj = lax.broadcasted_iota(jnp.int32, (LANES, 2 * LANES), 1)
    zero = jnp.zeros((), BF16)
    for a in range(nsub):
        rows = slice(a * LANES, (a + 1) * LANES)
        valid = None
        if a == 0:
            valid = kj >= jnp.where(i == 0, A_HALF, 0)
        if a == nsub - 1:
            v2 = kj < jnp.where(i == nblk - 1, LANES + A_HALF, 2 * LANES)
            valid = v2 if valid is None else jnp.logical_and(valid, v2)
        lse_all = jnp.zeros((LANES, LANES), F32)
        for p in range(A_HEADS // 2):
            cols = slice(p * LANES, (p + 1) * LANES)
            qp = q_ref[0, 0, rows, cols]
            kk = kwin[a * LANES:(a + 2) * LANES, cols]
            vv = vwin[a * LANES:(a + 2) * LANES, cols]
            halves = []
            for hf in range(2):
                h = 2 * p + hf
                qm = jnp.where(lo if hf == 0 else hi, qp, zero)
                s = _dot_nt(qm, kk) + bias_ref[h]
                if valid is not None:
                    s = jnp.where(valid, s, NEG)
                o, m, l = _softmax_pv(s, vv)
                lse = m + jnp.log(l)
                if mix:
                    la = l0_ref[0, rows, 16 * h:16 * h + 1]
                    lb = l1_ref[0, rows, 16 * h:16 * h + 1]
                    mx = jnp.maximum(jnp.maximum(la, lb), lse)
                    wa = jnp.exp(la - mx)
                    wb = jnp.exp(lb - mx)
                    wc = jnp.exp(lse - mx)
                    den = wa + wb + wc
                    o = ((wa / den) * o0_ref[0, rows, cols].astype(F32)
                         + (wb / den) * o1_ref[0, rows, cols].astype(F32)
                         + (wc / den) * o)
                else:
                    lse_all = jnp.where(lane // 16 == h, lse, lse_all)
                halves.append(o)
            o_ref[0, rows, cols] = jnp.where(lo, halves[0], halves[1]).astype(o_ref.dtype)
        if not mix:
            lse_ref[0, rows, :] = lse_all


def _dilated_pattern(qkv, batch, seq, dil, bias, prev):
    n = batch * seq
    sub = seq // dil
    tq = min(sub, 4 * LANES)
    nsub = tq // LANES
    nblk = sub // tq
    hb = tq // A_HALF
    nhalo = sub // A_HALF

    def main(c):
        return pl.BlockSpec((1, 1, tq, A_W), lambda b, r, i: (b, r, i, c))

    def before(c):
        return pl.BlockSpec((1, 1, A_HALF, A_W),
                            lambda b, r, i: (b, r, jnp.maximum(i * hb - 1, 0), c))

    def after(c):
        return pl.BlockSpec((1, 1, A_HALF, A_W),
                            lambda b, r, i: (b, r, jnp.minimum((i + 1) * hb, nhalo - 1), c))

    o_spec = pl.BlockSpec((1, tq, A_W), lambda b, r, i: (b, i, r))
    l_spec = pl.BlockSpec((1, tq, LANES), lambda b, r, i: (b, i, r))
    in_specs = [main(0), before(1), main(1), after(1), before(2), main(2), after(2),
                pl.BlockSpec((A_HEADS, LANES, 2 * LANES), lambda b, r, i: (0, 0, 0))]
    args = [qkv] * 7 + [bias]
    o_shape = jax.ShapeDtypeStruct((batch, sub, dil * A_W), BF16)
    l_shape = jax.ShapeDtypeStruct((batch, sub, dil * LANES), F32)
    if prev is None:
        out_specs, out_shape = [o_spec, l_spec], [o_shape, l_shape]
    else:
        o0, l0, o1, l1 = prev
        in_specs += [o_spec, l_spec, o_spec, l_spec]
        args += [o0.reshape(o_shape.shape), l0.reshape(l_shape.shape),
                 o1.reshape(o_shape.shape), l1.reshape(l_shape.shape)]
        out_specs, out_shape = o_spec, o_shape
    out = pl.pallas_call(
        functools.partial(_dil_kernel, nsub=nsub, mix=prev is not None),
        grid=(batch, dil, nblk),
        in_specs=in_specs,
        out_specs=out_specs,
        out_shape=out_shape,
        scratch_shapes=[pltpu.VMEM((tq + 2 * A_HALF, A_W), BF16),
                        pltpu.VMEM((tq + 2 * A_HALF, A_W), BF16)],
        compiler_params=_params("parallel", "parallel", "parallel"),
        name=f"dilated_d{dil}",
    )(*args)
    if prev is None:
        return out[0].reshape(n, A_W), out[1].reshape(n, LANES)
    return out.reshape(n, A_W)


def _dilated(views, batch, seq, biases):
    o0, l0 = _dilated_pattern(views[0], batch, seq, A_PATTERNS[0][1], biases[0], None)
    o1, l1 = _dilated_pattern(views[1], batch, seq, A_PATTERNS[1][1], biases[1], None)
    return _dilated_pattern(views[2], batch, seq, A_PATTERNS[2][1], biases[2], (o0, l0, o1, l1))


def _na_kernel(q_ref, k0_ref, k1_ref, k2_ref, v0_ref, v1_ref, v2_ref, bias_ref, o_ref):
    tq = q_ref.shape[1]
    lo, hi = _half_masks(tq)
    zero = jnp.zeros((), BF16)
    for p in range(B_HEADS // 2):
        cols = slice(p * LANES, (p + 1) * LANES)
        qp = q_ref[0, :, cols]
        kk = jnp.concatenate([k0_ref[0, :, cols], k1_ref[0, :, cols], k2_ref[0, :, cols]], axis=0)
        vv = jnp.concatenate([v0_ref[0, :, cols], v1_ref[0, :, cols], v2_ref[0, :, cols]], axis=0)
        halves = []
        for hf in range(2):
            qm = jnp.where(lo if hf == 0 else hi, qp, zero)
            s = _dot_nt(qm, kk) + bias_ref[0, 2 * p + hf]
            o, _, _ = _softmax_pv(s, vv)
            halves.append(o)
        o_ref[0, :, cols] = jnp.where(lo, halves[0], halves[1]).astype(o_ref.dtype)


def _natten(pr, batch, seq, bias):
    n = batch * seq
    tq = NA_QROWS * GRID_W
    nblk = seq // tq
    nkb = NA_KROWS // NA_QROWS
    ncol = pr.shape[1] // A_W
    prv = pr.reshape(batch, seq, pr.shape[1])

    def kv_spec(c, j):
        return pl.BlockSpec((1, tq, A_W),
                            lambda b, i: (b, jnp.clip(i - 1, 0, nblk - nkb) + j, c))

    def variant(i):
        return jnp.where(i == 0, 0, jnp.where(i == nblk - 1, 2, 1))

    in_specs = ([pl.BlockSpec((1, tq, A_W), lambda b, i: (b, i, 3))]
                + [kv_spec(4, j) for j in range(nkb)] + [kv_spec(5, j) for j in range(nkb)]
                + [pl.BlockSpec((1, B_HEADS, tq, nkb * tq), lambda b, i: (variant(i), 0, 0, 0))])
    out = pl.pallas_call(
        _na_kernel,
        grid=(batch, nblk),
        in_specs=in_specs,
        out_specs=pl.BlockSpec((1, tq, A_W), lambda b, i: (b, i, 0)),
        out_shape=jax.ShapeDtypeStruct((batch, seq, A_W), BF16),
        compiler_params=_params("parallel", "parallel"),
        name="natten",
    )(*([prv] * (1 + 2 * nkb) + [bias]))
    return out.reshape(n, A_W)


def _gqa_kernel(qt_ref, k_ref, vt_ref, o_ref, st_ref, e_ref, *, tk):
    tq = qt_ref.shape[2]
    nchunk = k_ref.shape[1] // tk
    heads = qt_ref.shape[1] // HEAD_DIM
    second = pl.program_id(1) % 2 == 1
    zeros = jnp.zeros((HEAD_DIM, tq), BF16)
    cols = []
    for h in range(heads):
        qh = qt_ref[0, h * HEAD_DIM:(h + 1) * HEAD_DIM, :]
        cols.append(jnp.where(second, jnp.concatenate([zeros, qh], axis=0),
                              jnp.concatenate([qh, zeros], axis=0)))
    qst = jnp.concatenate(cols, axis=1)
    width = heads * tq
    nchain = width // MXU_N
    qsts = [qst[:, j * MXU_N:(j + 1) * MXU_N] for j in range(nchain)]

    ones = jnp.ones((GQA_ONES_ROWS, tk), BF16)

    def chunk(c):
        return pl.ds(c * tk if isinstance(c, int) else pl.multiple_of(c * tk, tk), tk)

    def scores(c, slot):
        kc = k_ref[0, chunk(c), :]
        top = []
        for j in range(nchain):
            st = _dot(kc, qsts[j])
            st_ref[slot, j] = st
            top.append(jnp.max(st, axis=0, keepdims=True))
        return tuple(top)

    def values(c, slot, acc, alpha):
        vc = jnp.concatenate([vt_ref[0, :, chunk(c)], ones], axis=0)
        return tuple(acc[j] * alpha[j] + _dot(vc, e_ref[slot, j]) for j in range(nchain))

    def softmax(slot, m, top):
        m_new, alpha = [], []
        for j in range(nchain):
            s = st_ref[slot, j]
            mn = jnp.maximum(m[j], top[j])
            e_ref[slot, j] = jnp.exp2((s - mn).astype(BF16))
            alpha.append(jnp.exp2(m[j] - mn))
            m_new.append(mn)
        return tuple(m_new), tuple(alpha)

    def step(c, carry, ahead=True):
        slot, m, acc, a2, a1, t0, t1 = carry[0] % 4, *carry[1:]
        before = max(c - 2, 0) if isinstance(c, int) else jnp.maximum(c - 2, 0)
        other = (slot + 2) % 4
        vc = jnp.concatenate([vt_ref[0, :, chunk(before)], ones], axis=0)
        kc = k_ref[0, chunk(c + 2), :] if ahead else None
        acc_new, t2 = [], []
        for j in range(nchain):
            if ahead:
                st = _dot(kc, qsts[j])
                st_ref[other, j] = st
                t2.append(jnp.max(st, axis=0, keepdims=True))
            acc_new.append(acc[j] * a2[j] + _dot(vc, e_ref[other, j]))
        acc = tuple(acc_new)
        t2 = tuple(t2) if ahead else t1
        m, a0 = softmax(slot, m, t0)
        return carry[0] + 1, m, acc, a1, a0, t1, t2

    def body(i, carry):
        state = (0,) + carry
        for u in range(4):
            state = step(4 * i + u, state)
        return state[1:]

    row = lambda v: tuple(jnp.full((1, MXU_N), v, F32) for _ in range(nchain))
    carry = (row(NEG), tuple(jnp.zeros((HEAD_DIM + GQA_ONES_ROWS, MXU_N), F32) for _ in range(nchain)),
             row(1.0), row(1.0), scores(0, 0), scores(1, 1))
    for slot in (2, 3):
        e_ref[slot] = jnp.zeros(e_ref.shape[1:], BF16)
    carry = lax.fori_loop(0, nchunk // 4 - 1, body, carry)
    state = (0,) + carry
    for u in range(4):
        state = step(nchunk - 4 + u, state, ahead=u < 2)
    _, m, acc, a2, a1, _, _ = state
    acc = values(nchunk - 2, 2, acc, a2)
    acc = values(nchunk - 1, 3, acc, a1)
    o = jnp.concatenate([a[:HEAD_DIM] / a[HEAD_DIM:HEAD_DIM + 1] for a in acc], axis=1)
    ot = jnp.concatenate([o[:, h * tq:(h + 1) * tq] for h in range(heads)], axis=0)
    o_ref[0] = ot.T.astype(o_ref.dtype)


def _gqa(qt, k, vt, batch, seq, tq=256, tk=512):
    n = batch * seq
    gw = (C_Q_HEADS // C_KV_HEADS) * HEAD_DIM
    nchain = (C_Q_HEADS // C_KV_HEADS) * tq // MXU_N
    assert seq % (4 * tk) == 0
    out = pl.pallas_call(
        functools.partial(_gqa_kernel, tk=tk),
        grid=(batch, C_KV_HEADS, seq // tq),
        in_specs=[pl.BlockSpec((1, gw, tq), lambda b, g, i: (b, g, i)),
                  pl.BlockSpec((1, seq, LANES), lambda b, g, i: (b, 0, g // 2)),
                  pl.BlockSpec((1, HEAD_DIM, seq), lambda b, g, i: (b, g, 0))],
        out_specs=pl.BlockSpec((1, tq, gw), lambda b, g, i: (b, i, g)),
        out_shape=jax.ShapeDtypeStruct((batch, seq, C_Q_HEADS * HEAD_DIM), BF16),
        scratch_shapes=[pltpu.VMEM((4, nchain, tk, MXU_N), F32),
                        pltpu.VMEM((4, nchain, tk, MXU_N), BF16)],
        compiler_params=_params("parallel", "parallel", "parallel"),
        name="gqa",
    )(qt, k.reshape(batch, seq, -1), vt)
    return out.reshape(n, C_Q_HEADS * HEAD_DIM)


def _post_kernel(*refs, route):
    if route:
        (x_ref, a1_ref, a2_ref, wo1_ref, wo2_ref, gxa_ref, wq_ref, kv_ref, wox_ref, gffn_ref,
         wr_ref, x2_ref, hn_ref, info_ref) = refs
    else:
        (x_ref, a1_ref, a2_ref, wo1_ref, wo2_ref, gxa_ref, wq_ref, kv_ref, wox_ref, gffn_ref,
         x2_ref, hn_ref) = refs
    x1 = x_ref[...] + _dot(a1_ref[...], wo1_ref[...]) + _dot(a2_ref[...], wo2_ref[...])
    tm = x1.shape[0]
    h = _rms(x1, gxa_ref[...]).astype(BF16)
    q = _dot(h, wq_ref[...]).astype(BF16)
    lo, hi = _half_masks(tm)
    zero = jnp.zeros((), BF16)
    outs = []
    for p in range(X_HEADS // 2):
        qp = q[:, p * LANES:(p + 1) * LANES]
        kk = kv_ref[0, :, p * LANES:(p + 1) * LANES]
        vv = kv_ref[0, :, X_W + p * LANES:X_W + (p + 1) * LANES]
        halves = []
        for hf in range(2):
            qm = jnp.where(lo if hf == 0 else hi, qp, zero)
            o, _, _ = _softmax_pv(_dot_nt(qm, kk), vv)
            halves.append(o)
        outs.append(jnp.where(lo, halves[0], halves[1]).astype(BF16))
    x2 = x1 + _dot(jnp.concatenate(outs, axis=1), wox_ref[...])
    x2_ref[...] = x2
    hf32 = _rms(x2, gffn_ref[...])
    hn_ref[...] = _pack_bf16_pairs(hf32) if route else hf32.astype(BF16)
    if route:
        lane = lax.broadcasted_iota(jnp.int32, (tm, LANES), 1).astype(F32)
        lg = jnp.full((tm, LANES), NEG, F32)
        for ex in range(N_EXPERTS):
            le = jnp.sum(hf32 * wr_ref[ex:ex + 1, :], axis=1, keepdims=True)
            lg = jnp.where(lane == float(ex), le, lg)
        m1 = jnp.max(lg, axis=1, keepdims=True)
        i1 = jnp.min(jnp.where(lg == m1, lane, float(LANES)), axis=1, keepdims=True)
        lg2 = jnp.where(lane == i1, NEG, lg)
        m2 = jnp.max(lg2, axis=1, keepdims=True)
        i2 = jnp.min(jnp.where(lg2 == m2, lane, float(LANES)), axis=1, keepdims=True)
        e = jnp.exp(m2 - m1)
        den = 1.0 + e
        info_ref[...] = (jnp.where(lane == 0.0, i1, 0.0) + jnp.where(lane == 1.0, i2, 0.0)
                         + jnp.where(lane == 2.0, 1.0 / den, 0.0) + jnp.where(lane == 3.0, e / den, 0.0))


def _post(x, a1, a2, c1, c2, wo1, wo2, gxa, wq, kvm, wox, gffn, wr, seq, tm):
    n, d = x.shape
    spb = seq // tm
    row = lambda i: (i, 0)
    const = lambda i: (0, 0)
    in_specs = [pl.BlockSpec((tm, d), row),
                pl.BlockSpec((tm, A_W), lambda i: (i, c1)),
                pl.BlockSpec((tm, A_W), lambda i: (i, c2)),
                pl.BlockSpec((A_W, d), const),
                pl.BlockSpec((A_W, d), const),
                pl.BlockSpec((1, d), const),
                pl.BlockSpec((d, X_W), const),
                pl.BlockSpec((1, N_MEM, 2 * X_W), lambda i: (i // spb, 0, 0)),
                pl.BlockSpec((X_W, d), const),
                pl.BlockSpec((1, d), const)]
    args = [x, a1, a2, wo1, wo2, gxa.reshape(1, d), wq, kvm, wox, gffn.reshape(1, d)]
    hshape, hdtype = ((d, BF16) if wr is None else (d // 2, jnp.uint32))
    out_specs = [pl.BlockSpec((tm, d), row), pl.BlockSpec((tm, hshape), row)]
    out_shape = [jax.ShapeDtypeStruct((n, d), F32), jax.ShapeDtypeStruct((n, hshape), hdtype)]
    if wr is not None:
        in_specs.append(pl.BlockSpec((N_EXPERTS, d), const))
        args.append(wr)
        out_specs.append(pl.BlockSpec((tm, LANES), row))
        out_shape.append(jax.ShapeDtypeStruct((n, LANES), F32))
    return pl.pallas_call(
        functools.partial(_post_kernel, route=wr is not None),
        grid=(n // tm,),
        in_specs=in_specs,
        out_specs=out_specs,
        out_shape=out_shape,
        compiler_params=_params("parallel"),
        name="post_route" if wr is not None else "post",
    )(*args)


def _swiglu_chunk(h, wg, wu, wd):
    g = _dot(h, wg)
    u = _dot(h, wu)
    a = (g / (1.0 + jnp.exp(-g))) * u
    return _dot(a.astype(BF16), wd)


def _ffn_kernel(x_ref, h_ref, wg_ref, wu_ref, wd_ref, o_ref):
    f = pl.program_id(1)
    y = _swiglu_chunk(h_ref[...], wg_ref[...], wu_ref[...], wd_ref[...])

    @pl.when(f == 0)
    def _():
        o_ref[...] = x_ref[...] + y

    @pl.when(f > 0)
    def _():
        o_ref[...] += y


def _ffn(x, hn, wg, wu, wd, tm, tf):
    n, d = x.shape
    ff = wg.shape[1]
    return pl.pallas_call(
        _ffn_kernel,
        grid=(n // tm, ff // tf),
        in_specs=[pl.BlockSpec((tm, d), lambda i, f: (i, 0)),
                  pl.BlockSpec((tm, d), lambda i, f: (i, 0)),
                  pl.BlockSpec((d, tf), lambda i, f: (0, f)),
                  pl.BlockSpec((d, tf), lambda i, f: (0, f)),
                  pl.BlockSpec((tf, d), lambda i, f: (f, 0))],
        out_specs=pl.BlockSpec((tm, d), lambda i, f: (i, 0)),
        out_shape=jax.ShapeDtypeStruct((n, d), F32),
        compiler_params=_params("parallel", "arbitrary"),
        name="ffn",
    )(x, hn, wg, wu, wd)


def _route(info, tmg):
    n = info.shape[0]
    ea = info[:, 0:2].astype(jnp.int32).reshape(-1)
    onehot = (ea[:, None] == jnp.arange(N_EXPERTS, dtype=jnp.int32)[None, :]).astype(jnp.int32)
    csum = jnp.cumsum(onehot, axis=0)
    counts = csum[-1]
    padded = (counts + tmg - 1) // tmg * tmg
    ends = jnp.cumsum(padded)
    starts = ends - padded
    dest = jnp.sum(onehot * (csum - 1 + starts[None, :]), axis=1)
    r_max = 2 * n + N_EXPERTS * tmg
    token = jnp.arange(2 * n, dtype=jnp.int32) // 2
    src = jnp.zeros((r_max,), jnp.int32).at[dest].set(token, unique_indices=True)
    tile_start = jnp.arange(r_max // tmg, dtype=jnp.int32) * tmg
    tile_e = jnp.sum((tile_start[:, None] >= ends[None, :]).astype(jnp.int32), axis=1)
    tile_e = jnp.minimum(tile_e, N_EXPERTS - 1)
    tile_ok = (tile_start < ends[-1]).astype(jnp.int32)
    return src, dest, tile_e, tile_ok


def _row_copy(src_ref, row, dst_ref, r, sem):
    return pltpu.make_async_copy(src_ref.at[pl.ds(row, 1)], dst_ref.at[pl.ds(r, 1)], sem)


def _gather_kernel(idx_ref, src_ref, o_ref, buf, sem):
    i = pl.program_id(0)
    ntile = pl.num_programs(0) - 1
    tg = o_ref.shape[0]

    @pl.when(i < ntile)
    def _():
        slot = i % 2

        def start(pair, c):
            for priority in range(2):
                r = 2 * pair + priority
                _row_copy(src_ref, idx_ref[0, 0, r], buf.at[slot], r, sem.at[slot]).start(priority=priority)
            return c

        lax.fori_loop(0, tg // 2, start, 0, unroll=4)

    @pl.when(i > 0)
    def _():
        slot = (i + 1) % 2

        def wait(r, c):
            _row_copy(src_ref, 0, buf.at[slot], 0, sem.at[slot]).wait()
            return c

        lax.fori_loop(0, tg, wait, 0, unroll=8)
        o_ref[...] = buf[slot]


def _gather_rows(src, idx, tg):
    n, d = src.shape
    r = idx.shape[0]
    ntile = r // tg
    return pl.pallas_call(
        _gather_kernel,
        grid=(ntile + 1,),
        in_specs=[pl.BlockSpec((1, 1, tg), lambda i: (jnp.minimum(i, ntile - 1), 0, 0),
                               memory_space=pltpu.SMEM),
                  pl.BlockSpec(memory_space=pl.ANY)],
        out_specs=pl.BlockSpec((tg, d), lambda i: (jnp.maximum(i - 1, 0), 0)),
        out_shape=jax.ShapeDtypeStruct((r, d), src.dtype),
        scratch_shapes=[pltpu.VMEM((2, tg, d), src.dtype), pltpu.SemaphoreType.DMA((2,))],
        compiler_params=_params("arbitrary"),
        name="moe_gather",
    )(idx.reshape(ntile, 1, tg), src)


def _expert_kernel(te_ref, ok_ref, x_ref, wg_ref, wu_ref, wd_ref, o_ref, hb_ref):
    t = pl.program_id(0)
    f = pl.program_id(1)
    ok = ok_ref[t] == 1

    @pl.when(jnp.logical_and(ok, f == 0))
    def _():
        hb_ref[...] = _unpack_bf16_pairs(x_ref[...])
        o_ref[...] = _swiglu_chunk(hb_ref[...], wg_ref[0], wu_ref[0], wd_ref[0])

    @pl.when(jnp.logical_and(ok, f > 0))
    def _():
        o_ref[...] += _swiglu_chunk(hb_ref[...], wg_ref[0], wu_ref[0], wd_ref[0])

    @pl.when(jnp.logical_and(jnp.logical_not(ok), f == 0))
    def _():
        o_ref[...] = jnp.zeros_like(o_ref)


def _expert_ffn(xs, tile_e, tile_ok, wg, wu, wd, tmg, tf):
    r = xs.shape[0]
    _, d, ff = wg.shape
    grid_spec = pltpu.PrefetchScalarGridSpec(
        num_scalar_prefetch=2,
        grid=(r // tmg, ff // tf),
        in_specs=[pl.BlockSpec((tmg, d // 2), lambda t, f, te, ok: (t, 0)),
                  pl.BlockSpec((1, d, tf), lambda t, f, te, ok: (te[t], 0, f * ok[t])),
                  pl.BlockSpec((1, d, tf), lambda t, f, te, ok: (te[t], 0, f * ok[t])),
                  pl.BlockSpec((1, tf, d), lambda t, f, te, ok: (te[t], f * ok[t], 0))],
        out_specs=pl.BlockSpec((tmg, d), lambda t, f, te, ok: (t, 0)),
        scratch_shapes=[pltpu.VMEM((tmg, d), BF16)],
    )
    return pl.pallas_call(
        _expert_kernel,
        grid_spec=grid_spec,
        out_shape=jax.ShapeDtypeStruct((r, d), F32),
        compiler_params=_params("arbitrary", "arbitrary"),
        name="moe_experts",
    )(tile_e, tile_ok, xs, wg, wu, wd)


def _combine_kernel(*refs, norm):
    if norm:
        d_ref, x_ref, info_ref, ys_ref, gain_ref, o_ref, buf, sem = refs
    else:
        (d_ref, x_ref, info_ref, ys_ref, o_ref, buf, sem), gain_ref = refs, None
    tc = o_ref.shape[0]

    def copies(r):
        return (_row_copy(ys_ref, d_ref[0, 0, 2 * r], buf.at[0], r, sem),
                _row_copy(ys_ref, d_ref[0, 0, 2 * r + 1], buf.at[1], r, sem))

    def start(r, c):
        for priority, cp in enumerate(copies(r)):
            cp.start(priority=priority)
        return c

    def wait(r, c):
        for cp in copies(r):
            cp.wait()
        return c

    lax.fori_loop(0, tc, start, 0, unroll=4)
    lax.fori_loop(0, tc, wait, 0, unroll=4)
    g1 = info_ref[:, 2:3]
    g2 = info_ref[:, 3:4]
    o = x_ref[...] + (g1 * buf[0] + g2 * buf[1])
    o_ref[...] = o if gain_ref is None else _rms(o, gain_ref[...])


def _combine(x, info, dest, ys, gain, tc):
    n, d = x.shape
    in_specs = [pl.BlockSpec((1, 1, 2 * tc), lambda i: (i, 0, 0), memory_space=pltpu.SMEM),
                pl.BlockSpec((tc, d), lambda i: (i, 0)),
                pl.BlockSpec((tc, LANES), lambda i: (i, 0)),
                pl.BlockSpec(memory_space=pl.ANY)]
    args = [dest.reshape(n // tc, 1, 2 * tc), x, info, ys]
    if gain is not None:
        in_specs.append(pl.BlockSpec((1, d), lambda i: (0, 0)))
        args.append(gain.reshape(1, d))
    return pl.pallas_call(
        functools.partial(_combine_kernel, norm=gain is not None),
        grid=(n // tc,),
        in_specs=in_specs,
        out_specs=pl.BlockSpec((tc, d), lambda i: (i, 0)),
        out_shape=jax.ShapeDtypeStruct((n, d), F32),
        scratch_shapes=[pltpu.VMEM((2, tc, d), F32), pltpu.SemaphoreType.DMA(())],
        compiler_params=_params("arbitrary"),
        name="moe_combine",
    )(*args)


def _moe(x2, hf, info, wg, wu, wd, gain, tmg, tf, tg, tc):
    src, dest, tile_e, tile_ok = _route(info, tmg)
    xs = _gather_rows(hf, src, tg)
    ys = _expert_ffn(xs, tile_e, tile_ok, wg, wu, wd, tmg, tf)
    return _combine(x2, info, dest, ys, gain, tc)


def _final_norm_kernel(x_ref, g_ref, o_ref):
    o_ref[...] = _rms(x_ref[...], g_ref[...])


def _final_norm(x, g, tm):
    n, d = x.shape
    return pl.pallas_call(
        _final_norm_kernel,
        grid=(n // tm,),
        in_specs=[pl.BlockSpec((tm, d), lambda i: (i, 0)), pl.BlockSpec((1, d), lambda i: (0, 0))],
        out_specs=pl.BlockSpec((tm, d), lambda i: (i, 0)),
        out_shape=jax.ShapeDtypeStruct((n, d), F32),
        compiler_params=_params("parallel"),
        name="final_norm",
    )(x, g.reshape(1, d))


def _t5_bucket_np(rel):
    nb = N_BUCKETS // 2
    max_exact = nb // 2
    ret = np.where(rel > 0, nb, 0)
    n = np.abs(rel)
    large = max_exact + (np.log(np.maximum(n, 1).astype(np.float32) / max_exact)
                         / math.log(MAX_DISTANCE / max_exact) * (nb - max_exact)).astype(np.int32)
    large = np.minimum(large, nb - 1)
    return ret + np.where(n < max_exact, n, large)


def _dilated_bias(rel_table, dil):
    off = np.arange(2 * LANES)[None, :] - A_HALF - np.arange(LANES)[:, None]
    bucket = _t5_bucket_np(off * dil)
    sel = (bucket[:, :, None] == np.arange(N_BUCKETS)).astype(np.float32)
    bias = jnp.einsum('qkb,bh->hqk', sel, rel_table.astype(F32), precision=lax.Precision.HIGHEST)
    return jnp.where(jnp.asarray(np.abs(off) <= A_HALF)[None], bias, NEG)


def _natten_bias(rpb):
    rows = 4 * NA_KROWS
    c = np.arange(GRID_W)
    cs = np.clip(c - NA_KW // 2, 0, GRID_W - NA_KW)
    col_ok = (c[None, :] >= cs[:, None]) & (c[None, :] < cs[:, None] + NA_KW)
    dc = np.clip(c[None, :] - c[:, None], -(NA_KW - 1), NA_KW - 1) + NA_KW - 1
    col_sel = (dc[:, :, None] == np.arange(2 * NA_KW - 1)).astype(np.float32)
    row_sel, row_ok = [], []
    for r0 in (0, 2 * NA_QROWS, rows - NA_QROWS):
        ks = int(np.clip(r0 - NA_KH // 2, 0, rows - NA_KROWS))
        qr = r0 + np.arange(NA_QROWS)
        kr = ks + np.arange(NA_KROWS)
        rs = np.clip(qr - NA_KH // 2, 0, rows - NA_KH)
        row_ok.append((kr[None, :] >= rs[:, None]) & (kr[None, :] < rs[:, None] + NA_KH))
        dr = kr[None, :] - qr[:, None] + NA_KH - 1
        row_sel.append((dr[:, :, None] == np.arange(2 * NA_KH - 1)).astype(np.float32))
    row_sel = np.stack(row_sel)
    ok = np.stack(row_ok)[:, :, None, :, None] & col_ok[None, None, :, None, :]
    by_col = jnp.einsum('hab,xyb->haxy', rpb.astype(F32), col_sel, precision=lax.Precision.HIGHEST)
    table = jnp.einsum('vqka,haxy->vhqxky', row_sel, by_col, precision=lax.Precision.HIGHEST)
    table = jnp.where(jnp.asarray(ok)[:, None], table, NEG)
    tq = NA_QROWS * GRID_W
    return table.reshape(len(row_ok), rpb.shape[0], tq, NA_KROWS * GRID_W)


def _rope_tables(seq):
    t = jnp.arange(seq)
    half = HEAD_DIM // 2
    freqs = ROPE_THETA ** (-jnp.arange(0, half, 2, dtype=F32) / half)
    ang_r = (t // GRID_W).astype(F32)[:, None] * freqs[None, :]
    ang_c = (t % GRID_W).astype(F32)[:, None] * freqs[None, :]
    cos_h = jnp.concatenate([jnp.cos(ang_r)] * 2 + [jnp.cos(ang_c)] * 2, axis=1)
    sin_h = jnp.concatenate([-jnp.sin(ang_r), jnp.sin(ang_r), -jnp.sin(ang_c), jnp.sin(ang_c)], axis=1)
    return jnp.concatenate([cos_h, cos_h], axis=1), jnp.concatenate([sin_h, sin_h], axis=1)


def _prepare(p):
    w = {}
    ev_in = p['ev_w_in']
    scale = np.ones((ev_in.shape[-1],), np.float32)
    scale[0:A_W] = Q_SCALE
    scale[3 * A_W:4 * A_W] = Q_SCALE
    w['ev_w_in'] = (ev_in * scale).astype(BF16)
    w['ev_w_out'] = p['ev_w_out'].astype(BF16)
    w['ev_w_gate'] = p['ev_w_gate'].astype(BF16)
    w['ev_w_up'] = p['ev_w_up'].astype(BF16)
    w['ev_w_down'] = p['ev_w_down'].astype(BF16)
    w['od_w_in'] = p['od_w_in'].astype(BF16)
    w['od_w_out'] = p['od_w_out'].astype(BF16)
    w['od_q_norm'] = jnp.concatenate([p['od_q_norm']] * 2, axis=-1)[:, None, :]
    w['od_k_norm'] = jnp.concatenate([p['od_k_norm']] * 2, axis=-1)[:, None, :]
    w['od_router'] = jnp.swapaxes(p['od_router'], 1, 2)
    w['od_moe_gate'] = p['od_moe_gate'].astype(BF16)
    w['od_moe_up'] = p['od_moe_up'].astype(BF16)
    w['od_moe_down'] = p['od_moe_down'].astype(BF16)
    w['xa_w_q'] = (p['xa_w_q'] * Q_SCALE).astype(BF16)
    w['xa_w_kv'] = p['xa_w_kv'].astype(BF16)
    w['xa_w_out'] = p['xa_w_out'].astype(BF16)
    w['dil_bias'] = [_dilated_bias(p['rel_table'], dil) for _, dil in A_PATTERNS]
    w['na_bias'] = [_natten_bias(p['ev_na_rpb'][i]) for i in range(p['ev_na_rpb'].shape[0])]
    return w


def _trunk(x, mem, p, w):
    batch, seq, d = x.shape
    n = batch * seq
    xf = x.reshape(n, d)
    memf = mem.reshape(batch * N_MEM, d)
    cos, sin = _rope_tables(seq)
    depth = p['xa_norm'].shape[0]
    for layer in range(depth):
        i = layer // 2
        if layer % 2 == 0:
            pr, v4, v16 = _proj_even(xf, p['ev_norm_mix'][i], w['ev_w_in'][i], seq, tm=512)
            a1 = _dilated([pr.reshape(batch, 1, seq, -1), v4, v16], batch, seq, w['dil_bias'])
            a2 = _natten(pr, batch, seq, w['na_bias'][i])
            c1 = c2 = 0
            wo = w['ev_w_out'][i]
            wr = None
        else:
            q, kk, vv = _proj_odd(xf, p['od_norm_mix'][i], w['od_w_in'][i], w['od_q_norm'][i],
                                  w['od_k_norm'][i], cos, sin, seq, tm=512)
            a1 = a2 = _gqa(q, kk, vv, batch, seq)
            c1, c2 = 0, 1
            wo = w['od_w_out'][i]
            wr = w['od_router'][i]
        kvm = _proj(memf, p['xa_mem_norm'][layer], w['xa_w_kv'][layer], tm=512)
        kvm = kvm.reshape(batch, N_MEM, 2 * X_W)
        gffn = p['ev_norm_ffn'][i] if layer % 2 == 0 else p['od_norm_ffn'][i]
        outs = _post(xf, a1, a2, c1, c2, wo[:A_W], wo[A_W:], p['xa_norm'][layer], w['xa_w_q'][layer],
                     kvm, w['xa_w_out'][layer], gffn, wr, seq, tm=512)
        if layer % 2 == 0:
            x2, hn = outs
            xf = _ffn(x2, hn, w['ev_w_gate'][i], w['ev_w_up'][i], w['ev_w_down'][i], tm=512, tf=1408)
        else:
            x2, hf, info = outs
            gain = p['final_norm'] if layer == depth - 1 else None
            xf = _moe(x2, hf, info, w['od_moe_gate'][i], w['od_moe_up'][i], w['od_moe_down'][i],
                      gain, tmg=1024, tf=512, tg=512, tc=256)
    if depth % 2 == 1:
        xf = _final_norm(xf, p['final_norm'], tm=1024)
    return xf.reshape(batch, seq, d)


def kernel(x_prompt, x_sample, mem_prompt, mem_sample, rel_table, ev_norm_mix, ev_w_in, ev_na_rpb, ev_w_out, ev_norm_ffn, ev_w_gate, ev_w_up, ev_w_down, od_norm_mix, od_w_in, od_q_norm, od_k_norm, od_w_out, od_norm_ffn, od_router, od_moe_gate, od_moe_up, od_moe_down, xa_norm, xa_mem_norm, xa_w_q, xa_w_kv, xa_w_out, final_norm):
    p = dict(rel_table=rel_table,
             ev_norm_mix=ev_norm_mix, ev_w_in=ev_w_in, ev_na_rpb=ev_na_rpb, ev_w_out=ev_w_out,
             ev_norm_ffn=ev_norm_ffn, ev_w_gate=ev_w_gate, ev_w_up=ev_w_up, ev_w_down=ev_w_down,
             od_norm_mix=od_norm_mix, od_w_in=od_w_in, od_q_norm=od_q_norm, od_k_norm=od_k_norm,
             od_w_out=od_w_out, od_norm_ffn=od_norm_ffn, od_router=od_router,
             od_moe_gate=od_moe_gate, od_moe_up=od_moe_up, od_moe_down=od_moe_down,
             xa_norm=xa_norm, xa_mem_norm=xa_mem_norm, xa_w_q=xa_w_q, xa_w_kv=xa_w_kv,
             xa_w_out=xa_w_out, final_norm=final_norm)
    w = _prepare(p)
    return (_trunk(x_prompt, mem_prompt, p, w), _trunk(x_sample, mem_sample, p, w))
```

```python
import functools
import math

import numpy as np
import jax
import jax.numpy as jnp
from jax import lax
from jax.experimental import pallas as pl
from jax.experimental.pallas import tpu as pltpu

F32 = jnp.float32
BF16 = jnp.bfloat16

D_MODEL = 1024
HEAD_DIM = 64
GRID_W = 64
N_MEM = 256
A_HEADS = 8
A_PATTERNS = ((128, 1), (512, 4), (2048, 16))
A_HALF = 64
B_HEADS = 8
NA_KH = 8
NA_KW = 16
C_Q_HEADS = 16
C_KV_HEADS = 4
ROPE_THETA = 10000.0
N_BUCKETS = 32
MAX_DISTANCE = 1024
X_HEADS = 4
N_EXPERTS = 8
EPS = 1e-6
NEG = -1e30
A_W = A_HEADS * HEAD_DIM
X_W = X_HEADS * HEAD_DIM
Q_SCALE = HEAD_DIM ** -0.5
QK_SCALE = Q_SCALE * math.log2(math.e)

LANES = 128
MXU_N = 256
GQA_ONES_ROWS = 16
VMEM_LIMIT = 48 * 1024 * 1024

NA_QROWS = 4
NA_KROWS = 12


def _params(*sem):
    return pltpu.CompilerParams(dimension_semantics=sem, vmem_limit_bytes=VMEM_LIMIT)


def _rms(x, g):
    ms = jnp.mean(x * x, axis=-1, keepdims=True)
    return x * lax.rsqrt(ms + EPS) * g


def _dot(a, b):
    return jnp.dot(a, b, preferred_element_type=F32)


def _dot_nt(a, b):
    return lax.dot_general(a, b, (((1,), (1,)), ((), ())), preferred_element_type=F32)


def _half_masks(rows):
    lane = lax.broadcasted_iota(jnp.int32, (rows, LANES), 1)
    lo = lane < HEAD_DIM
    return lo, jnp.logical_not(lo)


def _pack_bf16_pairs(x):
    half = x.shape[1] // 2
    bits = lax.bitcast_convert_type(x.astype(BF16).astype(F32), jnp.uint32)
    return (bits[:, half:] & jnp.uint32(0xFFFF0000)) | (bits[:, :half] >> 16)


def _unpack_bf16_pairs(p):
    lo = lax.bitcast_convert_type(p << 16, F32).astype(BF16)
    hi = lax.bitcast_convert_type(p & jnp.uint32(0xFFFF0000), F32).astype(BF16)
    return jnp.concatenate([lo, hi], axis=1)


def _softmax_pv(s, v):
    m = jnp.max(s, axis=1, keepdims=True)
    e = jnp.exp(s - m)
    l = jnp.sum(e, axis=1, keepdims=True)
    o = _dot(e.astype(BF16), v) / l
    return o, m, l


def _proj_kernel(x_ref, g_ref, w_ref, o_ref):
    h = _rms(x_ref[...], g_ref[...]).astype(BF16)
    o_ref[...] = _dot(h, w_ref[...]).astype(o_ref.dtype)


def _proj(x, g, w, tm):
    n, d = x.shape
    nout = w.shape[1]
    return pl.pallas_call(
        _proj_kernel,
        grid=(n // tm,),
        in_specs=[pl.BlockSpec((tm, d), lambda i: (i, 0)),
                  pl.BlockSpec((1, d), lambda i: (0, 0)),
                  pl.BlockSpec((d, nout), lambda i: (0, 0))],
        out_specs=pl.BlockSpec((tm, nout), lambda i: (i, 0)),
        out_shape=jax.ShapeDtypeStruct((n, nout), BF16),
        compiler_params=_params("parallel"),
        name="proj",
    )(x, g.reshape(1, d), w)


def _proj_even_kernel(x_ref, g_ref, w_ref, o_ref, *rest):
    views, y_ref = rest[:-1], rest[-1]
    h = _rms(x_ref[...], g_ref[...]).astype(BF16)
    y = _dot(h, w_ref[...])
    o_ref[...] = y.astype(o_ref.dtype)
    tm = y.shape[0]
    ncol = y_ref.shape[0]
    for c in range(ncol):
        y_ref[c] = y[:, c * LANES:(c + 1) * LANES]
    for v_ref in views:
        dil = v_ref.shape[1]
        for r in range(dil):
            for c in range(ncol):
                rows = y_ref[c, pl.ds(r, tm // dil, stride=dil), :]
                v_ref[0, r, :, c * LANES:(c + 1) * LANES] = rows.astype(v_ref.dtype)


def _proj_even(x, g, w, seq, tm):
    n, d = x.shape
    nout = w.shape[1]
    spb = seq // tm
    batch = n // seq
    dils = [dil for _, dil in A_PATTERNS if dil > 1]
    aw = 3 * A_W
    return pl.pallas_call(
        _proj_even_kernel,
        grid=(n // tm,),
        in_specs=[pl.BlockSpec((tm, d), lambda i: (i, 0)),
                  pl.BlockSpec((1, d), lambda i: (0, 0)),
                  pl.BlockSpec((d, nout), lambda i: (0, 0))],
        out_specs=[pl.BlockSpec((tm, nout), lambda i: (i, 0))]
        + [pl.BlockSpec((1, dil, tm // dil, aw), lambda i: (i // spb, 0, i % spb, 0)) for dil in dils],
        out_shape=[jax.ShapeDtypeStruct((n, nout), BF16)]
        + [jax.ShapeDtypeStruct((batch, dil, seq // dil, aw), BF16) for dil in dils],
        scratch_shapes=[pltpu.VMEM((aw // LANES, tm, LANES), F32)],
        compiler_params=_params("parallel"),
        name="proj_even",
    )(x, g.reshape(1, d), w)


def _proj_odd_kernel(x_ref, g_ref, w_ref, qg_ref, kg_ref, cos_ref, sin_ref, q_ref, k_ref, v_ref):
    h = _rms(x_ref[...], g_ref[...]).astype(BF16)
    y = _dot(h, w_ref[...])
    tm = y.shape[0]
    cos = cos_ref[...]
    sin = sin_ref[...]
    lane = lax.broadcasted_iota(jnp.int32, (tm, LANES), 1)
    lo = lane < HEAD_DIM
    first = (lane & 31) < 16

    def norm_rope(c, gain, scale):
        ss = c * c
        s_lo = jnp.sum(jnp.where(lo, ss, 0.0), axis=1, keepdims=True)
        s_hi = jnp.sum(jnp.where(lo, 0.0, ss), axis=1, keepdims=True)
        ms = jnp.where(lo, s_lo, s_hi) * (1.0 / HEAD_DIM)
        c = c * lax.rsqrt(ms + EPS) * gain
        partner = jnp.where(first, pltpu.roll(c, LANES - 16, 1), pltpu.roll(c, 16, 1))
        return (c * cos + partner * sin) * scale

    nq = q_ref.shape[1] // LANES
    nk = k_ref.shape[1] // LANES
    for j in range(nq):
        c = y[:, j * LANES:(j + 1) * LANES]
        q_ref[0, j * LANES:(j + 1) * LANES, :] = norm_rope(c, qg_ref[...], QK_SCALE).T.astype(BF16)
    for j in range(nk):
        c = y[:, (nq + j) * LANES:(nq + j + 1) * LANES]
        k_ref[:, j * LANES:(j + 1) * LANES] = norm_rope(c, kg_ref[...], 1.0).astype(BF16)
    for j in range(nk):
        c = y[:, (nq + nk + j) * LANES:(nq + nk + j + 1) * LANES]
        v_ref[0, j * LANES:(j + 1) * LANES, :] = c.T.astype(BF16)


def _proj_odd(x, g, w, qg, kg, cos, sin, seq, tm):
    n, d = x.shape
    nq = C_Q_HEADS * HEAD_DIM
    nk = C_KV_HEADS * HEAD_DIM
    spb = seq // tm
    batch = n // seq
    return pl.pallas_call(
        _proj_odd_kernel,
        grid=(n // tm,),
        in_specs=[pl.BlockSpec((tm, d), lambda i: (i, 0)),
                  pl.BlockSpec((1, d), lambda i: (0, 0)),
                  pl.BlockSpec((d, nq + 2 * nk), lambda i: (0, 0)),
                  pl.BlockSpec((1, LANES), lambda i: (0, 0)),
                  pl.BlockSpec((1, LANES), lambda i: (0, 0)),
                  pl.BlockSpec((tm, LANES), lambda i: (i % spb, 0)),
                  pl.BlockSpec((tm, LANES), lambda i: (i % spb, 0))],
        out_specs=[pl.BlockSpec((1, nq, tm), lambda i: (i // spb, 0, i % spb)),
                   pl.BlockSpec((tm, nk), lambda i: (i, 0)),
                   pl.BlockSpec((1, nk, tm), lambda i: (i // spb, 0, i % spb))],
        out_shape=[jax.ShapeDtypeStruct((batch, nq, seq), BF16),
                   jax.ShapeDtypeStruct((n, nk), BF16),
                   jax.ShapeDtypeStruct((batch, nk, seq), BF16)],
        compiler_params=_params("parallel"),
        name="proj_odd",
    )(x, g.reshape(1, d), w, qg, kg, cos, sin)


def _dil_kernel(*refs, nsub, mix):
    if mix:
        (q_ref, kp_ref, kc_ref, kn_ref, vp_ref, vc_ref, vn_ref, bias_ref,
         o0_ref, l0_ref, o1_ref, l1_ref, o_ref, kwin, vwin) = refs
    else:
        (q_ref, kp_ref, kc_ref, kn_ref, vp_ref, vc_ref, vn_ref, bias_ref,
         o_ref, lse_ref, kwin, vwin) = refs
    tq = q_ref.shape[2]
    i = pl.program_id(2)
    nblk = pl.num_programs(2)
    kwin[0:A_HALF] = kp_ref[0, 0]
    kwin[A_HALF:A_HALF + tq] = kc_ref[0, 0]
    kwin[A_HALF + tq:] = kn_ref[0, 0]
    vwin[0:A_HALF] = vp_ref[0, 0]
    vwin[A_HALF:A_HALF + tq] = vc_ref[0, 0]
    vwin[A_HALF + tq:] = vn_ref[0, 0]
    lo, hi = _half_masks(LANES)
    lane = lax.broadcasted_iota(jnp.int32, (LANES, LANES), 1)
    kj = lax.broadcasted_iota(jnp.int32, (LANES, 2 * LANES), 1)
    zero = jnp.zeros((), BF16)
    for a in range(nsub):
        rows = slice(a * LANES, (a + 1) * LANES)
        valid = None
        if a == 0:
            valid = kj >= jnp.where(i == 0, A_HALF, 0)
        if a == nsub - 1:
            v2 = kj < jnp.where(i == nblk - 1, LANES + A_HALF, 2 * LANES)
            valid = v2 if valid is None else jnp.logical_and(valid, v2)
        lse_all = jnp.zeros((LANES, LANES), F32)
        for p in range(A_HEADS // 2):
            cols = slice(p * LANES, (p + 1) * LANES)
            qp = q_ref[0, 0, rows, cols]
            kk = kwin[a * LANES:(a + 2) * LANES, cols]
            vv = vwin[a * LANES:(a + 2) * LANES, cols]
            halves = []
            for hf in range(2):
                h = 2 * p + hf
                qm = jnp.where(lo if hf == 0 else hi, qp, zero)
                s = _dot_nt(qm, kk) + bias_ref[h]
                if valid is not None:
                    s = jnp.where(valid, s, NEG)
                o, m, l = _softmax_pv(s, vv)
                lse = m + jnp.log(l)
                if mix:
                    la = l0_ref[0, rows, 16 * h:16 * h + 1]
                    lb = l1_ref[0, rows, 16 * h:16 * h + 1]
                    mx = jnp.maximum(jnp.maximum(la, lb), lse)
                    wa = jnp.exp(la - mx)
                    wb = jnp.exp(lb - mx)
                    wc = jnp.exp(lse - mx)
                    den = wa + wb + wc
                    o = ((wa / den) * o0_ref[0, rows, cols].astype(F32)
                         + (wb / den) * o1_ref[0, rows, cols].astype(F32)
                         + (wc / den) * o)
                else:
                    lse_all = jnp.where(lane // 16 == h, lse, lse_all)
                halves.append(o)
            o_ref[0, rows, cols] = jnp.where(lo, halves[0], halves[1]).astype(o_ref.dtype)
        if not mix:
            lse_ref[0, rows, :] = lse_all


def _dilated_pattern(qkv, batch, seq, dil, bias, prev):
    n = batch * seq
    sub = seq // dil
    tq = min(sub, 4 * LANES)
    nsub = tq // LANES
    nblk = sub // tq
    hb = tq // A_HALF
    nhalo = sub // A_HALF

    def main(c):
        return pl.BlockSpec((1, 1, tq, A_W), lambda b, r, i: (b, r, i, c))

    def before(c):
        return pl.BlockSpec((1, 1, A_HALF, A_W),
                            lambda b, r, i: (b, r, jnp.maximum(i * hb - 1, 0), c))

    def after(c):
        return pl.BlockSpec((1, 1, A_HALF, A_W),
                            lambda b, r, i: (b, r, jnp.minimum((i + 1) * hb, nhalo - 1), c))

    o_spec = pl.BlockSpec((1, tq, A_W), lambda b, r, i: (b, i, r))
    l_spec = pl.BlockSpec((1, tq, LANES), lambda b, r, i: (b, i, r))
    in_specs = [main(0), before(1), main(1), after(1), before(2), main(2), after(2),
                pl.BlockSpec((A_HEADS, LANES, 2 * LANES), lambda b, r, i: (0, 0, 0))]
    args = [qkv] * 7 + [bias]
    o_shape = jax.ShapeDtypeStruct((batch, sub, dil * A_W), BF16)
    l_shape = jax.ShapeDtypeStruct((batch, sub, dil * LANES), F32)
    if prev is None:
        out_specs, out_shape = [o_spec, l_spec], [o_shape, l_shape]
    else:
        o0, l0, o1, l1 = prev
        in_specs += [o_spec, l_spec, o_spec, l_spec]
        args += [o0.reshape(o_shape.shape), l0.reshape(l_shape.shape),
                 o1.reshape(o_shape.shape), l1.reshape(l_shape.shape)]
        out_specs, out_shape = o_spec, o_shape
    out = pl.pallas_call(
        functools.partial(_dil_kernel, nsub=nsub, mix=prev is not None),
        grid=(batch, dil, nblk),
        in_specs=in_specs,
        out_specs=out_specs,
        out_shape=out_shape,
        scratch_shapes=[pltpu.VMEM((tq + 2 * A_HALF, A_W), BF16),
                        pltpu.VMEM((tq + 2 * A_HALF, A_W), BF16)],
        compiler_params=_params("parallel", "parallel", "parallel"),
        name=f"dilated_d{dil}",
    )(*args)
    if prev is None:
        return out[0].reshape(n, A_W), out[1].reshape(n, LANES)
    return out.reshape(n, A_W)


def _dilated(views, batch, seq, biases):
    o0, l0 = _dilated_pattern(views[0], batch, seq, A_PATTERNS[0][1], biases[0], None)
    o1, l1 = _dilated_pattern(views[1], batch, seq, A_PATTERNS[1][1], biases[1], None)
    return _dilated_pattern(views[2], batch, seq, A_PATTERNS[2][1], biases[2], (o0, l0, o1, l1))


def _na_kernel(q_ref, k0_ref, k1_ref, k2_ref, v0_ref, v1_ref, v2_ref, bias_ref, o_ref):
    tq = q_ref.shape[1]
    lo, hi = _half_masks(tq)
    zero = jnp.zeros((), BF16)
    for p in range(B_HEADS // 2):
        cols = slice(p * LANES, (p + 1) * LANES)
        qp = q_ref[0, :, cols]
        kk = jnp.concatenate([k0_ref[0, :, cols], k1_ref[0, :, cols], k2_ref[0, :, cols]], axis=0)
        vv = jnp.concatenate([v0_ref[0, :, cols], v1_ref[0, :, cols], v2_ref[0, :, cols]], axis=0)
        halves = []
        for hf in range(2):
            qm = jnp.where(lo if hf == 0 else hi, qp, zero)
            s = _dot_nt(qm, kk) + bias_ref[0, 2 * p + hf]
            o, _, _ = _softmax_pv(s, vv)
            halves.append(o)
        o_ref[0, :, cols] = jnp.where(lo, halves[0], halves[1]).astype(o_ref.dtype)


def _natten(pr, batch, seq, bias):
    n = batch * seq
    tq = NA_QROWS * GRID_W
    nblk = seq // tq
    nkb = NA_KROWS // NA_QROWS
    ncol = pr.shape[1] // A_W
    prv = pr.reshape(batch, seq, pr.shape[1])

    def kv_spec(c, j):
        return pl.BlockSpec((1, tq, A_W),
                            lambda b, i: (b, jnp.clip(i - 1, 0, nblk - nkb) + j, c))

    def variant(i):
        return jnp.where(i == 0, 0, jnp.where(i == nblk - 1, 2, 1))

    in_specs = ([pl.BlockSpec((1, tq, A_W), lambda b, i: (b, i, 3))]
                + [kv_spec(4, j) for j in range(nkb)] + [kv_spec(5, j) for j in range(nkb)]
                + [pl.BlockSpec((1, B_HEADS, tq, nkb * tq), lambda b, i: (variant(i), 0, 0, 0))])
    out = pl.pallas_call(
        _na_kernel,
        grid=(batch, nblk),
        in_specs=in_specs,
        out_specs=pl.BlockSpec((1, tq, A_W), lambda b, i: (b, i, 0)),
        out_shape=jax.ShapeDtypeStruct((batch, seq, A_W), BF16),
        compiler_params=_params("parallel", "parallel"),
        name="natten",
    )(*([prv] * (1 + 2 * nkb) + [bias]))
    return out.reshape(n, A_W)


def _gqa_kernel(qt_ref, k_ref, vt_ref, o_ref, st_ref, e_ref, *, tk):
    tq = qt_ref.shape[2]
    nchunk = k_ref.shape[1] // tk
    heads = qt_ref.shape[1] // HEAD_DIM
    second = pl.program_id(1) % 2 == 1
    zeros = jnp.zeros((HEAD_DIM, tq), BF16)
    cols = []
    for h in range(heads):
        qh = qt_ref[0, h * HEAD_DIM:(h + 1) * HEAD_DIM, :]
        cols.append(jnp.where(second, jnp.concatenate([zeros, qh], axis=0),
                              jnp.concatenate([qh, zeros], axis=0)))
    qst = jnp.concatenate(cols, axis=1)
    width = heads * tq
    nchain = width // MXU_N
    qsts = [qst[:, j * MXU_N:(j + 1) * MXU_N] for j in range(nchain)]

    ones = jnp.ones((GQA_ONES_ROWS, tk), BF16)

    def chunk(c):
        return pl.ds(c * tk if isinstance(c, int) else pl.multiple_of(c * tk, tk), tk)

    def scores(c, slot):
        kc = k_ref[0, chunk(c), :]
        top = []
        for j in range(nchain):
            st = _dot(kc, qsts[j])
            st_ref[slot, j] = st
            top.append(jnp.max(st, axis=0, keepdims=True))
        return tuple(top)

    def values(c, slot, acc, alpha):
        vc = jnp.concatenate([vt_ref[0, :, chunk(c)], ones], axis=0)
        return tuple(acc[j] * alpha[j] + _dot(vc, e_ref[slot, j]) for j in range(nchain))

    def softmax(slot, m, top):
        m_new, alpha = [], []
        for j in range(nchain):
            s = st_ref[slot, j]
            mn = jnp.maximum(m[j], top[j])
            e_ref[slot, j] = jnp.exp2((s - mn).astype(BF16))
            alpha.append(jnp.exp2(m[j] - mn))
            m_new.append(mn)
        return tuple(m_new), tuple(alpha)

    def step(c, carry, ahead=True):
        slot, m, acc, a2, a1, t0, t1 = carry[0] % 4, *carry[1:]
        before = max(c - 2, 0) if isinstance(c, int) else jnp.maximum(c - 2, 0)
        other = (slot + 2) % 4
        vc = jnp.concatenate([vt_ref[0, :, chunk(before)], ones], axis=0)
        kc = k_ref[0, chunk(c + 2), :] if ahead else None
        acc_new, t2 = [], []
        for j in range(nchain):
            if ahead:
                st = _dot(kc, qsts[j])
                st_ref[other, j] = st
                t2.append(jnp.max(st, axis=0, keepdims=True))
            acc_new.append(acc[j] * a2[j] + _dot(vc, e_ref[other, j]))
        acc = tuple(acc_new)
        t2 = tuple(t2) if ahead else t1
        m, a0 = softmax(slot, m, t0)
        return carry[0] + 1, m, acc, a1, a0, t1, t2

    def body(i, carry):
        state = (0,) + carry
        for u in range(4):
            state = step(4 * i + u, state)
        return state[1:]

    row = lambda v: tuple(jnp.full((1, MXU_N), v, F32) for _ in range(nchain))
    carry = (row(NEG), tuple(jnp.zeros((HEAD_DIM + GQA_ONES_ROWS, MXU_N), F32) for _ in range(nchain)),
             row(1.0), row(1.0), scores(0, 0), scores(1, 1))
    for slot in (2, 3):
        e_ref[slot] = jnp.zeros(e_ref.shape[1:], BF16)
    carry = lax.fori_loop(0, nchunk // 4 - 1, body, carry)
    state = (0,) + carry
    for u in range(4):
        state = step(nchunk - 4 + u, state, ahead=u < 2)
    _, m, acc, a2, a1, _, _ = state
    acc = values(nchunk - 2, 2, acc, a2)
    acc = values(nchunk - 1, 3, acc, a1)
    o = jnp.concatenate([a[:HEAD_DIM] / a[HEAD_DIM:HEAD_DIM + 1] for a in acc], axis=1)
    ot = jnp.concatenate([o[:, h * tq:(h + 1) * tq] for h in range(heads)], axis=0)
    o_ref[0] = ot.T.astype(o_ref.dtype)


def _gqa(qt, k, vt, batch, seq, tq=512, tk=256):
    n = batch * seq
    gw = (C_Q_HEADS // C_KV_HEADS) * HEAD_DIM
    nchain = (C_Q_HEADS // C_KV_HEADS) * tq // MXU_N
    assert seq % (4 * tk) == 0
    out = pl.pallas_call(
        functools.partial(_gqa_kernel, tk=tk),
        grid=(batch, C_KV_HEADS, seq // tq),
        in_specs=[pl.BlockSpec((1, gw, tq), lambda b, g, i: (b, g, i)),
                  pl.BlockSpec((1, seq, LANES), lambda b, g, i: (b, 0, g // 2)),
                  pl.BlockSpec((1, HEAD_DIM, seq), lambda b, g, i: (b, g, 0))],
        out_specs=pl.BlockSpec((1, tq, gw), lambda b, g, i: (b, i, g)),
        out_shape=jax.ShapeDtypeStruct((batch, seq, C_Q_HEADS * HEAD_DIM), BF16),
        scratch_shapes=[pltpu.VMEM((4, nchain, tk, MXU_N), F32),
                        pltpu.VMEM((4, nchain, tk, MXU_N), BF16)],
        compiler_params=_params("parallel", "parallel", "parallel"),
        name="gqa",
    )(qt, k.reshape(batch, seq, -1), vt)
    return out.reshape(n, C_Q_HEADS * HEAD_DIM)


def _post_kernel(*refs, route):
    if route:
        (x_ref, a1_ref, a2_ref, wo1_ref, wo2_ref, gxa_ref, wq_ref, kv_ref, wox_ref, gffn_ref,
         wr_ref, x2_ref, hn_ref, info_ref) = refs
    else:
        (x_ref, a1_ref, a2_ref, wo1_ref, wo2_ref, gxa_ref, wq_ref, kv_ref, wox_ref, gffn_ref,
         x2_ref, hn_ref) = refs
    x1 = x_ref[...] + _dot(a1_ref[...], wo1_ref[...]) + _dot(a2_ref[...], wo2_ref[...])
    tm = x1.shape[0]
    h = _rms(x1, gxa_ref[...]).astype(BF16)
    q = _dot(h, wq_ref[...]).astype(BF16)
    lo, hi = _half_masks(tm)
    zero = jnp.zeros((), BF16)
    outs = []
    for p in range(X_HEADS // 2):
        qp = q[:, p * LANES:(p + 1) * LANES]
        kk = kv_ref[0, :, p * LANES:(p + 1) * LANES]
        vv = kv_ref[0, :, X_W + p * LANES:X_W + (p + 1) * LANES]
        halves = []
        for hf in range(2):
            qm = jnp.where(lo if hf == 0 else hi, qp, zero)
            o, _, _ = _softmax_pv(_dot_nt(qm, kk), vv)
            halves.append(o)
        outs.append(jnp.where(lo, halves[0], halves[1]).astype(BF16))
    x2 = x1 + _dot(jnp.concatenate(outs, axis=1), wox_ref[...])
    x2_ref[...] = x2
    hf32 = _rms(x2, gffn_ref[...])
    hn_ref[...] = _pack_bf16_pairs(hf32) if route else hf32.astype(BF16)
    if route:
        lane = lax.broadcasted_iota(jnp.int32, (tm, LANES), 1).astype(F32)
        lg = jnp.full((tm, LANES), NEG, F32)
        for ex in range(N_EXPERTS):
            le = jnp.sum(hf32 * wr_ref[ex:ex + 1, :], axis=1, keepdims=True)
            lg = jnp.where(lane == float(ex), le, lg)
        m1 = jnp.max(lg, axis=1, keepdims=True)
        i1 = jnp.min(jnp.where(lg == m1, lane, float(LANES)), axis=1, keepdims=True)
        lg2 = jnp.where(lane == i1, NEG, lg)
        m2 = jnp.max(lg2, axis=1, keepdims=True)
        i2 = jnp.min(jnp.where(lg2 == m2, lane, float(LANES)), axis=1, keepdims=True)
        e = jnp.exp(m2 - m1)
        den = 1.0 + e
        info_ref[...] = (jnp.where(lane == 0.0, i1, 0.0) + jnp.where(lane == 1.0, i2, 0.0)
                         + jnp.where(lane == 2.0, 1.0 / den, 0.0) + jnp.where(lane == 3.0, e / den, 0.0))


def _post(x, a1, a2, c1, c2, wo1, wo2, gxa, wq, kvm, wox, gffn, wr, seq, tm):
    n, d = x.shape
    spb = seq // tm
    row = lambda i: (i, 0)
    const = lambda i: (0, 0)
    in_specs = [pl.BlockSpec((tm, d), row),
                pl.BlockSpec((tm, A_W), lambda i: (i, c1)),
                pl.BlockSpec((tm, A_W), lambda i: (i, c2)),
                pl.BlockSpec((A_W, d), const),
                pl.BlockSpec((A_W, d), const),
                pl.BlockSpec((1, d), const),
                pl.BlockSpec((d, X_W), const),
                pl.BlockSpec((1, N_MEM, 2 * X_W), lambda i: (i // spb, 0, 0)),
                pl.BlockSpec((X_W, d), const),
                pl.BlockSpec((1, d), const)]
    args = [x, a1, a2, wo1, wo2, gxa.reshape(1, d), wq, kvm, wox, gffn.reshape(1, d)]
    hshape, hdtype = ((d, BF16) if wr is None else (d // 2, jnp.uint32))
    out_specs = [pl.BlockSpec((tm, d), row), pl.BlockSpec((tm, hshape), row)]
    out_shape = [jax.ShapeDtypeStruct((n, d), F32), jax.ShapeDtypeStruct((n, hshape), hdtype)]
    if wr is not None:
        in_specs.append(pl.BlockSpec((N_EXPERTS, d), const))
        args.append(wr)
        out_specs.append(pl.BlockSpec((tm, LANES), row))
        out_shape.append(jax.ShapeDtypeStruct((n, LANES), F32))
    return pl.pallas_call(
        functools.partial(_post_kernel, route=wr is not None),
        grid=(n // tm,),
        in_specs=in_specs,
        out_specs=out_specs,
        out_shape=out_shape,
        compiler_params=_params("parallel"),
        name="post_route" if wr is not None else "post",
    )(*args)


def _swiglu_chunk(h, wg, wu, wd):
    g = _dot(h, wg)
    u = _dot(h, wu)
    a = (g / (1.0 + jnp.exp(-g))) * u
    return _dot(a.astype(BF16), wd)


def _ffn_kernel(x_ref, h_ref, wg_ref, wu_ref, wd_ref, o_ref):
    f = pl.program_id(1)
    y = _swiglu_chunk(h_ref[...], wg_ref[...], wu_ref[...], wd_ref[...])

    @pl.when(f == 0)
    def _():
        o_ref[...] = x_ref[...] + y

    @pl.when(f > 0)
    def _():
        o_ref[...] += y


def _ffn(x, hn, wg, wu, wd, tm, tf):
    n, d = x.shape
    ff = wg.shape[1]
    return pl.pallas_call(
        _ffn_kernel,
        grid=(n // tm, ff // tf),
        in_specs=[pl.BlockSpec((tm, d), lambda i, f: (i, 0)),
                  pl.BlockSpec((tm, d), lambda i, f: (i, 0)),
                  pl.BlockSpec((d, tf), lambda i, f: (0, f)),
                  pl.BlockSpec((d, tf), lambda i, f: (0, f)),
                  pl.BlockSpec((tf, d), lambda i, f: (f, 0))],
        out_specs=pl.BlockSpec((tm, d), lambda i, f: (i, 0)),
        out_shape=jax.ShapeDtypeStruct((n, d), F32),
        compiler_params=_params("parallel", "arbitrary"),
        name="ffn",
    )(x, hn, wg, wu, wd)


def _route(info, tmg):
    n = info.shape[0]
    ea = info[:, 0:2].astype(jnp.int32).reshape(-1)
    onehot = (ea[:, None] == jnp.arange(N_EXPERTS, dtype=jnp.int32)[None, :]).astype(jnp.int32)
    csum = jnp.cumsum(onehot, axis=0)
    counts = csum[-1]
    padded = (counts + tmg - 1) // tmg * tmg
    ends = jnp.cumsum(padded)
    starts = ends - padded
    dest = jnp.sum(onehot * (csum - 1 + starts[None, :]), axis=1)
    r_max = 2 * n + N_EXPERTS * tmg
    token = jnp.arange(2 * n, dtype=jnp.int32) // 2
    src = jnp.zeros((r_max,), jnp.int32).at[dest].set(token, unique_indices=True)
    tile_start = jnp.arange(r_max // tmg, dtype=jnp.int32) * tmg
    tile_e = jnp.sum((tile_start[:, None] >= ends[None, :]).astype(jnp.int32), axis=1)
    tile_e = jnp.minimum(tile_e, N_EXPERTS - 1)
    tile_ok = (tile_start < ends[-1]).astype(jnp.int32)
    return src, dest, tile_e, tile_ok


def _row_copy(src_ref, row, dst_ref, r, sem):
    return pltpu.make_async_copy(src_ref.at[pl.ds(row, 1)], dst_ref.at[pl.ds(r, 1)], sem)


def _gather_kernel(idx_ref, src_ref, o_ref, buf, sem):
    i = pl.program_id(0)
    ntile = pl.num_programs(0) - 1
    tg = o_ref.shape[0]

    @pl.when(i < ntile)
    def _():
        slot = i % 2

        def start(pair, c):
            for priority in range(2):
                r = 2 * pair + priority
                _row_copy(src_ref, idx_ref[0, 0, r], buf.at[slot], r, sem.at[slot]).start(priority=priority)
            return c

        lax.fori_loop(0, tg // 2, start, 0, unroll=4)

    @pl.when(i > 0)
    def _():
        slot = (i + 1) % 2

        def wait(r, c):
            _row_copy(src_ref, 0, buf.at[slot], 0, sem.at[slot]).wait()
            return c

        lax.fori_loop(0, tg, wait, 0, unroll=8)
        o_ref[...] = buf[slot]


def _gather_rows(src, idx, tg):
    n, d = src.shape
    r = idx.shape[0]
    ntile = r // tg
    return pl.pallas_call(
        _gather_kernel,
        grid=(ntile + 1,),
        in_specs=[pl.BlockSpec((1, 1, tg), lambda i: (jnp.minimum(i, ntile - 1), 0, 0),
                               memory_space=pltpu.SMEM),
                  pl.BlockSpec(memory_space=pl.ANY)],
        out_specs=pl.BlockSpec((tg, d), lambda i: (jnp.maximum(i - 1, 0), 0)),
        out_shape=jax.ShapeDtypeStruct((r, d), src.dtype),
        scratch_shapes=[pltpu.VMEM((2, tg, d), src.dtype), pltpu.SemaphoreType.DMA((2,))],
        compiler_params=_params("arbitrary"),
        name="moe_gather",
    )(idx.reshape(ntile, 1, tg), src)


def _expert_kernel(te_ref, ok_ref, x_ref, wg_ref, wu_ref, wd_ref, o_ref, hb_ref):
    t = pl.program_id(0)
    f = pl.program_id(1)
    ok = ok_ref[t] == 1

    @pl.when(jnp.logical_and(ok, f == 0))
    def _():
        hb_ref[...] = _unpack_bf16_pairs(x_ref[...])
        o_ref[...] = _swiglu_chunk(hb_ref[...], wg_ref[0], wu_ref[0], wd_ref[0])

    @pl.when(jnp.logical_and(ok, f > 0))
    def _():
        o_ref[...] += _swiglu_chunk(hb_ref[...], wg_ref[0], wu_ref[0], wd_ref[0])

    @pl.when(jnp.logical_and(jnp.logical_not(ok), f == 0))
    def _():
        o_ref[...] = jnp.zeros_like(o_ref)


def _expert_ffn(xs, tile_e, tile_ok, wg, wu, wd, tmg, tf):
    r = xs.shape[0]
    _, d, ff = wg.shape
    grid_spec = pltpu.PrefetchScalarGridSpec(
        num_scalar_prefetch=2,
        grid=(r // tmg, ff // tf),
        in_specs=[pl.BlockSpec((tmg, d // 2), lambda t, f, te, ok: (t, 0)),
                  pl.BlockSpec((1, d, tf), lambda t, f, te, ok: (te[t], 0, f * ok[t])),
                  pl.BlockSpec((1, d, tf), lambda t, f, te, ok: (te[t], 0, f * ok[t])),
                  pl.BlockSpec((1, tf, d), lambda t, f, te, ok: (te[t], f * ok[t], 0))],
        out_specs=pl.BlockSpec((tmg, d), lambda t, f, te, ok: (t, 0)),
        scratch_shapes=[pltpu.VMEM((tmg, d), BF16)],
    )
    return pl.pallas_call(
        _expert_kernel,
        grid_spec=grid_spec,
        out_shape=jax.ShapeDtypeStruct((r, d), F32),
        compiler_params=_params("arbitrary", "arbitrary"),
        name="moe_experts",
    )(tile_e, tile_ok, xs, wg, wu, wd)


def _combine_kernel(*refs, norm):
    if norm:
        d_ref, x_ref, info_ref, ys_ref, gain_ref, o_ref, buf, sem = refs
    else:
        (d_ref, x_ref, info_ref, ys_ref, o_ref, buf, sem), gain_ref = refs, None
    tc = o_ref.shape[0]

    def copies(r):
        return (_row_copy(ys_ref, d_ref[0, 0, 2 * r], buf.at[0], r, sem),
                _row_copy(ys_ref, d_ref[0, 0, 2 * r + 1], buf.at[1], r, sem))

    def start(r, c):
        for priority, cp in enumerate(copies(r)):
            cp.start(priority=priority)
        return c

    def wait(r, c):
        for cp in copies(r):
            cp.wait()
        return c

    lax.fori_loop(0, tc, start, 0, unroll=4)
    lax.fori_loop(0, tc, wait, 0, unroll=4)
    g1 = info_ref[:, 2:3]
    g2 = info_ref[:, 3:4]
    o = x_ref[...] + (g1 * buf[0] + g2 * buf[1])
    o_ref[...] = o if gain_ref is None else _rms(o, gain_ref[...])


def _combine(x, info, dest, ys, gain, tc):
    n, d = x.shape
    in_specs = [pl.BlockSpec((1, 1, 2 * tc), lambda i: (i, 0, 0), memory_space=pltpu.SMEM),
                pl.BlockSpec((tc, d), lambda i: (i, 0)),
                pl.BlockSpec((tc, LANES), lambda i: (i, 0)),
                pl.BlockSpec(memory_space=pl.ANY)]
    args = [dest.reshape(n // tc, 1, 2 * tc), x, info, ys]
    if gain is not None:
        in_specs.append(pl.BlockSpec((1, d), lambda i: (0, 0)))
        args.append(gain.reshape(1, d))
    return pl.pallas_call(
        functools.partial(_combine_kernel, norm=gain is not None),
        grid=(n // tc,),
        in_specs=in_specs,
        out_specs=pl.BlockSpec((tc, d), lambda i: (i, 0)),
        out_shape=jax.ShapeDtypeStruct((n, d), F32),
        scratch_shapes=[pltpu.VMEM((2, tc, d), F32), pltpu.SemaphoreType.DMA(())],
        compiler_params=_params("arbitrary"),
        name="moe_combine",
    )(*args)


def _moe(x2, hf, info, wg, wu, wd, gain, tmg, tf, tg, tc):
    src, dest, tile_e, tile_ok = _route(info, tmg)
    xs = _gather_rows(hf, src, tg)
    ys = _expert_ffn(xs, tile_e, tile_ok, wg, wu, wd, tmg, tf)
    return _combine(x2, info, dest, ys, gain, tc)


def _final_norm_kernel(x_ref, g_ref, o_ref):
    o_ref[...] = _rms(x_ref[...], g_ref[...])


def _final_norm(x, g, tm):
    n, d = x.shape
    return pl.pallas_call(
        _final_norm_kernel,
        grid=(n // tm,),
        in_specs=[pl.BlockSpec((tm, d), lambda i: (i, 0)), pl.BlockSpec((1, d), lambda i: (0, 0))],
        out_specs=pl.BlockSpec((tm, d), lambda i: (i, 0)),
        out_shape=jax.ShapeDtypeStruct((n, d), F32),
        compiler_params=_params("parallel"),
        name="final_norm",
    )(x, g.reshape(1, d))


def _t5_bucket_np(rel):
    nb = N_BUCKETS // 2
    max_exact = nb // 2
    ret = np.where(rel > 0, nb, 0)
    n = np.abs(rel)
    large = max_exact + (np.log(np.maximum(n, 1).astype(np.float32) / max_exact)
                         / math.log(MAX_DISTANCE / max_exact) * (nb - max_exact)).astype(np.int32)
    large = np.minimum(large, nb - 1)
    return ret + np.where(n < max_exact, n, large)


def _dilated_bias(rel_table, dil):
    off = np.arange(2 * LANES)[None, :] - A_HALF - np.arange(LANES)[:, None]
    bucket = _t5_bucket_np(off * dil)
    sel = (bucket[:, :, None] == np.arange(N_BUCKETS)).astype(np.float32)
    bias = jnp.einsum('qkb,bh->hqk', sel, rel_table.astype(F32), precision=lax.Precision.HIGHEST)
    return jnp.where(jnp.asarray(np.abs(off) <= A_HALF)[None], bias, NEG)


def _natten_bias(rpb):
    rows = 4 * NA_KROWS
    c = np.arange(GRID_W)
    cs = np.clip(c - NA_KW // 2, 0, GRID_W - NA_KW)
    col_ok = (c[None, :] >= cs[:, None]) & (c[None, :] < cs[:, None] + NA_KW)
    dc = np.clip(c[None, :] - c[:, None], -(NA_KW - 1), NA_KW - 1) + NA_KW - 1
    col_sel = (dc[:, :, None] == np.arange(2 * NA_KW - 1)).astype(np.float32)
    row_sel, row_ok = [], []
    for r0 in (0, 2 * NA_QROWS, rows - NA_QROWS):
        ks = int(np.clip(r0 - NA_KH // 2, 0, rows - NA_KROWS))
        qr = r0 + np.arange(NA_QROWS)
        kr = ks + np.arange(NA_KROWS)
        rs = np.clip(qr - NA_KH // 2, 0, rows - NA_KH)
        row_ok.append((kr[None, :] >= rs[:, None]) & (kr[None, :] < rs[:, None] + NA_KH))
        dr = kr[None, :] - qr[:, None] + NA_KH - 1
        row_sel.append((dr[:, :, None] == np.arange(2 * NA_KH - 1)).astype(np.float32))
    row_sel = np.stack(row_sel)
    ok = np.stack(row_ok)[:, :, None, :, None] & col_ok[None, None, :, None, :]
    by_col = jnp.einsum('hab,xyb->haxy', rpb.astype(F32), col_sel, precision=lax.Precision.HIGHEST)
    table = jnp.einsum('vqka,haxy->vhqxky', row_sel, by_col, precision=lax.Precision.HIGHEST)
    table = jnp.where(jnp.asarray(ok)[:, None], table, NEG)
    tq = NA_QROWS * GRID_W
    return table.reshape(len(row_ok), rpb.shape[0], tq, NA_KROWS * GRID_W)


def _rope_tables(seq):
    t = jnp.arange(seq)
    half = HEAD_DIM // 2
    freqs = ROPE_THETA ** (-jnp.arange(0, half, 2, dtype=F32) / half)
    ang_r = (t // GRID_W).astype(F32)[:, None] * freqs[None, :]
    ang_c = (t % GRID_W).astype(F32)[:, None] * freqs[None, :]
    cos_h = jnp.concatenate([jnp.cos(ang_r)] * 2 + [jnp.cos(ang_c)] * 2, axis=1)
    sin_h = jnp.concatenate([-jnp.sin(ang_r), jnp.sin(ang_r), -jnp.sin(ang_c), jnp.sin(ang_c)], axis=1)
    return jnp.concatenate([cos_h, cos_h], axis=1), jnp.concatenate([sin_h, sin_h], axis=1)


def _prepare(p):
    w = {}
    ev_in = p['ev_w_in']
    scale = np.ones((ev_in.shape[-1],), np.float32)
    scale[0:A_W] = Q_SCALE
    scale[3 * A_W:4 * A_W] = Q_SCALE
    w['ev_w_in'] = (ev_in * scale).astype(BF16)
    w['ev_w_out'] = p['ev_w_out'].astype(BF16)
    w['ev_w_gate'] = p['ev_w_gate'].astype(BF16)
    w['ev_w_up'] = p['ev_w_up'].astype(BF16)
    w['ev_w_down'] = p['ev_w_down'].astype(BF16)
    w['od_w_in'] = p['od_w_in'].astype(BF16)
    w['od_w_out'] = p['od_w_out'].astype(BF16)
    w['od_q_norm'] = jnp.concatenate([p['od_q_norm']] * 2, axis=-1)[:, None, :]
    w['od_k_norm'] = jnp.concatenate([p['od_k_norm']] * 2, axis=-1)[:, None, :]
    w['od_router'] = jnp.swapaxes(p['od_router'], 1, 2)
    w['od_moe_gate'] = p['od_moe_gate'].astype(BF16)
    w['od_moe_up'] = p['od_moe_up'].astype(BF16)
    w['od_moe_down'] = p['od_moe_down'].astype(BF16)
    w['xa_w_q'] = (p['xa_w_q'] * Q_SCALE).astype(BF16)
    w['xa_w_kv'] = p['xa_w_kv'].astype(BF16)
    w['xa_w_out'] = p['xa_w_out'].astype(BF16)
    w['dil_bias'] = [_dilated_bias(p['rel_table'], dil) for _, dil in A_PATTERNS]
    w['na_bias'] = [_natten_bias(p['ev_na_rpb'][i]) for i in range(p['ev_na_rpb'].shape[0])]
    return w


def _trunk(x, mem, p, w):
    batch, seq, d = x.shape
    n = batch * seq
    xf = x.reshape(n, d)
    memf = mem.reshape(batch * N_MEM, d)
    cos, sin = _rope_tables(seq)
    depth = p['xa_norm'].shape[0]
    for layer in range(depth):
        i = layer // 2
        if layer % 2 == 0:
            pr, v4, v16 = _proj_even(xf, p['ev_norm_mix'][i], w['ev_w_in'][i], seq, tm=512)
            a1 = _dilated([pr.reshape(batch, 1, seq, -1), v4, v16], batch, seq, w['dil_bias'])
            a2 = _natten(pr, batch, seq, w['na_bias'][i])
            c1 = c2 = 0
            wo = w['ev_w_out'][i]
            wr = None
        else:
            q, kk, vv = _proj_odd(xf, p['od_norm_mix'][i], w['od_w_in'][i], w['od_q_norm'][i],
                                  w['od_k_norm'][i], cos, sin, seq, tm=512)
            a1 = a2 = _gqa(q, kk, vv, batch, seq)
            c1, c2 = 0, 1
            wo = w['od_w_out'][i]
            wr = w['od_router'][i]
        kvm = _proj(memf, p['xa_mem_norm'][layer], w['xa_w_kv'][layer], tm=512)
        kvm = kvm.reshape(batch, N_MEM, 2 * X_W)
        gffn = p['ev_norm_ffn'][i] if layer % 2 == 0 else p['od_norm_ffn'][i]
        outs = _post(xf, a1, a2, c1, c2, wo[:A_W], wo[A_W:], p['xa_norm'][layer], w['xa_w_q'][layer],
                     kvm, w['xa_w_out'][layer], gffn, wr, seq, tm=512)
        if layer % 2 == 0:
            x2, hn = outs
            xf = _ffn(x2, hn, w['ev_w_gate'][i], w['ev_w_up'][i], w['ev_w_down'][i], tm=512, tf=1408)
        else:
            x2, hf, info = outs
            gain = p['final_norm'] if layer == depth - 1 else None
            xf = _moe(x2, hf, info, w['od_moe_gate'][i], w['od_moe_up'][i], w['od_moe_down'][i],
                      gain, tmg=1024, tf=512, tg=512, tc=256)
    if depth % 2 == 1:
        xf = _final_norm(xf, p['final_norm'], tm=1024)
    return xf.reshape(batch, seq, d)


def kernel(x_prompt, x_sample, mem_prompt, mem_sample, rel_table, ev_norm_mix, ev_w_in, ev_na_rpb, ev_w_out, ev_norm_ffn, ev_w_gate, ev_w_up, ev_w_down, od_norm_mix, od_w_in, od_q_norm, od_k_norm, od_w_out, od_norm_ffn, od_router, od_moe_gate, od_moe_up, od_moe_down, xa_norm, xa_mem_norm, xa_w_q, xa_w_kv, xa_w_out, final_norm):
    p = dict(rel_table=rel_table,
             ev_norm_mix=ev_norm_mix, ev_w_in=ev_w_in, ev_na_rpb=ev_na_rpb, ev_w_out=ev_w_out,
             ev_norm_ffn=ev_norm_ffn, ev_w_gate=ev_w_gate, ev_w_up=ev_w_up, ev_w_down=ev_w_down,
             od_norm_mix=od_norm_mix, od_w_in=od_w_in, od_q_norm=od_q_norm, od_k_norm=od_k_norm,
             od_w_out=od_w_out, od_norm_ffn=od_norm_ffn, od_router=od_router,
             od_moe_gate=od_moe_gate, od_moe_up=od_moe_up, od_moe_down=od_moe_down,
             xa_norm=xa_norm, xa_mem_norm=xa_mem_norm, xa_w_q=xa_w_q, xa_w_kv=xa_w_kv,
             xa_w_out=xa_w_out, final_norm=final_norm)
    w = _prepare(p)
    return (_trunk(x_prompt, mem_prompt, p, w), _trunk(x_sample, mem_sample, p, w))
```

```python
import functools
import math

import numpy as np
import jax
import jax.numpy as jnp
from jax import lax
from jax.experimental import pallas as pl
from jax.experimental.pallas import tpu as pltpu

F32 = jnp.float32
BF16 = jnp.bfloat16

D_MODEL = 1024
HEAD_DIM = 64
GRID_W = 64
N_MEM = 256
A_HEADS = 8
A_PATTERNS = ((128, 1), (512, 4), (2048, 16))
A_HALF = 64
B_HEADS = 8
NA_KH = 8
NA_KW = 16
C_Q_HEADS = 16
C_KV_HEADS = 4
ROPE_THETA = 10000.0
N_BUCKETS = 32
MAX_DISTANCE = 1024
X_HEADS = 4
N_EXPERTS = 8
EPS = 1e-6
NEG = -1e30
A_W = A_HEADS * HEAD_DIM
X_W = X_HEADS * HEAD_DIM
Q_SCALE = HEAD_DIM ** -0.5
QK_SCALE = Q_SCALE * math.log2(math.e)

LANES = 128
MXU_N = 256
GQA_ONES_ROWS = 16
VMEM_LIMIT = 48 * 1024 * 1024

NA_QROWS = 4
NA_KROWS = 12


def _params(*sem):
    return pltpu.CompilerParams(dimension_semantics=sem, vmem_limit_bytes=VMEM_LIMIT)


def _rms(x, g):
    ms = jnp.mean(x * x, axis=-1, keepdims=True)
    return x * lax.rsqrt(ms + EPS) * g


def _dot(a, b):
    return jnp.dot(a, b, preferred_element_type=F32)


def _dot_nt(a, b):
    return lax.dot_general(a, b, (((1,), (1,)), ((), ())), preferred_element_type=F32)


def _half_masks(rows):
    lane = lax.broadcasted_iota(jnp.int32, (rows, LANES), 1)
    lo = lane < HEAD_DIM
    return lo, jnp.logical_not(lo)


def _pack_bf16_pairs(x):
    half = x.shape[1] // 2
    bits = lax.bitcast_convert_type(x.astype(BF16).astype(F32), jnp.uint32)
    return (bits[:, half:] & jnp.uint32(0xFFFF0000)) | (bits[:, :half] >> 16)


def _unpack_bf16_pairs(p):
    lo = lax.bitcast_convert_type(p << 16, F32).astype(BF16)
    hi = lax.bitcast_convert_type(p & jnp.uint32(0xFFFF0000), F32).astype(BF16)
    return jnp.concatenate([lo, hi], axis=1)


def _softmax_pv(s, v):
    m = jnp.max(s, axis=1, keepdims=True)
    e = jnp.exp(s - m)
    l = jnp.sum(e, axis=1, keepdims=True)
    o = _dot(e.astype(BF16), v) / l
    return o, m, l


def _proj_kernel(x_ref, g_ref, w_ref, o_ref):
    h = _rms(x_ref[...], g_ref[...]).astype(BF16)
    o_ref[...] = _dot(h, w_ref[...]).astype(o_ref.dtype)


def _proj(x, g, w, tm):
    n, d = x.shape
    nout = w.shape[1]
    return pl.pallas_call(
        _proj_kernel,
        grid=(n // tm,),
        in_specs=[pl.BlockSpec((tm, d), lambda i: (i, 0)),
                  pl.BlockSpec((1, d), lambda i: (0, 0)),
                  pl.BlockSpec((d, nout), lambda i: (0, 0))],
        out_specs=pl.BlockSpec((tm, nout), lambda i: (i, 0)),
        out_shape=jax.ShapeDtypeStruct((n, nout), BF16),
        compiler_params=_params("parallel"),
        name="proj",
    )(x, g.reshape(1, d), w)


def _proj_even_kernel(x_ref, g_ref, w_ref, o_ref, *rest):
    views, y_ref = rest[:-1], rest[-1]
    h = _rms(x_ref[...], g_ref[...]).astype(BF16)
    y = _dot(h, w_ref[...])
    o_ref[...] = y.astype(o_ref.dtype)
    tm = y.shape[0]
    ncol = y_ref.shape[0]
    for c in range(ncol):
        y_ref[c] = y[:, c * LANES:(c + 1) * LANES]
    for v_ref in views:
        dil = v_ref.shape[1]
        for r in range(dil):
            for c in range(ncol):
                rows = y_ref[c, pl.ds(r, tm // dil, stride=dil), :]
                v_ref[0, r, :, c * LANES:(c + 1) * LANES] = rows.astype(v_ref.dtype)


def _proj_even(x, g, w, seq, tm):
    n, d = x.shape
    nout = w.shape[1]
    spb = seq // tm
    batch = n // seq
    dils = [dil for _, dil in A_PATTERNS if dil > 1]
    aw = 3 * A_W
    return pl.pallas_call(
        _proj_even_kernel,
        grid=(n // tm,),
        in_specs=[pl.BlockSpec((tm, d), lambda i: (i, 0)),
                  pl.BlockSpec((1, d), lambda i: (0, 0)),
                  pl.BlockSpec((d, nout), lambda i: (0, 0))],
        out_specs=[pl.BlockSpec((tm, nout), lambda i: (i, 0))]
        + [pl.BlockSpec((1, dil, tm // dil, aw), lambda i: (i // spb, 0, i % spb, 0)) for dil in dils],
        out_shape=[jax.ShapeDtypeStruct((n, nout), BF16)]
        + [jax.ShapeDtypeStruct((batch, dil, seq // dil, aw), BF16) for dil in dils],
        scratch_shapes=[pltpu.VMEM((aw // LANES, tm, LANES), F32)],
        compiler_params=_params("parallel"),
        name="proj_even",
    )(x, g.reshape(1, d), w)


def _proj_odd_kernel(x_ref, g_ref, w_ref, qg_ref, kg_ref, cos_ref, sin_ref, q_ref, k_ref, v_ref):
    h = _rms(x_ref[...], g_ref[...]).astype(BF16)
    y = _dot(h, w_ref[...])
    tm = y.shape[0]
    cos = cos_ref[...]
    sin = sin_ref[...]
    lane = lax.broadcasted_iota(jnp.int32, (tm, LANES), 1)
    lo = lane < HEAD_DIM
    first = (lane & 31) < 16

    def norm_rope(c, gain, scale):
        ss = c * c
        s_lo = jnp.sum(jnp.where(lo, ss, 0.0), axis=1, keepdims=True)
        s_hi = jnp.sum(jnp.where(lo, 0.0, ss), axis=1, keepdims=True)
        ms = jnp.where(lo, s_lo, s_hi) * (1.0 / HEAD_DIM)
        c = c * lax.rsqrt(ms + EPS) * gain
        partner = jnp.where(first, pltpu.roll(c, LANES - 16, 1), pltpu.roll(c, 16, 1))
        return (c * cos + partner * sin) * scale

    nq = q_ref.shape[1] // LANES
    nk = k_ref.shape[1] // LANES
    for j in range(nq):
        c = y[:, j * LANES:(j + 1) * LANES]
        q_ref[0, j * LANES:(j + 1) * LANES, :] = norm_rope(c, qg_ref[...], QK_SCALE).T.astype(BF16)
    for j in range(nk):
        c = y[:, (nq + j) * LANES:(nq + j + 1) * LANES]
        k_ref[:, j * LANES:(j + 1) * LANES] = norm_rope(c, kg_ref[...], 1.0).astype(BF16)
    for j in range(nk):
        c = y[:, (nq + nk + j) * LANES:(nq + nk + j + 1) * LANES]
        v_ref[0, j * LANES:(j + 1) * LANES, :] = c.T.astype(BF16)


def _proj_odd(x, g, w, qg, kg, cos, sin, seq, tm):
    n, d = x.shape
    nq = C_Q_HEADS * HEAD_DIM
    nk = C_KV_HEADS * HEAD_DIM
    spb = seq // tm
    batch = n // seq
    return pl.pallas_call(
        _proj_odd_kernel,
        grid=(n // tm,),
        in_specs=[pl.BlockSpec((tm, d), lambda i: (i, 0)),
                  pl.BlockSpec((1, d), lambda i: (0, 0)),
                  pl.BlockSpec((d, nq + 2 * nk), lambda i: (0, 0)),
                  pl.BlockSpec((1, LANES), lambda i: (0, 0)),
                  pl.BlockSpec((1, LANES), lambda i: (0, 0)),
                  pl.BlockSpec((tm, LANES), lambda i: (i % spb, 0)),
                  pl.BlockSpec((tm, LANES), lambda i: (i % spb, 0))],
        out_specs=[pl.BlockSpec((1, nq, tm), lambda i: (i // spb, 0, i % spb)),
                   pl.BlockSpec((tm, nk), lambda i: (i, 0)),
                   pl.BlockSpec((1, nk, tm), lambda i: (i // spb, 0, i % spb))],
        out_shape=[jax.ShapeDtypeStruct((batch, nq, seq), BF16),
                   jax.ShapeDtypeStruct((n, nk), BF16),
                   jax.ShapeDtypeStruct((batch, nk, seq), BF16)],
        compiler_params=_params("parallel"),
        name="proj_odd",
    )(x, g.reshape(1, d), w, qg, kg, cos, sin)


def _dil_kernel(*refs, nsub, mix):
    if mix:
        (q_ref, kp_ref, kc_ref, kn_ref, vp_ref, vc_ref, vn_ref, bias_ref,
         o0_ref, l0_ref, o1_ref, l1_ref, o_ref, kwin, vwin) = refs
    else:
        (q_ref, kp_ref, kc_ref, kn_ref, vp_ref, vc_ref, vn_ref, bias_ref,
         o_ref, lse_ref, kwin, vwin) = refs
    tq = q_ref.shape[2]
    i = pl.program_id(2)
    nblk = pl.num_programs(2)
    kwin[0:A_HALF] = kp_ref[0, 0]
    kwin[A_HALF:A_HALF + tq] = kc_ref[0, 0]
    kwin[A_HALF + tq:] = kn_ref[0, 0]
    vwin[0:A_HALF] = vp_ref[0, 0]
    vwin[A_HALF:A_HALF + tq] = vc_ref[0, 0]
    vwin[A_HALF + tq:] = vn_ref[0, 0]
    lo, hi = _half_masks(LANES)
    lane = lax.broadcasted_iota(jnp.int32, (LANES, LANES), 1)
    kj = lax.broadcasted_iota(jnp.int32, (LANES, 2 * LANES), 1)
    zero = jnp.zeros((), BF16)
    for a in range(nsub):
        rows = slice(a * LANES, (a + 1) * LANES)
        valid = None
        if a == 0:
            valid = kj >= jnp.where(i == 0, A_HALF, 0)
        if a == nsub - 1:
            v2 = kj < jnp.where(i == nblk - 1, LANES + A_HALF, 2 * LANES)
            valid = v2 if valid is None else jnp.logical_and(valid, v2)
        lse_all = jnp.zeros((LANES, LANES), F32)
        for p in range(A_HEADS // 2):
            cols = slice(p * LANES, (p + 1) * LANES)
            qp = q_ref[0, 0, rows, cols]
            kk = kwin[a * LANES:(a + 2) * LANES, cols]
            vv = vwin[a * LANES:(a + 2) * LANES, cols]
            halves = []
            for hf in range(2):
                h = 2 * p + hf
                qm = jnp.where(lo if hf == 0 else hi, qp, zero)
                s = _dot_nt(qm, kk) + bias_ref[h]
                if valid is not None:
                    s = jnp.where(valid, s, NEG)
                o, m, l = _softmax_pv(s, vv)
                lse = m + jnp.log(l)
                if mix:
                    la = l0_ref[0, rows, 16 * h:16 * h + 1]
                    lb = l1_ref[0, rows, 16 * h:16 * h + 1]
                    mx = jnp.maximum(jnp.maximum(la, lb), lse)
                    wa = jnp.exp(la - mx)
                    wb = jnp.exp(lb - mx)
                    wc = jnp.exp(lse - mx)
                    den = wa + wb + wc
                    o = ((wa / den) * o0_ref[0, rows, cols].astype(F32)
                         + (wb / den) * o1_ref[0, rows, cols].astype(F32)
                         + (wc / den) * o)
                else:
                    lse_all = jnp.where(lane // 16 == h, lse, lse_all)
                halves.append(o)
            o_ref[0, rows, cols] = jnp.where(lo, halves[0], halves[1]).astype(o_ref.dtype)
        if not mix:
            lse_ref[0, rows, :] = lse_all


def _dilated_pattern(qkv, batch, seq, dil, bias, prev):
    n = batch * seq
    sub = seq // dil
    tq = min(sub, 4 * LANES)
    nsub = tq // LANES
    nblk = sub // tq
    hb = tq // A_HALF
    nhalo = sub // A_HALF

    def main(c):
        return pl.BlockSpec((1, 1, tq, A_W), lambda b, r, i: (b, r, i, c))

    def before(c):
        return pl.BlockSpec((1, 1, A_HALF, A_W),
                            lambda b, r, i: (b, r, jnp.maximum(i * hb - 1, 0), c))

    def after(c):
        return pl.BlockSpec((1, 1, A_HALF, A_W),
                            lambda b, r, i: (b, r, jnp.minimum((i + 1) * hb, nhalo - 1), c))

    o_spec = pl.BlockSpec((1, tq, A_W), lambda b, r, i: (b, i, r))
    l_spec = pl.BlockSpec((1, tq, LANES), lambda b, r, i: (b, i, r))
    in_specs = [main(0), before(1), main(1), after(1), before(2), main(2), after(2),
                pl.BlockSpec((A_HEADS, LANES, 2 * LANES), lambda b, r, i: (0, 0, 0))]
    args = [qkv] * 7 + [bias]
    o_shape = jax.ShapeDtypeStruct((batch, sub, dil * A_W), BF16)
    l_shape = jax.ShapeDtypeStruct((batch, sub, dil * LANES), F32)
    if prev is None:
        out_specs, out_shape = [o_spec, l_spec], [o_shape, l_shape]
    else:
        o0, l0, o1, l1 = prev
        in_specs += [o_spec, l_spec, o_spec, l_spec]
        args += [o0.reshape(o_shape.shape), l0.reshape(l_shape.shape),
                 o1.reshape(o_shape.shape), l1.reshape(l_shape.shape)]
        out_specs, out_shape = o_spec, o_shape
    out = pl.pallas_call(
        functools.partial(_dil_kernel, nsub=nsub, mix=prev is not None),
        grid=(batch, dil, nblk),
        in_specs=in_specs,
        out_specs=out_specs,
        out_shape=out_shape,
        scratch_shapes=[pltpu.VMEM((tq + 2 * A_HALF, A_W), BF16),
                        pltpu.VMEM((tq + 2 * A_HALF, A_W), BF16)],
        compiler_params=_params("parallel", "parallel", "parallel"),
        name=f"dilated_d{dil}",
    )(*args)
    if prev is None:
        return out[0].reshape(n, A_W), out[1].reshape(n, LANES)
    return out.reshape(n, A_W)


def _dilated(views, batch, seq, biases):
    o0, l0 = _dilated_pattern(views[0], batch, seq, A_PATTERNS[0][1], biases[0], None)
    o1, l1 = _dilated_pattern(views[1], batch, seq, A_PATTERNS[1][1], biases[1], None)
    return _dilated_pattern(views[2], batch, seq, A_PATTERNS[2][1], biases[2], (o0, l0, o1, l1))


def _na_kernel(q_ref, k0_ref, k1_ref, k2_ref, v0_ref, v1_ref, v2_ref, bias_ref, o_ref):
    tq = q_ref.shape[1]
    lo, hi = _half_masks(tq)
    zero = jnp.zeros((), BF16)
    for p in range(B_HEADS // 2):
        cols = slice(p * LANES, (p + 1) * LANES)
        qp = q_ref[0, :, cols]
        kk = jnp.concatenate([k0_ref[0, :, cols], k1_ref[0, :, cols], k2_ref[0, :, cols]], axis=0)
        vv = jnp.concatenate([v0_ref[0, :, cols], v1_ref[0, :, cols], v2_ref[0, :, cols]], axis=0)
        halves = []
        for hf in range(2):
            qm = jnp.where(lo if hf == 0 else hi, qp, zero)
            s = _dot_nt(qm, kk) + bias_ref[0, 2 * p + hf]
            o, _, _ = _softmax_pv(s, vv)
            halves.append(o)
        o_ref[0, :, cols] = jnp.where(lo, halves[0], halves[1]).astype(o_ref.dtype)


def _natten(pr, batch, seq, bias):
    n = batch * seq
    tq = NA_QROWS * GRID_W
    nblk = seq // tq
    nkb = NA_KROWS // NA_QROWS
    ncol = pr.shape[1] // A_W
    prv = pr.reshape(batch, seq, pr.shape[1])

    def kv_spec(c, j):
        return pl.BlockSpec((1, tq, A_W),
                            lambda b, i: (b, jnp.clip(i - 1, 0, nblk - nkb) + j, c))

    def variant(i):
        return jnp.where(i == 0, 0, jnp.where(i == nblk - 1, 2, 1))

    in_specs = ([pl.BlockSpec((1, tq, A_W), lambda b, i: (b, i, 3))]
                + [kv_spec(4, j) for j in range(nkb)] + [kv_spec(5, j) for j in range(nkb)]
                + [pl.BlockSpec((1, B_HEADS, tq, nkb * tq), lambda b, i: (variant(i), 0, 0, 0))])
    out = pl.pallas_call(
        _na_kernel,
        grid=(batch, nblk),
        in_specs=in_specs,
        out_specs=pl.BlockSpec((1, tq, A_W), lambda b, i: (b, i, 0)),
        out_shape=jax.ShapeDtypeStruct((batch, seq, A_W), BF16),
        compiler_params=_params("parallel", "parallel"),
        name="natten",
    )(*([prv] * (1 + 2 * nkb) + [bias]))
    return out.reshape(n, A_W)


def _gqa_kernel(qt_ref, k_ref, vt_ref, o_ref, st_ref, e_ref, *, tk):
    tq = qt_ref.shape[2]
    nchunk = k_ref.shape[1] // tk
    heads = qt_ref.shape[1] // HEAD_DIM
    second = pl.program_id(1) % 2 == 1
    zeros = jnp.zeros((HEAD_DIM, tq), BF16)
    cols = []
    for h in range(heads):
        qh = qt_ref[0, h * HEAD_DIM:(h + 1) * HEAD_DIM, :]
        cols.append(jnp.where(second, jnp.concatenate([zeros, qh], axis=0),
                              jnp.concatenate([qh, zeros], axis=0)))
    qst = jnp.concatenate(cols, axis=1)
    width = heads * tq
    nchain = width // MXU_N
    qsts = [qst[:, j * MXU_N:(j + 1) * MXU_N] for j in range(nchain)]

    ones = jnp.ones((GQA_ONES_ROWS, tk), BF16)

    def chunk(c):
        return pl.ds(c * tk if isinstance(c, int) else pl.multiple_of(c * tk, tk), tk)

    def scores(c, slot):
        kc = k_ref[0, chunk(c), :]
        top = []
        for j in range(nchain):
            st = _dot(kc, qsts[j])
            st_ref[slot, j] = st
            top.append(jnp.max(st, axis=0, keepdims=True))
        return tuple(top)

    def values(c, slot, acc, alpha):
        vc = jnp.concatenate([vt_ref[0, :, chunk(c)], ones], axis=0)
        return tuple(acc[j] * alpha[j] + _dot(vc, e_ref[slot, j]) for j in range(nchain))

    def softmax(slot, m, top):
        m_new, alpha = [], []
        for j in range(nchain):
            s = st_ref[slot, j]
            mn = jnp.maximum(m[j], top[j])
            e_ref[slot, j] = jnp.exp2((s - mn).astype(BF16))
            alpha.append(jnp.exp2(m[j] - mn))
            m_new.append(mn)
        return tuple(m_new), tuple(alpha)

    def step(c, carry, ahead=True):
        slot, m, acc, a2, a1, t0, t1 = carry[0] % 4, *carry[1:]
        before = max(c - 2, 0) if isinstance(c, int) else jnp.maximum(c - 2, 0)
        other = (slot + 2) % 4
        vc = jnp.concatenate([vt_ref[0, :, chunk(before)], ones], axis=0)
        kc = k_ref[0, chunk(c + 2), :] if ahead else None
        acc_new, t2 = [], []
        for j in range(nchain):
            if ahead:
                st = _dot(kc, qsts[j])
                st_ref[other, j] = st
                t2.append(jnp.max(st, axis=0, keepdims=True))
            acc_new.append(acc[j] * a2[j] + _dot(vc, e_ref[other, j]))
        acc = tuple(acc_new)
        t2 = tuple(t2) if ahead else t1
        m, a0 = softmax(slot, m, t0)
        return carry[0] + 1, m, acc, a1, a0, t1, t2

    def body(i, carry):
        state = (0,) + carry
        for u in range(4):
            state = step(4 * i + u, state)
        return state[1:]

    row = lambda v: tuple(jnp.full((1, MXU_N), v, F32) for _ in range(nchain))
    carry = (row(NEG), tuple(jnp.zeros((HEAD_DIM + GQA_ONES_ROWS, MXU_N), F32) for _ in range(nchain)),
             row(1.0), row(1.0), scores(0, 0), scores(1, 1))
    for slot in (2, 3):
        e_ref[slot] = jnp.zeros(e_ref.shape[1:], BF16)
    carry = lax.fori_loop(0, nchunk // 4 - 1, body, carry)
    state = (0,) + carry
    for u in range(4):
        state = step(nchunk - 4 + u, state, ahead=u < 2)
    _, m, acc, a2, a1, _, _ = state
    acc = values(nchunk - 2, 2, acc, a2)
    acc = values(nchunk - 1, 3, acc, a1)
    o = jnp.concatenate([a[:HEAD_DIM] / a[HEAD_DIM:HEAD_DIM + 1] for a in acc], axis=1)
    ot = jnp.concatenate([o[:, h * tq:(h + 1) * tq] for h in range(heads)], axis=0)
    o_ref[0] = ot.T.astype(o_ref.dtype)


def _gqa(qt, k, vt, batch, seq, tq=1024, tk=256):
    n = batch * seq
    gw = (C_Q_HEADS // C_KV_HEADS) * HEAD_DIM
    nchain = (C_Q_HEADS // C_KV_HEADS) * tq // MXU_N
    assert seq % (4 * tk) == 0
    out = pl.pallas_call(
        functools.partial(_gqa_kernel, tk=tk),
        grid=(batch, C_KV_HEADS, seq // tq),
        in_specs=[pl.BlockSpec((1, gw, tq), lambda b, g, i: (b, g, i)),
                  pl.BlockSpec((1, seq, LANES), lambda b, g, i: (b, 0, g // 2)),
                  pl.BlockSpec((1, HEAD_DIM, seq), lambda b, g, i: (b, g, 0))],
        out_specs=pl.BlockSpec((1, tq, gw), lambda b, g, i: (b, i, g)),
        out_shape=jax.ShapeDtypeStruct((batch, seq, C_Q_HEADS * HEAD_DIM), BF16),
        scratch_shapes=[pltpu.VMEM((4, nchain, tk, MXU_N), F32),
                        pltpu.VMEM((4, nchain, tk, MXU_N), BF16)],
        compiler_params=_params("parallel", "parallel", "parallel"),
        name="gqa",
    )(qt, k.reshape(batch, seq, -1), vt)
    return out.reshape(n, C_Q_HEADS * HEAD_DIM)


def _post_kernel(*refs, route):
    if route:
        (x_ref, a1_ref, a2_ref, wo1_ref, wo2_ref, gxa_ref, wq_ref, kv_ref, wox_ref, gffn_ref,
         wr_ref, x2_ref, hn_ref, info_ref) = refs
    else:
        (x_ref, a1_ref, a2_ref, wo1_ref, wo2_ref, gxa_ref, wq_ref, kv_ref, wox_ref, gffn_ref,
         x2_ref, hn_ref) = refs
    x1 = x_ref[...] + _dot(a1_ref[...], wo1_ref[...]) + _dot(a2_ref[...], wo2_ref[...])
    tm = x1.shape[0]
    h = _rms(x1, gxa_ref[...]).astype(BF16)
    q = _dot(h, wq_ref[...]).astype(BF16)
    lo, hi = _half_masks(tm)
    zero = jnp.zeros((), BF16)
    outs = []
    for p in range(X_HEADS // 2):
        qp = q[:, p * LANES:(p + 1) * LANES]
        kk = kv_ref[0, :, p * LANES:(p + 1) * LANES]
        vv = kv_ref[0, :, X_W + p * LANES:X_W + (p + 1) * LANES]
        halves = []
        for hf in range(2):
            qm = jnp.where(lo if hf == 0 else hi, qp, zero)
            o, _, _ = _softmax_pv(_dot_nt(qm, kk), vv)
            halves.append(o)
        outs.append(jnp.where(lo, halves[0], halves[1]).astype(BF16))
    x2 = x1 + _dot(jnp.concatenate(outs, axis=1), wox_ref[...])
    x2_ref[...] = x2
    hf32 = _rms(x2, gffn_ref[...])
    hn_ref[...] = _pack_bf16_pairs(hf32) if route else hf32.astype(BF16)
    if route:
        lane = lax.broadcasted_iota(jnp.int32, (tm, LANES), 1).astype(F32)
        lg = jnp.full((tm, LANES), NEG, F32)
        for ex in range(N_EXPERTS):
            le = jnp.sum(hf32 * wr_ref[ex:ex + 1, :], axis=1, keepdims=True)
            lg = jnp.where(lane == float(ex), le, lg)
        m1 = jnp.max(lg, axis=1, keepdims=True)
        i1 = jnp.min(jnp.where(lg == m1, lane, float(LANES)), axis=1, keepdims=True)
        lg2 = jnp.where(lane == i1, NEG, lg)
        m2 = jnp.max(lg2, axis=1, keepdims=True)
        i2 = jnp.min(jnp.where(lg2 == m2, lane, float(LANES)), axis=1, keepdims=True)
        e = jnp.exp(m2 - m1)
        den = 1.0 + e
        info_ref[...] = (jnp.where(lane == 0.0, i1, 0.0) + jnp.where(lane == 1.0, i2, 0.0)
                         + jnp.where(lane == 2.0, 1.0 / den, 0.0) + jnp.where(lane == 3.0, e / den, 0.0))


def _post(x, a1, a2, c1, c2, wo1, wo2, gxa, wq, kvm, wox, gffn, wr, seq, tm):
    n, d = x.shape
    spb = seq // tm
    row = lambda i: (i, 0)
    const = lambda i: (0, 0)
    in_specs = [pl.BlockSpec((tm, d), row),
                pl.BlockSpec((tm, A_W), lambda i: (i, c1)),
                pl.BlockSpec((tm, A_W), lambda i: (i, c2)),
                pl.BlockSpec((A_W, d), const),
                pl.BlockSpec((A_W, d), const),
                pl.BlockSpec((1, d), const),
                pl.BlockSpec((d, X_W), const),
                pl.BlockSpec((1, N_MEM, 2 * X_W), lambda i: (i // spb, 0, 0)),
                pl.BlockSpec((X_W, d), const),
                pl.BlockSpec((1, d), const)]
    args = [x, a1, a2, wo1, wo2, gxa.reshape(1, d), wq, kvm, wox, gffn.reshape(1, d)]
    hshape, hdtype = ((d, BF16) if wr is None else (d // 2, jnp.uint32))
    out_specs = [pl.BlockSpec((tm, d), row), pl.BlockSpec((tm, hshape), row)]
    out_shape = [jax.ShapeDtypeStruct((n, d), F32), jax.ShapeDtypeStruct((n, hshape), hdtype)]
    if wr is not None:
        in_specs.append(pl.BlockSpec((N_EXPERTS, d), const))
        args.append(wr)
        out_specs.append(pl.BlockSpec((tm, LANES), row))
        out_shape.append(jax.ShapeDtypeStruct((n, LANES), F32))
    return pl.pallas_call(
        functools.partial(_post_kernel, route=wr is not None),
        grid=(n // tm,),
        in_specs=in_specs,
        out_specs=out_specs,
        out_shape=out_shape,
        compiler_params=_params("parallel"),
        name="post_route" if wr is not None else "post",
    )(*args)


def _swiglu_chunk(h, wg, wu, wd):
    g = _dot(h, wg)
    u = _dot(h, wu)
    a = (g / (1.0 + jnp.exp(-g))) * u
    return _dot(a.astype(BF16), wd)


def _ffn_kernel(x_ref, h_ref, wg_ref, wu_ref, wd_ref, o_ref):
    f = pl.program_id(1)
    y = _swiglu_chunk(h_ref[...], wg_ref[...], wu_ref[...], wd_ref[...])

    @pl.when(f == 0)
    def _():
        o_ref[...] = x_ref[...] + y

    @pl.when(f > 0)
    def _():
        o_ref[...] += y


def _ffn(x, hn, wg, wu, wd, tm, tf):
    n, d = x.shape
    ff = wg.shape[1]
    return pl.pallas_call(
        _ffn_kernel,
        grid=(n // tm, ff // tf),
        in_specs=[pl.BlockSpec((tm, d), lambda i, f: (i, 0)),
                  pl.BlockSpec((tm, d), lambda i, f: (i, 0)),
                  pl.BlockSpec((d, tf), lambda i, f: (0, f)),
                  pl.BlockSpec((d, tf), lambda i, f: (0, f)),
                  pl.BlockSpec((tf, d), lambda i, f: (f, 0))],
        out_specs=pl.BlockSpec((tm, d), lambda i, f: (i, 0)),
        out_shape=jax.ShapeDtypeStruct((n, d), F32),
        compiler_params=_params("parallel", "arbitrary"),
        name="ffn",
    )(x, hn, wg, wu, wd)


def _route(info, tmg):
    n = info.shape[0]
    ea = info[:, 0:2].astype(jnp.int32).reshape(-1)
    onehot = (ea[:, None] == jnp.arange(N_EXPERTS, dtype=jnp.int32)[None, :]).astype(jnp.int32)
    csum = jnp.cumsum(onehot, axis=0)
    counts = csum[-1]
    padded = (counts + tmg - 1) // tmg * tmg
    ends = jnp.cumsum(padded)
    starts = ends - padded
    dest = jnp.sum(onehot * (csum - 1 + starts[None, :]), axis=1)
    r_max = 2 * n + N_EXPERTS * tmg
    token = jnp.arange(2 * n, dtype=jnp.int32) // 2
    src = jnp.zeros((r_max,), jnp.int32).at[dest].set(token, unique_indices=True)
    tile_start = jnp.arange(r_max // tmg, dtype=jnp.int32) * tmg
    tile_e = jnp.sum((tile_start[:, None] >= ends[None, :]).astype(jnp.int32), axis=1)
    tile_e = jnp.minimum(tile_e, N_EXPERTS - 1)
    tile_ok = (tile_start < ends[-1]).astype(jnp.int32)
    return src, dest, tile_e, tile_ok


def _row_copy(src_ref, row, dst_ref, r, sem):
    return pltpu.make_async_copy(src_ref.at[pl.ds(row, 1)], dst_ref.at[pl.ds(r, 1)], sem)


def _gather_kernel(idx_ref, src_ref, o_ref, buf, sem):
    i = pl.program_id(0)
    ntile = pl.num_programs(0) - 1
    tg = o_ref.shape[0]

    @pl.when(i < ntile)
    def _():
        slot = i % 2

        def start(pair, c):
            for priority in range(2):
                r = 2 * pair + priority
                _row_copy(src_ref, idx_ref[0, 0, r], buf.at[slot], r, sem.at[slot]).start(priority=priority)
            return c

        lax.fori_loop(0, tg // 2, start, 0, unroll=4)

    @pl.when(i > 0)
    def _():
        slot = (i + 1) % 2

        def wait(r, c):
            _row_copy(src_ref, 0, buf.at[slot], 0, sem.at[slot]).wait()
            return c

        lax.fori_loop(0, tg, wait, 0, unroll=8)
        o_ref[...] = buf[slot]


def _gather_rows(src, idx, tg):
    n, d = src.shape
    r = idx.shape[0]
    ntile = r // tg
    return pl.pallas_call(
        _gather_kernel,
        grid=(ntile + 1,),
        in_specs=[pl.BlockSpec((1, 1, tg), lambda i: (jnp.minimum(i, ntile - 1), 0, 0),
                               memory_space=pltpu.SMEM),
                  pl.BlockSpec(memory_space=pl.ANY)],
        out_specs=pl.BlockSpec((tg, d), lambda i: (jnp.maximum(i - 1, 0), 0)),
        out_shape=jax.ShapeDtypeStruct((r, d), src.dtype),
        scratch_shapes=[pltpu.VMEM((2, tg, d), src.dtype), pltpu.SemaphoreType.DMA((2,))],
        compiler_params=_params("arbitrary"),
        name="moe_gather",
    )(idx.reshape(ntile, 1, tg), src)


def _expert_kernel(te_ref, ok_ref, x_ref, wg_ref, wu_ref, wd_ref, o_ref, hb_ref):
    t = pl.program_id(0)
    f = pl.program_id(1)
    ok = ok_ref[t] == 1

    @pl.when(jnp.logical_and(ok, f == 0))
    def _():
        hb_ref[...] = _unpack_bf16_pairs(x_ref[...])
        o_ref[...] = _swiglu_chunk(hb_ref[...], wg_ref[0], wu_ref[0], wd_ref[0])

    @pl.when(jnp.logical_and(ok, f > 0))
    def _():
        o_ref[...] += _swiglu_chunk(hb_ref[...], wg_ref[0], wu_ref[0], wd_ref[0])

    @pl.when(jnp.logical_and(jnp.logical_not(ok), f == 0))
    def _():
        o_ref[...] = jnp.zeros_like(o_ref)


def _expert_ffn(xs, tile_e, tile_ok, wg, wu, wd, tmg, tf):
    r = xs.shape[0]
    _, d, ff = wg.shape
    grid_spec = pltpu.PrefetchScalarGridSpec(
        num_scalar_prefetch=2,
        grid=(r // tmg, ff // tf),
        in_specs=[pl.BlockSpec((tmg, d // 2), lambda t, f, te, ok: (t, 0)),
                  pl.BlockSpec((1, d, tf), lambda t, f, te, ok: (te[t], 0, f * ok[t])),
                  pl.BlockSpec((1, d, tf), lambda t, f, te, ok: (te[t], 0, f * ok[t])),
                  pl.BlockSpec((1, tf, d), lambda t, f, te, ok: (te[t], f * ok[t], 0))],
        out_specs=pl.BlockSpec((tmg, d), lambda t, f, te, ok: (t, 0)),
        scratch_shapes=[pltpu.VMEM((tmg, d), BF16)],
    )
    return pl.pallas_call(
        _expert_kernel,
        grid_spec=grid_spec,
        out_shape=jax.ShapeDtypeStruct((r, d), F32),
        compiler_params=_params("arbitrary", "arbitrary"),
        name="moe_experts",
    )(tile_e, tile_ok, xs, wg, wu, wd)


def _combine_kernel(*refs, norm):
    if norm:
        d_ref, x_ref, info_ref, ys_ref, gain_ref, o_ref, buf, sem = refs
    else:
        (d_ref, x_ref, info_ref, ys_ref, o_ref, buf, sem), gain_ref = refs, None
    tc = o_ref.shape[0]

    def copies(r):
        return (_row_copy(ys_ref, d_ref[0, 0, 2 * r], buf.at[0], r, sem),
                _row_copy(ys_ref, d_ref[0, 0, 2 * r + 1], buf.at[1], r, sem))

    def start(r, c):
        for priority, cp in enumerate(copies(r)):
            cp.start(priority=priority)
        return c

    def wait(r, c):
        for cp in copies(r):
            cp.wait()
        return c

    lax.fori_loop(0, tc, start, 0, unroll=4)
    lax.fori_loop(0, tc, wait, 0, unroll=4)
    g1 = info_ref[:, 2:3]
    g2 = info_ref[:, 3:4]
    o = x_ref[...] + (g1 * buf[0] + g2 * buf[1])
    o_ref[...] = o if gain_ref is None else _rms(o, gain_ref[...])


def _combine(x, info, dest, ys, gain, tc):
    n, d = x.shape
    in_specs = [pl.BlockSpec((1, 1, 2 * tc), lambda i: (i, 0, 0), memory_space=pltpu.SMEM),
                pl.BlockSpec((tc, d), lambda i: (i, 0)),
                pl.BlockSpec((tc, LANES), lambda i: (i, 0)),
                pl.BlockSpec(memory_space=pl.ANY)]
    args = [dest.reshape(n // tc, 1, 2 * tc), x, info, ys]
    if gain is not None:
        in_specs.append(pl.BlockSpec((1, d), lambda i: (0, 0)))
        args.append(gain.reshape(1, d))
    return pl.pallas_call(
        functools.partial(_combine_kernel, norm=gain is not None),
        grid=(n // tc,),
        in_specs=in_specs,
        out_specs=pl.BlockSpec((tc, d), lambda i: (i, 0)),
        out_shape=jax.ShapeDtypeStruct((n, d), F32),
        scratch_shapes=[pltpu.VMEM((2, tc, d), F32), pltpu.SemaphoreType.DMA(())],
        compiler_params=_params("arbitrary"),
        name="moe_combine",
    )(*args)


def _moe(x2, hf, info, wg, wu, wd, gain, tmg, tf, tg, tc):
    src, dest, tile_e, tile_ok = _route(info, tmg)
    xs = _gather_rows(hf, src, tg)
    ys = _expert_ffn(xs, tile_e, tile_ok, wg, wu, wd, tmg, tf)
    return _combine(x2, info, dest, ys, gain, tc)


def _final_norm_kernel(x_ref, g_ref, o_ref):
    o_ref[...] = _rms(x_ref[...], g_ref[...])


def _final_norm(x, g, tm):
    n, d = x.shape
    return pl.pallas_call(
        _final_norm_kernel,
        grid=(n // tm,),
        in_specs=[pl.BlockSpec((tm, d), lambda i: (i, 0)), pl.BlockSpec((1, d), lambda i: (0, 0))],
        out_specs=pl.BlockSpec((tm, d), lambda i: (i, 0)),
        out_shape=jax.ShapeDtypeStruct((n, d), F32),
        compiler_params=_params("parallel"),
        name="final_norm",
    )(x, g.reshape(1, d))


def _t5_bucket_np(rel):
    nb = N_BUCKETS // 2
    max_exact = nb // 2
    ret = np.where(rel > 0, nb, 0)
    n = np.abs(rel)
    large = max_exact + (np.log(np.maximum(n, 1).astype(np.float32) / max_exact)
                         / math.log(MAX_DISTANCE / max_exact) * (nb - max_exact)).astype(np.int32)
    large = np.minimum(large, nb - 1)
    return ret + np.where(n < max_exact, n, large)


def _dilated_bias(rel_table, dil):
    off = np.arange(2 * LANES)[None, :] - A_HALF - np.arange(LANES)[:, None]
    bucket = _t5_bucket_np(off * dil)
    sel = (bucket[:, :, None] == np.arange(N_BUCKETS)).astype(np.float32)
    bias = jnp.einsum('qkb,bh->hqk', sel, rel_table.astype(F32), precision=lax.Precision.HIGHEST)
    return jnp.where(jnp.asarray(np.abs(off) <= A_HALF)[None], bias, NEG)


def _natten_bias(rpb):
    rows = 4 * NA_KROWS
    c = np.arange(GRID_W)
    cs = np.clip(c - NA_KW // 2, 0, GRID_W - NA_KW)
    col_ok = (c[None, :] >= cs[:, None]) & (c[None, :] < cs[:, None] + NA_KW)
    dc = np.clip(c[None, :] - c[:, None], -(NA_KW - 1), NA_KW - 1) + NA_KW - 1
    col_sel = (dc[:, :, None] == np.arange(2 * NA_KW - 1)).astype(np.float32)
    row_sel, row_ok = [], []
    for r0 in (0, 2 * NA_QROWS, rows - NA_QROWS):
        ks = int(np.clip(r0 - NA_KH // 2, 0, rows - NA_KROWS))
        qr = r0 + np.arange(NA_QROWS)
        kr = ks + np.arange(NA_KROWS)
        rs = np.clip(qr - NA_KH // 2, 0, rows - NA_KH)
        row_ok.append((kr[None, :] >= rs[:, None]) & (kr[None, :] < rs[:, None] + NA_KH))
        dr = kr[None, :] - qr[:, None] + NA_KH - 1
        row_sel.append((dr[:, :, None] == np.arange(2 * NA_KH - 1)).astype(np.float32))
    row_sel = np.stack(row_sel)
    ok = np.stack(row_ok)[:, :, None, :, None] & col_ok[None, None, :, None, :]
    by_col = jnp.einsum('hab,xyb->haxy', rpb.astype(F32), col_sel, precision=lax.Precision.HIGHEST)
    table = jnp.einsum('vqka,haxy->vhqxky', row_sel, by_col, precision=lax.Precision.HIGHEST)
    table = jnp.where(jnp.asarray(ok)[:, None], table, NEG)
    tq = NA_QROWS * GRID_W
    return table.reshape(len(row_ok), rpb.shape[0], tq, NA_KROWS * GRID_W)


def _rope_tables(seq):
    t = jnp.arange(seq)
    half = HEAD_DIM // 2
    freqs = ROPE_THETA ** (-jnp.arange(0, half, 2, dtype=F32) / half)
    ang_r = (t // GRID_W).astype(F32)[:, None] * freqs[None, :]
    ang_c = (t % GRID_W).astype(F32)[:, None] * freqs[None, :]
    cos_h = jnp.concatenate([jnp.cos(ang_r)] * 2 + [jnp.cos(ang_c)] * 2, axis=1)
    sin_h = jnp.concatenate([-jnp.sin(ang_r), jnp.sin(ang_r), -jnp.sin(ang_c), jnp.sin(ang_c)], axis=1)
    return jnp.concatenate([cos_h, cos_h], axis=1), jnp.concatenate([sin_h, sin_h], axis=1)


def _prepare(p):
    w = {}
    ev_in = p['ev_w_in']
    scale = np.ones((ev_in.shape[-1],), np.float32)
    scale[0:A_W] = Q_SCALE
    scale[3 * A_W:4 * A_W] = Q_SCALE
    w['ev_w_in'] = (ev_in * scale).astype(BF16)
    w['ev_w_out'] = p['ev_w_out'].astype(BF16)
    w['ev_w_gate'] = p['ev_w_gate'].astype(BF16)
    w['ev_w_up'] = p['ev_w_up'].astype(BF16)
    w['ev_w_down'] = p['ev_w_down'].astype(BF16)
    w['od_w_in'] = p['od_w_in'].astype(BF16)
    w['od_w_out'] = p['od_w_out'].astype(BF16)
    w['od_q_norm'] = jnp.concatenate([p['od_q_norm']] * 2, axis=-1)[:, None, :]
    w['od_k_norm'] = jnp.concatenate([p['od_k_norm']] * 2, axis=-1)[:, None, :]
    w['od_router'] = jnp.swapaxes(p['od_router'], 1, 2)
    w['od_moe_gate'] = p['od_moe_gate'].astype(BF16)
    w['od_moe_up'] = p['od_moe_up'].astype(BF16)
    w['od_moe_down'] = p['od_moe_down'].astype(BF16)
    w['xa_w_q'] = (p['xa_w_q'] * Q_SCALE).astype(BF16)
    w['xa_w_kv'] = p['xa_w_kv'].astype(BF16)
    w['xa_w_out'] = p['xa_w_out'].astype(BF16)
    w['dil_bias'] = [_dilated_bias(p['rel_table'], dil) for _, dil in A_PATTERNS]
    w['na_bias'] = [_natten_bias(p['ev_na_rpb'][i]) for i in range(p['ev_na_rpb'].shape[0])]
    return w


def _trunk(x, mem, p, w):
    batch, seq, d = x.shape
    n = batch * seq
    xf = x.reshape(n, d)
    memf = mem.reshape(batch * N_MEM, d)
    cos, sin = _rope_tables(seq)
    depth = p['xa_norm'].shape[0]
    for layer in range(depth):
        i = layer // 2
        if layer % 2 == 0:
            pr, v4, v16 = _proj_even(xf, p['ev_norm_mix'][i], w['ev_w_in'][i], seq, tm=512)
            a1 = _dilated([pr.reshape(batch, 1, seq, -1), v4, v16], batch, seq, w['dil_bias'])
            a2 = _natten(pr, batch, seq, w['na_bias'][i])
            c1 = c2 = 0
            wo = w['ev_w_out'][i]
            wr = None
        else:
            q, kk, vv = _proj_odd(xf, p['od_norm_mix'][i], w['od_w_in'][i], w['od_q_norm'][i],
                                  w['od_k_norm'][i], cos, sin, seq, tm=512)
            a1 = a2 = _gqa(q, kk, vv, batch, seq)
            c1, c2 = 0, 1
            wo = w['od_w_out'][i]
            wr = w['od_router'][i]
        kvm = _proj(memf, p['xa_mem_norm'][layer], w['xa_w_kv'][layer], tm=512)
        kvm = kvm.reshape(batch, N_MEM, 2 * X_W)
        gffn = p['ev_norm_ffn'][i] if layer % 2 == 0 else p['od_norm_ffn'][i]
        outs = _post(xf, a1, a2, c1, c2, wo[:A_W], wo[A_W:], p['xa_norm'][layer], w['xa_w_q'][layer],
                     kvm, w['xa_w_out'][layer], gffn, wr, seq, tm=512)
        if layer % 2 == 0:
            x2, hn = outs
            xf = _ffn(x2, hn, w['ev_w_gate'][i], w['ev_w_up'][i], w['ev_w_down'][i], tm=512, tf=1408)
        else:
            x2, hf, info = outs
            gain = p['final_norm'] if layer == depth - 1 else None
            xf = _moe(x2, hf, info, w['od_moe_gate'][i], w['od_moe_up'][i], w['od_moe_down'][i],
                      gain, tmg=1024, tf=512, tg=512, tc=256)
    if depth % 2 == 1:
        xf = _final_norm(xf, p['final_norm'], tm=1024)
    return xf.reshape(batch, seq, d)


def kernel(x_prompt, x_sample, mem_prompt, mem_sample, rel_table, ev_norm_mix, ev_w_in, ev_na_rpb, ev_w_out, ev_norm_ffn, ev_w_gate, ev_w_up, ev_w_down, od_norm_mix, od_w_in, od_q_norm, od_k_norm, od_w_out, od_norm_ffn, od_router, od_moe_gate, od_moe_up, od_moe_down, xa_norm, xa_mem_norm, xa_w_q, xa_w_kv, xa_w_out, final_norm):
    p = dict(rel_table=rel_table,
             ev_norm_mix=ev_norm_mix, ev_w_in=ev_w_in, ev_na_rpb=ev_na_rpb, ev_w_out=ev_w_out,
             ev_norm_ffn=ev_norm_ffn, ev_w_gate=ev_w_gate, ev_w_up=ev_w_up, ev_w_down=ev_w_down,
             od_norm_mix=od_norm_mix, od_w_in=od_w_in, od_q_norm=od_q_norm, od_k_norm=od_k_norm,
             od_w_out=od_w_out, od_norm_ffn=od_norm_ffn, od_router=od_router,
             od_moe_gate=od_moe_gate, od_moe_up=od_moe_up, od_moe_down=od_moe_down,
             xa_norm=xa_norm, xa_mem_norm=xa_mem_norm, xa_w_q=xa_w_q, xa_w_kv=xa_w_kv,
             xa_w_out=xa_w_out, final_norm=final_norm)
    w = _prepare(p)
    return (_trunk(x_prompt, mem_prompt, p, w), _trunk(x_sample, mem_sample, p, w))
```
